```python
import jax
import jax.numpy as jnp
from jax import lax
import numpy as np

D_MODEL = 1024
BATCH = 8
SEQ = 2048
DEPTH = 2
DEC_BATCH = 32
DEC_SEQ = 8
PAST_LEN = 16384
PAGE_SIZE = 128

N_MIXERS = 2
N_LAYERS_A = (DEPTH + 1) // 2
N_LAYERS_B = DEPTH // 2
EPS = 1e-6
L2_EPS = 1e-6

H_A = 8
DK_A = 128
DV_A = 128
WIDTH_A = H_A * DV_A
CONV_W = 4
CONV_DIM = H_A * (2 * DK_A + DV_A)
CHUNK = 64
W_IN_A = CONV_DIM + WIDTH_A + 2 * H_A

H_B = 8
DH_B = 128
DIL_GROUPS = ((128, 1), (512, 4), (2048, 16))
N_GROUPS = len(DIL_GROUPS)
WIDTH_B = H_B * DH_B
W_IN_B = 3 * N_GROUPS * WIDTH_B + WIDTH_B
ATTN_SCALE = DH_B ** -0.5

kernel_name = 'hybrid_gdn_dilated_window_step'


def rms_norm(x, g):
    xf = x.astype(jnp.float32)
    y = xf * lax.rsqrt(jnp.mean(xf * xf, axis=-1, keepdims=True) + EPS)
    return (y * g.astype(jnp.float32)).astype(x.dtype)


def l2_normalize(x):
    xf = x.astype(jnp.float32)
    return xf * lax.rsqrt(jnp.sum(xf * xf, axis=-1, keepdims=True) + L2_EPS)


def ada_modulate(x, c, norm_g, ada_w, ada_b):
    mod = jnp.einsum('bd,de->be', jax.nn.silu(c), ada_w) + ada_b
    shift, scale, gate = jnp.split(mod[:, None, :], 3, axis=-1)
    h = rms_norm(x, norm_g) * (1 + scale) + shift
    return h, gate


def causal_conv_silu(u, buf, w):
    L = u.shape[1]
    ext = jnp.concatenate([buf.astype(u.dtype), u], axis=1)
    y = ext[:, 0:L] * w[0]
    for i in range(1, CONV_W):
        y = y + ext[:, i:i + L] * w[i]
    return jax.nn.silu(y), ext[:, L:]


def gated_delta_rule(q, k, v, g, beta, S0):
    B, L, H, DK = q.shape
    DV = v.shape[-1]
    C = min(CHUNK, L)
    pad = (-L) % C
    if pad:
        pw = ((0, 0), (0, pad), (0, 0), (0, 0))
        q, k, v = jnp.pad(q, pw), jnp.pad(k, pw), jnp.pad(v, pw)
        g, beta = jnp.pad(g, pw[:3]), jnp.pad(beta, pw[:3])
    n = (L + pad) // C

    def blocks(t):
        t = t.reshape((B, n, C, H) + t.shape[3:])
        return jnp.moveaxis(t, (1, 3), (0, 2))

    qc, kc, vc, bc = blocks(q), blocks(k), blocks(v), blocks(beta)
    gc = jnp.cumsum(blocks(g), axis=-1)
    pos = jnp.arange(C)
    tril = pos[:, None] >= pos[None, :]
    strict = pos[:, None] > pos[None, :]
    diff = gc[..., :, None] - gc[..., None, :]
    decay = jnp.where(tril, jnp.exp(jnp.where(tril, diff, 0.0)), 0.0)
    kb = kc * bc[..., None]
    n_mat = jnp.where(strict, jnp.einsum('nbhik,nbhjk->nbhij', kb, kc) * decay, 0.0)
    a_mat = n_mat + jnp.eye(C, dtype=n_mat.dtype)
    rhs = jnp.concatenate([vc * bc[..., None], kb * jnp.exp(gc)[..., None]], axis=-1)
    sol = lax.linalg.triangular_solve(a_mat, rhs, left_side=True, lower=True, unit_diagonal=True)
    w_val, k_cum = sol[..., :DV], sol[..., DV:]
    qk = jnp.einsum('nbhik,nbhjk->nbhij', qc, kc) * decay
    q_dec = qc * jnp.exp(gc)[..., None]
    k_dec = kc * jnp.exp(gc[..., -1:] - gc)[..., None]
    g_tot = jnp.exp(gc[..., -1])

    def step(S, xs):
        w_c, kcum_c, qk_c, qdec_c, kdec_c, gtot_c = xs
        u = w_c - jnp.einsum('bhck,bhkv->bhcv', kcum_c, S)
        o = jnp.einsum('bhck,bhkv->bhcv', qdec_c, S) + jnp.einsum('bhij,bhjv->bhiv', qk_c, u)
        S = S * gtot_c[..., None, None] + jnp.einsum('bhck,bhcv->bhkv', kdec_c, u)
        return S, o

    S, o = lax.scan(step, S0, (w_val, k_cum, qk, q_dec, k_dec, g_tot))
    o = jnp.moveaxis(o, (0, 2), (1, 3)).reshape(B, n * C, H, DV)[:, :L]
    return o, S


def mixer_a(h, conv_buf, S0, w_in, conv_w, A_log, dt_bias, out_g, w_out):
    B, L, _ = h.shape
    proj = jnp.einsum('bld,de->ble', h, w_in)
    qkv = proj[..., :CONV_DIM]
    z = proj[..., CONV_DIM:CONV_DIM + WIDTH_A]
    a = proj[..., CONV_DIM + WIDTH_A:CONV_DIM + WIDTH_A + H_A]
    b = proj[..., CONV_DIM + WIDTH_A + H_A:]
    qkv, new_buf = causal_conv_silu(qkv, conv_buf, conv_w)
    q = l2_normalize(qkv[..., :H_A * DK_A].reshape(B, L, H_A, DK_A)) * (DK_A ** -0.5)
    k = l2_normalize(qkv[..., H_A * DK_A:2 * H_A * DK_A].reshape(B, L, H_A, DK_A))
    v = qkv[..., 2 * H_A * DK_A:].reshape(B, L, H_A, DV_A).astype(jnp.float32)
    g = -jnp.exp(A_log.astype(jnp.float32)) * jax.nn.softplus(a.astype(jnp.float32) + dt_bias.astype(jnp.float32))
    beta = jax.nn.sigmoid(b.astype(jnp.float32))
    o, S = gated_delta_rule(q, k, v, g, beta, S0.astype(jnp.float32))
    o = rms_norm(o, out_g).reshape(B, L, WIDTH_A) * jax.nn.silu(z.astype(jnp.float32))
    y = jnp.einsum('ble,ed->bld', o.astype(h.dtype), w_out)
    return y, new_buf, S


def dilated_attn_prompt(q, k, v, window, dil):
    B, S, H, Dh = q.shape
    n = window // dil
    Sd = S // dil
    blk = n
    nb = -(-Sd // blk)
    Sp = nb * blk

    def sub(t):
        t = t.reshape(B, Sd, dil, H, Dh).transpose(0, 2, 1, 3, 4)
        t = jnp.pad(t, ((0, 0), (0, 0), (0, Sp - Sd), (0, 0), (0, 0)))
        return t.reshape(B, dil, nb, blk, H, Dh)

    def with_prev(t):
        prev = jnp.pad(t, ((0, 0), (0, 0), (1, 0), (0, 0), (0, 0), (0, 0)))[:, :, :-1]
        return jnp.concatenate([prev, t], axis=3)

    qs = sub(q).astype(jnp.float32)
    kk = with_prev(sub(k)).astype(jnp.float32)
    vv = with_prev(sub(v)).astype(jnp.float32)
    s = jnp.einsum('brnqhe,brnkhe->brnhqk', qs, kk) * ATTN_SCALE
    qi = jnp.arange(nb)[:, None, None] * blk + jnp.arange(blk)[None, :, None]
    ki = jnp.arange(nb)[:, None, None] * blk - blk + jnp.arange(2 * blk)[None, None, :]
    rel = qi - ki
    valid = (rel >= 0) & (rel <= n) & (ki >= 0)
    s = jnp.where(valid[:, None], s, -jnp.inf)
    m = jnp.max(s, axis=-1, keepdims=True)
    p = jnp.exp(s - m)
    l = jnp.sum(p, axis=-1, keepdims=True)
    o = jnp.einsum('brnhqk,brnkhe->brnqhe', p / l, vv)
    lse = (m + jnp.log(l))[..., 0].transpose(0, 1, 2, 4, 3)
    o = o.reshape(B, dil, Sp, H, Dh)[:, :, :Sd].transpose(0, 2, 1, 3, 4).reshape(B, S, H, Dh)
    lse = lse.reshape(B, dil, Sp, H)[:, :, :Sd].transpose(0, 2, 1, 3).reshape(B, S, H)
    return o, lse


def dilated_attn_step(q, k, v, kv_buf, window, dil):
    Bd, L, H, Dh = q.shape
    n = window // dil
    Lbuf = kv_buf.shape[1]
    k_all = jnp.concatenate([kv_buf[:, :, 0].astype(k.dtype), k], axis=1)
    v_all = jnp.concatenate([kv_buf[:, :, 1].astype(v.dtype), v], axis=1)
    idx = Lbuf + jnp.arange(L)[:, None] - dil * jnp.arange(n + 1)[None, :]
    valid = idx >= 0
    idx = jnp.maximum(idx, 0)
    kg = k_all[:, idx].astype(jnp.float32)
    vg = v_all[:, idx].astype(jnp.float32)
    s = jnp.einsum('blhe,blmhe->blhm', q.astype(jnp.float32), kg) * ATTN_SCALE
    s = jnp.where(valid[:, None, :], s, -jnp.inf)
    m = jnp.max(s, axis=-1, keepdims=True)
    p = jnp.exp(s - m)
    l = jnp.sum(p, axis=-1, keepdims=True)
    o = jnp.einsum('blhm,blmhe->blhe', p / l, vg)
    lse = (m + jnp.log(l))[..., 0]
    return o, lse


def mixer_b(h, kv_bufs, w_in, w_out):
    B, L, _ = h.shape
    proj = jnp.einsum('bld,de->ble', h, w_in)
    qkv = proj[..., :3 * N_GROUPS * WIDTH_B].reshape(B, L, 3, N_GROUPS, H_B, DH_B)
    z = proj[..., 3 * N_GROUPS * WIDTH_B:]
    outs, lses, new_kv = [], [], []
    for gi, (window, dil) in enumerate(DIL_GROUPS):
        q, k, v = qkv[:, :, 0, gi], qkv[:, :, 1, gi], qkv[:, :, 2, gi]
        if kv_bufs is None:
            o, lse = dilated_attn_prompt(q, k, v, window, dil)
            keep = min(window, L)
            new_kv.append(jnp.stack([k[:, L - keep:], v[:, L - keep:]], axis=2))
        else:
            o, lse = dilated_attn_step(q, k, v, kv_bufs[gi], window, dil)
            new_kv.append(jnp.stack([k, v], axis=2))
        outs.append(o)
        lses.append(lse)
    wts = jax.nn.softmax(jnp.stack(lses, axis=0), axis=0)
    o = jnp.sum(wts[..., None] * jnp.stack(outs, axis=0), axis=0)
    o = o.reshape(B, L, WIDTH_B) * jax.nn.silu(z.astype(jnp.float32))
    y = jnp.einsum('ble,ed->bld', o.astype(h.dtype), w_out)
    return y, new_kv


def setup_inputs(seed: int = 0) -> dict:
    key = jax.random.key(seed)
    ks = jax.random.split(key, 24)
    f32 = jnp.float32
    n_buf = [min(w, PAST_LEN) for (w, _) in DIL_GROUPS]
    dt = jnp.exp(jax.random.uniform(ks[14], (N_LAYERS_A, H_A), f32) * (np.log(0.1) - np.log(0.001)) + np.log(0.001))
    return {
        'x_prompt': jax.random.normal(ks[0], (BATCH, SEQ, D_MODEL), f32),
        'x_sample': jax.random.normal(ks[1], (DEC_BATCH, DEC_SEQ, D_MODEL), f32),
        'state_delta': 0.1 * jax.random.normal(ks[2], (N_LAYERS_A, DEC_BATCH, H_A, DK_A, DV_A), f32),
        'state_conv': jax.random.normal(ks[3], (N_LAYERS_A, DEC_BATCH, CONV_W - 1, CONV_DIM), f32),
        'cache_kv_w128': jax.random.normal(ks[4], (N_LAYERS_B, DEC_BATCH, n_buf[0], 2, H_B, DH_B), f32),
        'cache_kv_w512': jax.random.normal(ks[5], (N_LAYERS_B, DEC_BATCH, n_buf[1], 2, H_B, DH_B), f32),
        'cache_kv_w2048': jax.random.normal(ks[6], (N_LAYERS_B, DEC_BATCH, n_buf[2], 2, H_B, DH_B), f32),
        'c_prompt': jax.random.normal(ks[7], (BATCH, D_MODEL), f32),
        'c_sample': jax.random.normal(ks[8], (DEC_BATCH, D_MODEL), f32),
        'norm_g': 1.0 + 0.02 * jax.random.normal(ks[9], (DEPTH, D_MODEL), f32),
        'ada_w': 0.5 * D_MODEL ** -0.5 * jax.random.normal(ks[10], (DEPTH, D_MODEL, 3 * D_MODEL), f32),
        'ada_b': 0.01 * jax.random.normal(ks[11], (DEPTH, 3 * D_MODEL), f32),
        'a_w_in': D_MODEL ** -0.5 * jax.random.normal(ks[12], (N_LAYERS_A, D_MODEL, W_IN_A), f32),
        'a_conv_w': CONV_W ** -0.5 * jax.random.normal(ks[13], (N_LAYERS_A, CONV_W, CONV_DIM), f32),
        'a_A_log': jnp.log(jax.random.uniform(ks[15], (N_LAYERS_A, H_A), f32, 1.0, 16.0)),
        'a_dt_bias': dt + jnp.log(-jnp.expm1(-dt)),
        'a_out_norm_g': 1.0 + 0.02 * jax.random.normal(ks[16], (N_LAYERS_A, DV_A), f32),
        'a_w_out': WIDTH_A ** -0.5 * jax.random.normal(ks[17], (N_LAYERS_A, WIDTH_A, D_MODEL), f32),
        'b_w_in': D_MODEL ** -0.5 * jax.random.normal(ks[18], (N_LAYERS_B, D_MODEL, W_IN_B), f32),
        'b_w_out': WIDTH_B ** -0.5 * jax.random.normal(ks[19], (N_LAYERS_B, WIDTH_B, D_MODEL), f32),
        'final_norm_g': 1.0 + 0.02 * jax.random.normal(ks[20], (D_MODEL,), f32),
    }


def reference(x_prompt, x_sample, state_delta, state_conv, cache_kv_w128, cache_kv_w512, cache_kv_w2048,
              c_prompt, c_sample, norm_g, ada_w, ada_b, a_w_in, a_conv_w, a_A_log, a_dt_bias, a_out_norm_g,
              a_w_out, b_w_in, b_w_out, final_norm_g):
    kv_caches = (cache_kv_w128, cache_kv_w512, cache_kv_w2048)
    xp, xs = x_prompt, x_sample
    bp = xp.shape[0]
    delta_p, delta_s, conv_p, conv_s = [], [], [], []
    kv_p = [[] for _ in DIL_GROUPS]
    kv_s = [[] for _ in DIL_GROUPS]
    for layer in range(DEPTH):
        hp, gate_p = ada_modulate(xp, c_prompt, norm_g[layer], ada_w[layer], ada_b[layer])
        hs, gate_s = ada_modulate(xs, c_sample, norm_g[layer], ada_w[layer], ada_b[layer])
        i = layer // N_MIXERS
        if layer % N_MIXERS == 0:
            params = (a_w_in[i], a_conv_w[i], a_A_log[i], a_dt_bias[i], a_out_norm_g[i], a_w_out[i])
            zero_buf = jnp.zeros((bp, CONV_W - 1, CONV_DIM), xp.dtype)
            zero_S = jnp.zeros((bp, H_A, DK_A, DV_A), jnp.float32)
            yp, buf_p, S_p = mixer_a(hp, zero_buf, zero_S, *params)
            ys, buf_s, S_s = mixer_a(hs, state_conv[i], state_delta[i], *params)
            delta_p.append(S_p.astype(state_delta.dtype))
            delta_s.append(S_s.astype(state_delta.dtype))
            conv_p.append(buf_p)
            conv_s.append(buf_s)
        else:
            yp, new_p = mixer_b(hp, None, b_w_in[i], b_w_out[i])
            ys, new_s = mixer_b(hs, [cache[i] for cache in kv_caches], b_w_in[i], b_w_out[i])
            for gi in range(N_GROUPS):
                kv_p[gi].append(new_p[gi])
                kv_s[gi].append(new_s[gi])
        xp = xp + gate_p * yp
        xs = xs + gate_s * ys
    y_prompt = rms_norm(xp, final_norm_g)
    y_sample = rms_norm(xs, final_norm_g)
    return (y_prompt, y_sample,
            jnp.stack(delta_p), jnp.stack(delta_s), jnp.stack(conv_p), jnp.stack(conv_s),
            jnp.stack(kv_p[0]), jnp.stack(kv_s[0]), jnp.stack(kv_p[1]), jnp.stack(kv_s[1]),
            jnp.stack(kv_p[2]), jnp.stack(kv_s[2]))
```

```python
import functools

import jax
import jax.numpy as jnp
from jax import lax
from jax.experimental import pallas as pl
from jax.experimental.pallas import tpu as pltpu

F32 = jnp.float32
BF16 = jnp.bfloat16

D_MODEL = 1024
N_HEADS = 8
D_HEAD = 128
WIDTH = N_HEADS * D_HEAD
CONV_W = 4
CONV_DIM = 3 * WIDTH
DIL_GROUPS = ((128, 1), (512, 4), (2048, 16))
KEYS_PER_QUERY = 129
EPS = 1e-6
ATTN_SCALE = D_HEAD ** -0.5
NEG_BIG = -1e30

LANES = 128
SUBLANES = 8
VMEM_LIMIT_BYTES = 56 * 1024 * 1024


def _cparams(semantics):
    return pltpu.CompilerParams(dimension_semantics=semantics,
                                vmem_limit_bytes=VMEM_LIMIT_BYTES)


def _silu(x):
    return x * (1.0 / (1.0 + jnp.exp(-x)))


def _softplus(x):
    return jnp.maximum(x, 0.0) + jnp.log1p(jnp.exp(-jnp.abs(x)))


def _dot(a, b):
    return jnp.dot(a.astype(BF16), b.astype(BF16), preferred_element_type=F32)


def _dot_nt(a, b):
    return lax.dot_general(a.astype(BF16), b.astype(BF16), (((1,), (1,)), ((), ())),
                           preferred_element_type=F32)


def _dot_tn(a, b):
    return lax.dot_general(a.astype(BF16), b.astype(BF16), (((0,), (0,)), ((), ())),
                           preferred_element_type=F32)


def _split3(x):
    x1 = x.astype(BF16)
    r = x - x1.astype(F32)
    x2 = r.astype(BF16)
    x3 = (r - x2.astype(F32)).astype(BF16)
    return x1, x2, x3


def _dot_sel(sel, x):
    s = sel.astype(BF16)
    out = None
    for p in _split3(x):
        t = jnp.dot(s, p, preferred_element_type=F32)
        out = t if out is None else out + t
    return out


def _dot_x_sel(x, sel):
    s = sel.astype(BF16)
    out = None
    for p in _split3(x):
        t = jnp.dot(p, s, preferred_element_type=F32)
        out = t if out is None else out + t
    return out


def _mod_kernel(c_ref, w_ref, b_ref, o_ref):
    s = _silu(c_ref[...])
    o_ref[...] = _dot(s, w_ref[...]) + b_ref[...]


def _ada_mod(c_all, ada_w, ada_b):
    n_layers = ada_w.shape[0]
    rows = c_all.shape[0]
    tn = 768
    return pl.pallas_call(
        _mod_kernel,
        grid=(n_layers, 3 * D_MODEL // tn),
        in_specs=[
            pl.BlockSpec((rows, D_MODEL), lambda l, j: (0, 0)),
            pl.BlockSpec((None, D_MODEL, tn), lambda l, j: (l, 0, j)),
            pl.BlockSpec((None, 1, tn), lambda l, j: (l, 0, j)),
        ],
        out_specs=pl.BlockSpec((None, rows, tn), lambda l, j: (l, 0, j)),
        out_shape=jax.ShapeDtypeStruct((n_layers, rows, 3 * D_MODEL), F32),
        compiler_params=_cparams(("arbitrary", "arbitrary")),
        name="ada_mod",
    )(c_all, ada_w, ada_b.reshape(n_layers, 1, 3 * D_MODEL))


def _proj_kernel(seg_bounds, has_ab, *refs):
    x_ref, mod_ref, g_ref, w_ref = refs[:4]
    pos = 4
    if has_ab:
        wab_ref, wabt_ref = refs[4:6]
        pos = 6
    n_seg = len(seg_bounds)
    seg_refs = refs[pos:pos + n_seg]
    pos += n_seg
    if has_ab:
        ab_ref, abt_ref = refs[pos:pos + 2]
        pos += 2
    h_ref = refs[pos]
    j = pl.program_id(2)

    @pl.when(j == 0)
    def _():
        x = x_ref[...]
        ms = jnp.mean(x * x, axis=-1, keepdims=True)
        y = x * lax.rsqrt(ms + EPS) * g_ref[...]
        shift = mod_ref[:, 0:D_MODEL]
        scale = mod_ref[:, D_MODEL:2 * D_MODEL]
        hb = (y * (1.0 + scale) + shift).astype(BF16)
        h_ref[...] = hb
        if has_ab:
            ab_ref[...] = jnp.dot(hb, wab_ref[...], preferred_element_type=F32)
            abt_ref[...] = lax.dot_general(wabt_ref[...], hb, (((1,), (1,)), ((), ())),
                                           preferred_element_type=F32)

    res = jnp.dot(h_ref[...], w_ref[...], preferred_element_type=F32)
    for (lo, hi), o_ref in zip(seg_bounds, seg_refs):
        @pl.when((j >= lo) & (j < hi))
        def _(o_ref=o_ref):
            o_ref[...] = res


def _project(x, mod, norm_g, w, seg_slabs, tm, w_ab=None, w_abt=None):
    n, t, _ = x.shape
    r = mod.shape[1]
    slab = 1024
    n_slab = w.shape[1] // slab
    assert sum(seg_slabs) == n_slab and t % tm == 0
    has_ab = w_ab is not None
    bounds, lo = [], 0
    for s in seg_slabs:
        bounds.append((lo, lo + s))
        lo += s
    mod_rows = 1 if r == 1 else tm
    mod_map = (lambda b, i, j: (b, 0, 0)) if r == 1 else (lambda b, i, j: (b, i, 0))
    in_specs = [
        pl.BlockSpec((None, tm, D_MODEL), lambda b, i, j: (b, i, 0)),
        pl.BlockSpec((None, mod_rows, 3 * D_MODEL), mod_map),
        pl.BlockSpec((1, D_MODEL), lambda b, i, j: (0, 0)),
        pl.BlockSpec((D_MODEL, slab), lambda b, i, j: (0, j)),
    ]
    args = [x, mod, norm_g.reshape(1, D_MODEL), w]
    if has_ab:
        in_specs += [pl.BlockSpec((D_MODEL, LANES), lambda b, i, j: (0, 0)),
                     pl.BlockSpec((16, D_MODEL), lambda b, i, j: (0, 0))]
        args += [w_ab, w_abt]
    out_specs, out_shapes = [], []
    for (lo, hi) in bounds:
        def seg_map(b, i, j, lo=lo, hi=hi):
            return (b, i, jnp.clip(j - lo, 0, hi - lo - 1))
        out_specs.append(pl.BlockSpec((None, tm, slab), seg_map))
        out_shapes.append(jax.ShapeDtypeStruct((n, t, slab * (hi - lo)), F32))
    if has_ab:
        out_specs += [pl.BlockSpec((None, tm, LANES), lambda b, i, j: (b, i, 0)),
                      pl.BlockSpec((None, 16, tm), lambda b, i, j: (b, 0, i))]
        out_shapes += [jax.ShapeDtypeStruct((n, t, LANES), F32),
                       jax.ShapeDtypeStruct((n, 16, t), F32)]
    return pl.pallas_call(
        functools.partial(_proj_kernel, tuple(bounds), has_ab),
        grid=(n, t // tm, n_slab),
        in_specs=in_specs,
        out_specs=out_specs,
        out_shape=out_shapes,
        scratch_shapes=[pltpu.VMEM((tm, D_MODEL), BF16)],
        compiler_params=_cparams(("arbitrary", "arbitrary", "arbitrary")),
        name="norm_proj",
    )(*args)


def _out_kernel(final_norm, x_ref, o_ref, mod_ref, w_ref, fg_ref, out_ref):
    y = _dot(o_ref[...], w_ref[...])
    gate = mod_ref[:, 2 * D_MODEL:3 * D_MODEL]
    x = x_ref[...] + gate * y
    if final_norm:
        ms = jnp.mean(x * x, axis=-1, keepdims=True)
        x = x * lax.rsqrt(ms + EPS) * fg_ref[...]
    out_ref[...] = x


def _out_project(x, o, mod, w_out, final_g, tm, final_norm):
    n, t, _ = x.shape
    r = mod.shape[1]
    mod_rows = 1 if r == 1 else tm
    mod_map = (lambda b, i: (b, 0, 0)) if r == 1 else (lambda b, i: (b, i, 0))
    return pl.pallas_call(
        functools.partial(_out_kernel, final_norm),
        grid=(n, t // tm),
        in_specs=[
            pl.BlockSpec((None, tm, D_MODEL), lambda b, i: (b, i, 0)),
            pl.BlockSpec((None, tm, WIDTH), lambda b, i: (b, i, 0)),
            pl.BlockSpec((None, mod_rows, 3 * D_MODEL), mod_map),
            pl.BlockSpec((WIDTH, D_MODEL), lambda b, i: (0, 0)),
            pl.BlockSpec((1, D_MODEL), lambda b, i: (0, 0)),
        ],
        out_specs=pl.BlockSpec((None, tm, D_MODEL), lambda b, i: (b, i, 0)),
        out_shape=jax.ShapeDtypeStruct((n, t, D_MODEL), F32),
        compiler_params=_cparams(("arbitrary", "arbitrary")),
        name="out_proj",
    )(x, o, mod, w_out, final_g.reshape(1, D_MODEL))


def _unit_lower_inverse(n_mat, c):
    row = lax.broadcasted_iota(jnp.int32, (c, c), 0)
    col = lax.broadcasted_iota(jnp.int32, (c, c), 1)
    base = min(16, c)
    blk_id = lambda v, size: jnp.right_shift(v, size.bit_length() - 1)
    eye = (row == col).astype(F32)
    same = blk_id(row, base) == blk_id(col, base)
    p = jnp.where(same, -n_mat, 0.0)
    inv = eye + p
    size = 2
    while size < base:
        p = _dot(p, p)
        inv = inv + _dot(p, inv)
        size *= 2
    size = base
    while size < c:
        same_next = blk_id(row, 2 * size) == blk_id(col, 2 * size)
        l_mat = jnp.where(same_next & jnp.logical_not(same), n_mat, 0.0)
        inv = inv - _dot(_dot(inv, l_mat), inv)
        same = same_next
        size *= 2
    return inv


def _gdn_kernel(tm, c, has_state, *refs):
    (qkv_ref, z_ref, ab_ref, abt_ref, cw_ref, alog_ref, dtb_ref, alogc_ref, dtbc_ref,
     og_ref) = refs[:10]
    pos = 10
    if has_state:
        s0_ref, conv0_ref = refs[10:12]
        pos = 12
    o_ref, s_ref, conv_ref, ext_ref, g_ref, gt_ref = refs[pos:pos + 6]
    t = pl.program_id(1)
    pad = SUBLANES
    halo = CONV_W - 1

    @pl.when(t == 0)
    def _():
        if has_state:
            s_ref[...] = s0_ref[...]
            ext_ref[pad - halo:pad, :] = conv0_ref[...]
        else:
            s_ref[...] = jnp.zeros_like(s_ref)
            ext_ref[pad - halo:pad, :] = jnp.zeros((halo, CONV_DIM), F32)

    ext_ref[pad:pad + tm, :] = qkv_ref[...]

    ab = ab_ref[...]
    g_ref[...] = -jnp.exp(alog_ref[...]) * _softplus(ab + dtb_ref[...])
    abt = abt_ref[...]
    gt_ref[...] = -jnp.exp(alogc_ref[...]) * _softplus(abt + dtbc_ref[...])

    row = lax.broadcasted_iota(jnp.int32, (c, c), 0)
    col = lax.broadcasted_iota(jnp.int32, (c, c), 1)
    tril = row >= col
    strict = row > col
    tril_f = tril.astype(F32)
    triu_f = (row <= col).astype(F32)

    def chunk(ci, carry):
        r0 = ci * c if isinstance(ci, int) else pl.multiple_of(ci * c, c)
        gc_all = _dot_sel(tril_f, g_ref[pl.ds(r0, c), :])
        gct_all = _dot_x_sel(gt_ref[0:N_HEADS, pl.ds(r0, c)], triu_f)
        beta_all = 1.0 / (1.0 + jnp.exp(-ab_ref[pl.ds(r0, c), :]))
        for h in range(N_HEADS):
            def conv(off):
                cols = slice(off + h * D_HEAD, off + (h + 1) * D_HEAD)
                win = ext_ref[pl.ds(r0, c + pad), cols]
                acc = None
                for i in range(CONV_W):
                    lo = pad - halo + i
                    term = win[lo:lo + c] * cw_ref[i:i + 1, cols]
                    acc = term if acc is None else acc + term
                return _silu(acc)

            q = conv(0)
            k = conv(WIDTH)
            v = conv(2 * WIDTH)
            q = q * lax.rsqrt(jnp.sum(q * q, axis=-1, keepdims=True) + EPS) * (D_HEAD ** -0.5)
            k = k * lax.rsqrt(jnp.sum(k * k, axis=-1, keepdims=True) + EPS)
            gc = jnp.broadcast_to(gc_all[:, h:h + 1], (c, D_HEAD))
            beta = jnp.broadcast_to(beta_all[:, N_HEADS + h:N_HEADS + h + 1], (c, D_HEAD))
            g_row = gct_all[h:h + 1, :]
            g_last = gc[c - 1:c, :]
            decay = jnp.where(tril, jnp.exp(jnp.where(tril, gc[:, 0:c] - g_row, 0.0)), 0.0)
            kb = k * beta
            n_mat = jnp.where(strict, _dot_nt(kb, k) * decay, 0.0)
            inv = _unit_lower_inverse(n_mat, c)
            e_gc = jnp.exp(gc)
            rhs = jnp.concatenate([v * beta, kb * e_gc], axis=1)
            sol = _dot(inv, rhs)
            w_val = sol[:, 0:D_HEAD]
            k_cum = sol[:, D_HEAD:2 * D_HEAD]
            qk = _dot_nt(q, k) * decay
            s_h = s_ref[h]
            u = w_val - _dot(k_cum, s_h)
            o = _dot(q * e_gc, s_h) + _dot(qk, u)
            k_dec = k * jnp.exp(g_last - gc)
            s_ref[h] = s_h * jnp.exp(g_last) + _dot_tn(k_dec, u)
            o = o * lax.rsqrt(jnp.mean(o * o, axis=-1, keepdims=True) + EPS) * og_ref[...]
            hc = slice(h * D_HEAD, (h + 1) * D_HEAD)
            o_ref[pl.ds(r0, c), hc] = o * _silu(z_ref[pl.ds(r0, c), hc])
        return carry

    if tm == c:
        chunk(0, 0)
    else:
        lax.fori_loop(0, tm // c, chunk, 0)

    tail = ext_ref[pad + tm - halo:pad + tm, :]
    ext_ref[pad - halo:pad, :] = tail
    conv_ref[...] = tail


def _gdn_core(qkv, z, ab, abt, conv_w, a_log, dt_bias, out_g, tm, c, s0=None, conv0=None):
    b, t, _ = qkv.shape
    has_state = s0 is not None
    alog_row = jnp.zeros((1, LANES), F32).at[0, :N_HEADS].set(a_log)
    dtb_row = jnp.zeros((1, LANES), F32).at[0, :N_HEADS].set(dt_bias)
    alog_col = jnp.zeros((16, 1), F32).at[:N_HEADS, 0].set(a_log)
    dtb_col = jnp.zeros((16, 1), F32).at[:N_HEADS, 0].set(dt_bias)
    const = lambda shape: pl.BlockSpec(shape, lambda i, j: (0,) * len(shape))
    in_specs = [
        pl.BlockSpec((None, tm, CONV_DIM), lambda i, j: (i, j, 0)),
        pl.BlockSpec((None, tm, WIDTH), lambda i, j: (i, j, 0)),
        pl.BlockSpec((None, tm, LANES), lambda i, j: (i, j, 0)),
        pl.BlockSpec((None, 16, tm), lambda i, j: (i, 0, j)),
        const((CONV_W, CONV_DIM)), const((1, LANES)), const((1, LANES)),
        const((16, 1)), const((16, 1)), const((1, D_HEAD)),
    ]
    args = [qkv, z, ab, abt, conv_w, alog_row, dtb_row, alog_col, dtb_col,
            out_g.reshape(1, D_HEAD)]
    if has_state:
        in_specs += [pl.BlockSpec((None, N_HEADS, D_HEAD, D_HEAD), lambda i, j: (i, 0, 0, 0)),
                     pl.BlockSpec((None, CONV_W - 1, CONV_DIM), lambda i, j: (i, 0, 0))]
        args += [s0, conv0]
    return pl.pallas_call(
        functools.partial(_gdn_kernel, tm, c, has_state),
        grid=(b, t // tm),
        in_specs=in_specs,
        out_specs=[
            pl.BlockSpec((None, tm, WIDTH), lambda i, j: (i, j, 0)),
            pl.BlockSpec((None, N_HEADS, D_HEAD, D_HEAD), lambda i, j: (i, 0, 0, 0)),
            pl.BlockSpec((None, CONV_W - 1, CONV_DIM), lambda i, j: (i, 0, 0)),
        ],
        out_shape=[
            jax.ShapeDtypeStruct((b, t, WIDTH), F32),
            jax.ShapeDtypeStruct((b, N_HEADS, D_HEAD, D_HEAD), F32),
            jax.ShapeDtypeStruct((b, CONV_W - 1, CONV_DIM), F32),
        ],
        scratch_shapes=[
            pltpu.VMEM((tm + SUBLANES, CONV_DIM), F32),
            pltpu.VMEM((tm, LANES), F32),
            pltpu.VMEM((16, tm), F32),
        ],
        compiler_params=_cparams(("arbitrary", "arbitrary")),
        name="gdn_core",
    )(*args)


def _attn_prompt_kernel(seq, *refs):
    q_refs = refs[0:3]
    k_refs = refs[3:9:2]
    v_refs = refs[4:9:2]
    z_ref, o_ref, acc_ref, m_ref, l_ref = refs[9:14]
    blk = D_HEAD
    row = lax.broadcasted_iota(jnp.int32, (blk, blk), 0)
    col = lax.broadcasted_iota(jnp.int32, (blk, blk), 1)
    cur_mask = col <= row

    for g, (window, dil) in enumerate(DIL_GROUPS):
        q_ref, k_ref, v_ref = q_refs[g], k_refs[g], v_refs[g]
        n_blk = seq // dil // blk

        def rows(start, dil=dil):
            if dil == 1:
                return pl.ds(start, blk)
            return pl.ds(start, blk, stride=dil)

        def body(i, carry, g=g, dil=dil, n_blk=n_blk, q_ref=q_ref, k_ref=k_ref, v_ref=v_ref,
                 rows=rows):
            r = i // n_blk
            n = i % n_blk
            start = r + dil * blk * n
            q = q_ref[rows(start), :]
            s_cur = _dot_nt(q, k_ref[rows(start), :]) * ATTN_SCALE
            s_cur = jnp.where(cur_mask, s_cur, NEG_BIG)
            m = jnp.max(s_cur, axis=-1, keepdims=True)
            if n_blk > 1:
                start_p = r + dil * blk * jnp.maximum(n - 1, 0)
                pmask = col >= row + jnp.where(n > 0, 0, blk)
                s_prev = _dot_nt(q, k_ref[rows(start_p), :]) * ATTN_SCALE
                s_prev = jnp.where(pmask, s_prev, NEG_BIG)
                m = jnp.maximum(m, jnp.max(s_prev, axis=-1, keepdims=True))
            p_cur = jnp.where(cur_mask, jnp.exp(s_cur - m), 0.0)
            l = jnp.sum(p_cur, axis=-1, keepdims=True)
            acc = _dot(p_cur, v_ref[rows(start), :])
            if n_blk > 1:
                p_prev = jnp.where(pmask, jnp.exp(s_prev - m), 0.0)
                l = l + jnp.sum(p_prev, axis=-1, keepdims=True)
                acc = acc + _dot(p_prev, v_ref[rows(start_p), :])
            acc_ref[g, rows(start), :] = acc
            m_ref[g, rows(start), :] = jnp.broadcast_to(m, (blk, blk))
            l_ref[g, rows(start), :] = jnp.broadcast_to(l, (blk, blk))
            return carry

        lax.fori_loop(0, dil * n_blk, body, 0)

    m_all = jnp.maximum(jnp.maximum(m_ref[0], m_ref[1]), m_ref[2])
    num = jnp.zeros((seq, blk), F32)
    den = jnp.zeros((seq, blk), F32)
    for g in range(3):
        w = jnp.exp(m_ref[g] - m_all)
        num = num + w * acc_ref[g]
        den = den + w * l_ref[g]
    o_ref[...] = num / den * _silu(z_ref[...])


def _attn_prompt(q, kvs, z):
    b, seq, _ = q.shape
    blk_spec = lambda f: pl.BlockSpec((None, seq, D_HEAD), f)
    in_specs = [blk_spec(lambda i, h, g=g: (i, 0, g * N_HEADS + h)) for g in range(3)]
    args = [q, q, q]
    for g in range(3):
        in_specs += [blk_spec(lambda i, h: (i, 0, h)), blk_spec(lambda i, h: (i, 0, N_HEADS + h))]
        args += [kvs[g], kvs[g]]
    in_specs.append(blk_spec(lambda i, h: (i, 0, h)))
    args.append(z)
    return pl.pallas_call(
        functools.partial(_attn_prompt_kernel, seq),
        grid=(b, N_HEADS),
        in_specs=in_specs,
        out_specs=blk_spec(lambda i, h: (i, 0, h)),
        out_shape=jax.ShapeDtypeStruct((b, seq, WIDTH), F32),
        scratch_shapes=[pltpu.VMEM((3, seq, D_HEAD), F32)] * 3,
        compiler_params=_cparams(("arbitrary", "arbitrary")),
        name="attn_prompt",
    )(*args)


def _attn_sample_kernel(n_new, *refs):
    (q_ref, kv0_ref, kv1_ref, kv2_ref, z_ref, c0_ref, c1_ref, c2_ref, o_ref,
     e0_ref, e1_ref, e2_ref, s_ref, p_ref) = refs
    n_t = KEYS_PER_QUERY
    tb = 16
    lane = lax.broadcasted_iota(jnp.int32, (WIDTH, LANES), 1)
    feat = lax.broadcasted_iota(jnp.int32, (WIDTH, LANES), 0)
    head_sum = (feat // D_HEAD == lane).astype(F32)
    lane_t = lax.broadcasted_iota(jnp.int32, (LANES, WIDTH), 0)
    feat_t = lax.broadcasted_iota(jnp.int32, (LANES, WIDTH), 1)
    head_expand = (feat_t // D_HEAD == lane_t).astype(F32)

    w0 = c0_ref.shape[0]
    e0_ref[0:w0, :] = c0_ref[...]
    e0_ref[w0:w0 + n_new, :] = kv0_ref[...]
    w1 = c1_ref.shape[0]
    e1_ref[0:w1, :] = c1_ref[...]
    e1_ref[w1:w1 + n_new, :] = kv1_ref[...]
    w2 = c2_ref.shape[0] * n_new
    e2_ref[0:w2, :] = c2_ref[...].reshape(w2, 2 * WIDTH)
    e2_ref[w2:w2 + n_new, :] = kv2_ref[...]

    m_g, l_g, acc_g = [], [], []
    for g, (e_ref, step) in enumerate(((e0_ref, 1), (e1_ref, 4), (e2_ref, n_new))):
        q = q_ref[:, g * WIDTH:(g + 1) * WIDTH] * ATTN_SCALE

        def tiles(j0, count, lo, e_ref=e_ref, step=step):
            if step == n_new:
                r0 = j0 * n_new if isinstance(j0, int) else pl.multiple_of(j0 * n_new, n_new)
                return e_ref[pl.ds(r0, count * n_new), lo:lo + WIDTH]
            if isinstance(j0, int):
                return e_ref[j0 * step:j0 * step + n_new, lo:lo + WIDTH]
            span = -(-((count - 1) * step + n_new) // SUBLANES) * SUBLANES
            win = e_ref[pl.ds(pl.multiple_of(j0 * step, SUBLANES), span), lo:lo + WIDTH]
            parts = [win[jj * step:jj * step + n_new] for jj in range(count)]
            return jnp.concatenate(parts, axis=0)

        def score_step(jb, carry, q=q, tiles=tiles):
            j0 = jb * tb
            kt = tiles(j0, tb, 0)
            prod = kt * jnp.concatenate([q] * tb, axis=0)
            s_ref[pl.ds(pl.multiple_of(j0 * n_new, tb * n_new), tb * n_new), :] = (
                _dot_x_sel(prod, head_sum))
            return carry

        lax.fori_loop(0, (n_t - 1) // tb, score_step, 0)
        j_last = n_t - 1
        prod = tiles(j_last, 1, 0) * q
        s_ref[j_last * n_new:(j_last + 1) * n_new, :] = _dot_x_sel(prod, head_sum)

        s_all = s_ref[...].reshape(n_t, n_new, LANES)
        m = jnp.max(s_all, axis=0)
        p_all = jnp.exp(s_all - m[None])
        l = jnp.sum(p_all, axis=0)
        p_ref[...] = p_all.reshape(n_t * n_new, LANES)

        def pv_step(jb, acc, tiles=tiles):
            j0 = jb * tb
            vt = tiles(j0, tb, WIDTH)
            p = p_ref[pl.ds(pl.multiple_of(j0 * n_new, tb * n_new), tb * n_new), :]
            pe = _dot_x_sel(p, head_expand) * vt
            return acc + jnp.sum(pe.reshape(tb, n_new, WIDTH), axis=0)

        acc = lax.fori_loop(0, (n_t - 1) // tb, pv_step, jnp.zeros((n_new, WIDTH), F32))
        p = p_ref[j_last * n_new:(j_last + 1) * n_new, :]
        acc = acc + _dot_x_sel(p, head_expand) * tiles(j_last, 1, WIDTH)
        m_g.append(m)
        l_g.append(l)
        acc_g.append(acc)

    m_all = jnp.maximum(jnp.maximum(m_g[0], m_g[1]), m_g[2])
    w_g = [jnp.exp(m - m_all) for m in m_g]
    den = w_g[0] * l_g[0] + w_g[1] * l_g[1] + w_g[2] * l_g[2]
    out = jnp.zeros((n_new, WIDTH), F32)
    for g in range(3):
        out = out + _dot_x_sel(w_g[g] / den, head_expand) * acc_g[g]
    o_ref[...] = out * _silu(z_ref[...])


def _attn_sample(q, kvs, z, caches):
    b, n_new, _ = q.shape
    assert n_new == SUBLANES
    c0 = caches[0].reshape(b, caches[0].shape[1], 2 * WIDTH)
    c1 = caches[1].reshape(b, caches[1].shape[1], 2 * WIDTH)
    w2, dil2 = DIL_GROUPS[2]
    assert caches[0].shape[1] == DIL_GROUPS[0][0] and caches[1].shape[1] == DIL_GROUPS[1][0]
    assert caches[2].shape[1] == w2 and dil2 == 2 * n_new
    c2 = caches[2].reshape(b, w2 // dil2, dil2, 2 * WIDTH)
    n_t = KEYS_PER_QUERY
    row = lambda width: pl.BlockSpec((None, n_new, width), lambda i: (i, 0, 0))
    return pl.pallas_call(
        functools.partial(_attn_sample_kernel, n_new),
        grid=(b,),
        in_specs=[
            row(3 * WIDTH), row(2 * WIDTH), row(2 * WIDTH), row(2 * WIDTH), row(WIDTH),
            pl.BlockSpec((None, c0.shape[1], 2 * WIDTH), lambda i: (i, 0, 0)),
            pl.BlockSpec((None, c1.shape[1], 2 * WIDTH), lambda i: (i, 0, 0)),
            pl.BlockSpec((None, w2 // dil2, n_new, 2 * WIDTH), lambda i: (i, 0, 0, 0)),
        ],
        out_specs=row(WIDTH),
        out_shape=jax.ShapeDtypeStruct((b, n_new, WIDTH), F32),
        scratch_shapes=[
            pltpu.VMEM((c0.shape[1] + n_new, 2 * WIDTH), F32),
            pltpu.VMEM((c1.shape[1] + n_new, 2 * WIDTH), F32),
            pltpu.VMEM((n_t * n_new, 2 * WIDTH), F32),
            pltpu.VMEM((n_t * n_new, LANES), F32),
            pltpu.VMEM((n_t * n_new, LANES), F32),
        ],
        compiler_params=_cparams(("arbitrary",)),
        name="attn_sample",
    )(q, kvs[0], kvs[1], kvs[2], z, c0, c1, c2)


def kernel(x_prompt, x_sample, state_delta, state_conv, cache_kv_w128, cache_kv_w512, cache_kv_w2048,
           c_prompt, c_sample, norm_g, ada_w, ada_b, a_w_in, a_conv_w, a_A_log, a_dt_bias,
           a_out_norm_g, a_w_out, b_w_in, b_w_out, final_norm_g):
    bp, seq, _ = x_prompt.shape
    bs, n_new, _ = x_sample.shape
    n_s = bs * n_new

    mod = _ada_mod(jnp.concatenate([c_prompt, c_sample], axis=0), ada_w, ada_b)
    mod_p = [mod[l, :bp].reshape(bp, 1, 3 * D_MODEL) for l in range(2)]
    mod_s = [jnp.repeat(mod[l, bp:], n_new, axis=0).reshape(1, n_s, 3 * D_MODEL) for l in range(2)]

    w_in = a_w_in[0]
    w_main = w_in[:, :CONV_DIM + WIDTH].astype(BF16)
    w_ab = jnp.zeros((D_MODEL, LANES), BF16).at[:, :2 * N_HEADS].set(
        w_in[:, CONV_DIM + WIDTH:].astype(BF16))
    w_abt = w_in[:, CONV_DIM + WIDTH:].T.astype(BF16)
    w_out_a = a_w_out[0].astype(BF16)
    xs_flat = x_sample.reshape(1, n_s, D_MODEL)

    qkv_p, z_p, ab_p, abt_p = _project(x_prompt, mod_p[0], norm_g[0], w_main, (3, 1), 1024,
                                       w_ab, w_abt)
    o_p, delta_p, conv_p = _gdn_core(qkv_p, z_p, ab_p, abt_p, a_conv_w[0], a_A_log[0],
                                     a_dt_bias[0], a_out_norm_g[0], 512, 128)
    x1_p = _out_project(x_prompt, o_p, mod_p[0], w_out_a, final_norm_g, 512, False)

    qkv_s, z_s, ab_s, abt_s = _project(xs_flat, mod_s[0], norm_g[0], w_main, (3, 1), n_s,
                                       w_ab, w_abt)
    abt_s = abt_s.reshape(16, bs, n_new).transpose(1, 0, 2)
    o_s, delta_s, conv_s = _gdn_core(
        qkv_s.reshape(bs, n_new, CONV_DIM), z_s.reshape(bs, n_new, WIDTH),
        ab_s.reshape(bs, n_new, LANES), abt_s, a_conv_w[0], a_A_log[0], a_dt_bias[0],
        a_out_norm_g[0], n_new, n_new, state_delta[0], state_conv[0])
    x1_s = _out_project(xs_flat, o_s.reshape(1, n_s, WIDTH), mod_s[0], w_out_a, final_norm_g,
                        n_s, False)

    wb = b_w_in[0]
    n_g = len(DIL_GROUPS)
    cols = [wb[:, :n_g * WIDTH]]
    for g in range(n_g):
        cols += [wb[:, (n_g + g) * WIDTH:(n_g + g + 1) * WIDTH],
                 wb[:, (2 * n_g + g) * WIDTH:(2 * n_g + g + 1) * WIDTH]]
    cols.append(wb[:, 3 * n_g * WIDTH:])
    w_b = jnp.concatenate(cols, axis=1).astype(BF16)
    w_out_b = b_w_out[0].astype(BF16)
    segs = (3, 2, 2, 2, 1)

    q_p, kv0_p, kv1_p, kv2_p, zb_p = _project(x1_p, mod_p[1], norm_g[1], w_b, segs, 512)
    ob_p = _attn_prompt(q_p, (kv0_p, kv1_p, kv2_p), zb_p)
    y_p = _out_project(x1_p, ob_p, mod_p[1], w_out_b, final_norm_g, 512, True)

    q_s, kv0_s, kv1_s, kv2_s, zb_s = _project(x1_s, mod_s[1], norm_g[1], w_b, segs, n_s)
    rs = lambda a: a.reshape(bs, n_new, a.shape[-1])
    ob_s = _attn_sample(rs(q_s), (rs(kv0_s), rs(kv1_s), rs(kv2_s)), rs(zb_s),
                        (cache_kv_w128[0], cache_kv_w512[0], cache_kv_w2048[0]))
    y_s = _out_project(x1_s, ob_s.reshape(1, n_s, WIDTH), mod_s[1], w_out_b, final_norm_g,
                       n_s, True)

    def kv_prompt(kv, window):
        keep = min(window, seq)
        return kv[:, seq - keep:].reshape(1, bp, keep, 2, N_HEADS, D_HEAD)

    def kv_sample(kv):
        return kv.reshape(1, bs, n_new, 2, N_HEADS, D_HEAD)

    return (y_p, y_s.reshape(bs, n_new, D_MODEL),
            delta_p[None], delta_s[None], conv_p[None], conv_s[None],
            kv_prompt(kv0_p, DIL_GROUPS[0][0]), kv_sample(kv0_s),
            kv_prompt(kv1_p, DIL_GROUPS[1][0]), kv_sample(kv1_s),
            kv_prompt(kv2_p, DIL_GROUPS[2][0]), kv_sample(kv2_s))
```

```python
import functools

import jax
import jax.numpy as jnp
from jax import lax
from jax.experimental import pallas as pl
from jax.experimental.pallas import tpu as pltpu

F32 = jnp.float32
BF16 = jnp.bfloat16

D_MODEL = 1024
N_HEADS = 8
D_HEAD = 128
WIDTH = N_HEADS * D_HEAD
CONV_W = 4
CONV_DIM = 3 * WIDTH
DIL_GROUPS = ((128, 1), (512, 4), (2048, 16))
KEYS_PER_QUERY = 129
EPS = 1e-6
ATTN_SCALE = D_HEAD ** -0.5
NEG_BIG = -1e30

LANES = 128
SUBLANES = 8
VMEM_LIMIT_BYTES = 56 * 1024 * 1024


def _cparams(semantics):
    return pltpu.CompilerParams(dimension_semantics=semantics,
                                vmem_limit_bytes=VMEM_LIMIT_BYTES)


def _silu(x):
    return x * (1.0 / (1.0 + jnp.exp(-x)))


def _softplus(x):
    return jnp.maximum(x, 0.0) + jnp.log1p(jnp.exp(-jnp.abs(x)))


def _dot(a, b):
    return jnp.dot(a.astype(BF16), b.astype(BF16), preferred_element_type=F32)


def _dot_nt(a, b):
    return lax.dot_general(a.astype(BF16), b.astype(BF16), (((1,), (1,)), ((), ())),
                           preferred_element_type=F32)


def _dot_tn(a, b):
    return lax.dot_general(a.astype(BF16), b.astype(BF16), (((0,), (0,)), ((), ())),
                           preferred_element_type=F32)


def _split3(x):
    x1 = x.astype(BF16)
    r = x - x1.astype(F32)
    x2 = r.astype(BF16)
    x3 = (r - x2.astype(F32)).astype(BF16)
    return x1, x2, x3


def _dot_sel(sel, x):
    s = sel.astype(BF16)
    out = None
    for p in _split3(x):
        t = jnp.dot(s, p, preferred_element_type=F32)
        out = t if out is None else out + t
    return out


def _dot_x_sel(x, sel):
    s = sel.astype(BF16)
    out = None
    for p in _split3(x):
        t = jnp.dot(p, s, preferred_element_type=F32)
        out = t if out is None else out + t
    return out


def _mod_kernel(c_ref, w_ref, b_ref, o_ref):
    s = _silu(c_ref[...])
    o_ref[...] = _dot(s, w_ref[...]) + b_ref[...]


def _ada_mod(c_all, ada_w, ada_b):
    n_layers = ada_w.shape[0]
    rows = c_all.shape[0]
    tn = 768
    return pl.pallas_call(
        _mod_kernel,
        grid=(n_layers, 3 * D_MODEL // tn),
        in_specs=[
            pl.BlockSpec((rows, D_MODEL), lambda l, j: (0, 0)),
            pl.BlockSpec((None, D_MODEL, tn), lambda l, j: (l, 0, j)),
            pl.BlockSpec((None, 1, tn), lambda l, j: (l, 0, j)),
        ],
        out_specs=pl.BlockSpec((None, rows, tn), lambda l, j: (l, 0, j)),
        out_shape=jax.ShapeDtypeStruct((n_layers, rows, 3 * D_MODEL), F32),
        compiler_params=_cparams(("arbitrary", "arbitrary")),
        name="ada_mod",
    )(c_all, ada_w, ada_b.reshape(n_layers, 1, 3 * D_MODEL))


def _proj_kernel(seg_bounds, has_ab, *refs):
    x_ref, mod_ref, g_ref, w_ref = refs[:4]
    pos = 4
    if has_ab:
        wab_ref, wabt_ref = refs[4:6]
        pos = 6
    n_seg = len(seg_bounds)
    seg_refs = refs[pos:pos + n_seg]
    pos += n_seg
    if has_ab:
        ab_ref, abt_ref = refs[pos:pos + 2]
        pos += 2
    h_ref = refs[pos]
    j = pl.program_id(2)

    @pl.when(j == 0)
    def _():
        x = x_ref[...]
        ms = jnp.mean(x * x, axis=-1, keepdims=True)
        y = x * lax.rsqrt(ms + EPS) * g_ref[...]
        shift = mod_ref[:, 0:D_MODEL]
        scale = mod_ref[:, D_MODEL:2 * D_MODEL]
        hb = (y * (1.0 + scale) + shift).astype(BF16)
        h_ref[...] = hb
        if has_ab:
            ab_ref[...] = jnp.dot(hb, wab_ref[...], preferred_element_type=F32)
            abt_ref[...] = lax.dot_general(wabt_ref[...], hb, (((1,), (1,)), ((), ())),
                                           preferred_element_type=F32)

    res = jnp.dot(h_ref[...], w_ref[...], preferred_element_type=F32)
    for (lo, hi), o_ref in zip(seg_bounds, seg_refs):
        @pl.when((j >= lo) & (j < hi))
        def _(o_ref=o_ref):
            o_ref[...] = res


def _project(x, mod, norm_g, w, seg_slabs, tm, w_ab=None, w_abt=None):
    n, t, _ = x.shape
    r = mod.shape[1]
    slab = 1024
    n_slab = w.shape[1] // slab
    assert sum(seg_slabs) == n_slab and t % tm == 0
    has_ab = w_ab is not None
    bounds, lo = [], 0
    for s in seg_slabs:
        bounds.append((lo, lo + s))
        lo += s
    mod_rows = 1 if r == 1 else tm
    mod_map = (lambda b, i, j: (b, 0, 0)) if r == 1 else (lambda b, i, j: (b, i, 0))
    in_specs = [
        pl.BlockSpec((None, tm, D_MODEL), lambda b, i, j: (b, i, 0)),
        pl.BlockSpec((None, mod_rows, 3 * D_MODEL), mod_map),
        pl.BlockSpec((1, D_MODEL), lambda b, i, j: (0, 0)),
        pl.BlockSpec((D_MODEL, slab), lambda b, i, j: (0, j)),
    ]
    args = [x, mod, norm_g.reshape(1, D_MODEL), w]
    if has_ab:
        in_specs += [pl.BlockSpec((D_MODEL, LANES), lambda b, i, j: (0, 0)),
                     pl.BlockSpec((16, D_MODEL), lambda b, i, j: (0, 0))]
        args += [w_ab, w_abt]
    out_specs, out_shapes = [], []
    for (lo, hi) in bounds:
        def seg_map(b, i, j, lo=lo, hi=hi):
            return (b, i, jnp.clip(j - lo, 0, hi - lo - 1))
        out_specs.append(pl.BlockSpec((None, tm, slab), seg_map))
        out_shapes.append(jax.ShapeDtypeStruct((n, t, slab * (hi - lo)), F32))
    if has_ab:
        out_specs += [pl.BlockSpec((None, tm, LANES), lambda b, i, j: (b, i, 0)),
                      pl.BlockSpec((None, 16, tm), lambda b, i, j: (b, 0, i))]
        out_shapes += [jax.ShapeDtypeStruct((n, t, LANES), F32),
                       jax.ShapeDtypeStruct((n, 16, t), F32)]
    return pl.pallas_call(
        functools.partial(_proj_kernel, tuple(bounds), has_ab),
        grid=(n, t // tm, n_slab),
        in_specs=in_specs,
        out_specs=out_specs,
        out_shape=out_shapes,
        scratch_shapes=[pltpu.VMEM((tm, D_MODEL), BF16)],
        compiler_params=_cparams(("arbitrary", "arbitrary", "arbitrary")),
        name="norm_proj",
    )(*args)


def _out_kernel(final_norm, x_ref, o_ref, mod_ref, w_ref, fg_ref, out_ref):
    y = _dot(o_ref[...], w_ref[...])
    gate = mod_ref[:, 2 * D_MODEL:3 * D_MODEL]
    x = x_ref[...] + gate * y
    if final_norm:
        ms = jnp.mean(x * x, axis=-1, keepdims=True)
        x = x * lax.rsqrt(ms + EPS) * fg_ref[...]
    out_ref[...] = x


def _out_project(x, o, mod, w_out, final_g, tm, final_norm):
    n, t, _ = x.shape
    r = mod.shape[1]
    mod_rows = 1 if r == 1 else tm
    mod_map = (lambda b, i: (b, 0, 0)) if r == 1 else (lambda b, i: (b, i, 0))
    return pl.pallas_call(
        functools.partial(_out_kernel, final_norm),
        grid=(n, t // tm),
        in_specs=[
            pl.BlockSpec((None, tm, D_MODEL), lambda b, i: (b, i, 0)),
            pl.BlockSpec((None, tm, WIDTH), lambda b, i: (b, i, 0)),
            pl.BlockSpec((None, mod_rows, 3 * D_MODEL), mod_map),
            pl.BlockSpec((WIDTH, D_MODEL), lambda b, i: (0, 0)),
            pl.BlockSpec((1, D_MODEL), lambda b, i: (0, 0)),
        ],
        out_specs=pl.BlockSpec((None, tm, D_MODEL), lambda b, i: (b, i, 0)),
        out_shape=jax.ShapeDtypeStruct((n, t, D_MODEL), F32),
        compiler_params=_cparams(("arbitrary", "arbitrary")),
        name="out_proj",
    )(x, o, mod, w_out, final_g.reshape(1, D_MODEL))


def _unit_lower_inverses(n_mats, c):
    row = lax.broadcasted_iota(jnp.int32, (c, c), 0)
    col = lax.broadcasted_iota(jnp.int32, (c, c), 1)
    base = min(16, c)
    blk_id = lambda v, size: jnp.right_shift(v, size.bit_length() - 1)
    eye = (row == col).astype(F32)
    same = blk_id(row, base) == blk_id(col, base)
    ps = [jnp.where(same, -n, 0.0) for n in n_mats]
    invs = [eye + p for p in ps]
    ps = [p.astype(BF16) for p in ps]
    size = 2
    while size < base:
        ps = [_dot(p, p).astype(BF16) for p in ps]
        invs = [inv + _dot(p, inv) for p, inv in zip(ps, invs)]
        size *= 2
    size = base
    while size < c:
        same_next = blk_id(row, 2 * size) == blk_id(col, 2 * size)
        off = same_next & jnp.logical_not(same)
        ls = [jnp.where(off, n, 0.0).astype(BF16) for n in n_mats]
        invs_b = [inv.astype(BF16) for inv in invs]
        ts = [_dot(ib, l) for ib, l in zip(invs_b, ls)]
        invs = [inv - _dot(t, ib) for inv, t, ib in zip(invs, ts, invs_b)]
        same = same_next
        size *= 2
    return invs


def _gdn_kernel(tm, c, has_state, *refs):
    (qkv_ref, z_ref, ab_ref, abt_ref, cw_ref, alog_ref, dtb_ref, alogc_ref, dtbc_ref,
     og_ref) = refs[:10]
    pos = 10
    if has_state:
        s0_ref, conv0_ref = refs[10:12]
        pos = 12
    o_ref, s_ref, conv_ref, ext_ref, g_ref, gt_ref = refs[pos:pos + 6]
    t = pl.program_id(1)
    pad = SUBLANES
    halo = CONV_W - 1

    @pl.when(t == 0)
    def _():
        if has_state:
            s_ref[...] = s0_ref[...]
            ext_ref[pad - halo:pad, :] = conv0_ref[...]
        else:
            s_ref[...] = jnp.zeros_like(s_ref)
            ext_ref[pad - halo:pad, :] = jnp.zeros((halo, CONV_DIM), F32)

    ext_ref[pad:pad + tm, :] = qkv_ref[...]

    ab = ab_ref[...]
    g_ref[...] = -jnp.exp(alog_ref[...]) * _softplus(ab + dtb_ref[...])
    abt = abt_ref[...]
    gt_ref[...] = -jnp.exp(alogc_ref[...]) * _softplus(abt + dtbc_ref[...])

    row = lax.broadcasted_iota(jnp.int32, (c, c), 0)
    col = lax.broadcasted_iota(jnp.int32, (c, c), 1)
    tril = row >= col
    strict = row > col
    tril_f = tril.astype(F32)
    triu_f = (row <= col).astype(F32)

    def chunk(ci, carry):
        r0 = ci * c if isinstance(ci, int) else pl.multiple_of(ci * c, c)
        gc_all = _dot_sel(tril_f, g_ref[pl.ds(r0, c), :])
        gct_all = _dot_x_sel(gt_ref[0:N_HEADS, pl.ds(r0, c)], triu_f)
        beta_all = 1.0 / (1.0 + jnp.exp(-ab_ref[pl.ds(r0, c), :]))
        heads = range(N_HEADS)

        def conv(off, h):
            cols = slice(off + h * D_HEAD, off + (h + 1) * D_HEAD)
            win = ext_ref[pl.ds(r0, c + pad), cols]
            acc = None
            for i in range(CONV_W):
                lo = pad - halo + i
                term = win[lo:lo + c] * cw_ref[i:i + 1, cols]
                acc = term if acc is None else acc + term
            return _silu(acc)

        qs, ks, vs = [], [], []
        for h in heads:
            q = conv(0, h)
            k = conv(WIDTH, h)
            qs.append(q * lax.rsqrt(jnp.sum(q * q, axis=-1, keepdims=True) + EPS)
                      * (D_HEAD ** -0.5))
            ks.append(k * lax.rsqrt(jnp.sum(k * k, axis=-1, keepdims=True) + EPS))
            vs.append(conv(2 * WIDTH, h))
        gcs = [jnp.broadcast_to(gc_all[:, h:h + 1], (c, D_HEAD)) for h in heads]
        betas = [jnp.broadcast_to(beta_all[:, N_HEADS + h:N_HEADS + h + 1], (c, D_HEAD))
                 for h in heads]
        g_lasts = [gc[c - 1:c, :] for gc in gcs]
        decays = [jnp.where(tril, jnp.exp(jnp.where(tril, gcs[h][:, 0:c] - gct_all[h:h + 1, :],
                                                    0.0)), 0.0) for h in heads]
        kbs = [k * beta for k, beta in zip(ks, betas)]
        ks_b = [k.astype(BF16) for k in ks]
        kks = [_dot_nt(kb, k) for kb, k in zip(kbs, ks_b)]
        qks = [_dot_nt(q, k) for q, k in zip(qs, ks_b)]
        n_mats = [jnp.where(strict, kk * decay, 0.0) for kk, decay in zip(kks, decays)]
        qks = [(qk * decay).astype(BF16) for qk, decay in zip(qks, decays)]
        invs = _unit_lower_inverses(n_mats, c)
        e_gcs = [jnp.exp(gc) for gc in gcs]
        rhss = [jnp.concatenate([v * beta, kb * e_gc], axis=1).astype(BF16)
                for v, beta, kb, e_gc in zip(vs, betas, kbs, e_gcs)]
        sols = [_dot(inv, rhs) for inv, rhs in zip(invs, rhss)]
        q_decs = [(q * e_gc).astype(BF16) for q, e_gc in zip(qs, e_gcs)]
        k_decs = [(k * jnp.exp(g_last - gc)).astype(BF16)
                  for k, g_last, gc in zip(ks, g_lasts, gcs)]
        s_olds = [s_ref[h] for h in heads]
        s_bs = [s.astype(BF16) for s in s_olds]
        us = [sol[:, 0:D_HEAD] - _dot(sol[:, D_HEAD:2 * D_HEAD], s_b)
              for sol, s_b in zip(sols, s_bs)]
        us_b = [u.astype(BF16) for u in us]
        os_ = [_dot(q_dec, s_b) + _dot(qk, u_b)
               for q_dec, s_b, qk, u_b in zip(q_decs, s_bs, qks, us_b)]
        for h in heads:
            s_ref[h] = s_olds[h] * jnp.exp(g_lasts[h]) + _dot_tn(k_decs[h], us_b[h])
        for h in heads:
            o = os_[h]
            o = o * lax.rsqrt(jnp.mean(o * o, axis=-1, keepdims=True) + EPS) * og_ref[...]
            hc = slice(h * D_HEAD, (h + 1) * D_HEAD)
            o_ref[pl.ds(r0, c), hc] = o * _silu(z_ref[pl.ds(r0, c), hc])
        return carry

    if tm == c:
        chunk(0, 0)
    else:
        lax.fori_loop(0, tm // c, chunk, 0)

    tail = ext_ref[pad + tm - halo:pad + tm, :]
    ext_ref[pad - halo:pad, :] = tail
    conv_ref[...] = tail


def _gdn_core(qkv, z, ab, abt, conv_w, a_log, dt_bias, out_g, tm, c, s0=None, conv0=None):
    b, t, _ = qkv.shape
    has_state = s0 is not None
    alog_row = jnp.zeros((1, LANES), F32).at[0, :N_HEADS].set(a_log)
    dtb_row = jnp.zeros((1, LANES), F32).at[0, :N_HEADS].set(dt_bias)
    alog_col = jnp.zeros((16, 1), F32).at[:N_HEADS, 0].set(a_log)
    dtb_col = jnp.zeros((16, 1), F32).at[:N_HEADS, 0].set(dt_bias)
    const = lambda shape: pl.BlockSpec(shape, lambda i, j: (0,) * len(shape))
    in_specs = [
        pl.BlockSpec((None, tm, CONV_DIM), lambda i, j: (i, j, 0)),
        pl.BlockSpec((None, tm, WIDTH), lambda i, j: (i, j, 0)),
        pl.BlockSpec((None, tm, LANES), lambda i, j: (i, j, 0)),
        pl.BlockSpec((None, 16, tm), lambda i, j: (i, 0, j)),
        const((CONV_W, CONV_DIM)), const((1, LANES)), const((1, LANES)),
        const((16, 1)), const((16, 1)), const((1, D_HEAD)),
    ]
    args = [qkv, z, ab, abt, conv_w, alog_row, dtb_row, alog_col, dtb_col,
            out_g.reshape(1, D_HEAD)]
    if has_state:
        in_specs += [pl.BlockSpec((None, N_HEADS, D_HEAD, D_HEAD), lambda i, j: (i, 0, 0, 0)),
                     pl.BlockSpec((None, CONV_W - 1, CONV_DIM), lambda i, j: (i, 0, 0))]
        args += [s0, conv0]
    return pl.pallas_call(
        functools.partial(_gdn_kernel, tm, c, has_state),
        grid=(b, t // tm),
        in_specs=in_specs,
        out_specs=[
            pl.BlockSpec((None, tm, WIDTH), lambda i, j: (i, j, 0)),
            pl.BlockSpec((None, N_HEADS, D_HEAD, D_HEAD), lambda i, j: (i, 0, 0, 0)),
            pl.BlockSpec((None, CONV_W - 1, CONV_DIM), lambda i, j: (i, 0, 0)),
        ],
        out_shape=[
            jax.ShapeDtypeStruct((b, t, WIDTH), F32),
            jax.ShapeDtypeStruct((b, N_HEADS, D_HEAD, D_HEAD), F32),
            jax.ShapeDtypeStruct((b, CONV_W - 1, CONV_DIM), F32),
        ],
        scratch_shapes=[
            pltpu.VMEM((tm + SUBLANES, CONV_DIM), F32),
            pltpu.VMEM((tm, LANES), F32),
            pltpu.VMEM((16, tm), F32),
        ],
        compiler_params=_cparams(("arbitrary", "arbitrary")),
        name="gdn_core",
    )(*args)


def _attn_prompt_kernel(seq, *refs):
    q_refs = refs[0:3]
    k_refs = refs[3:9:2]
    v_refs = refs[4:9:2]
    z_ref, o_ref, acc_ref, m_ref, l_ref = refs[9:14]
    blk = D_HEAD
    per_step = 8

    for g, (window, dil) in enumerate(DIL_GROUPS):
        q_ref, k_ref, v_ref = q_refs[g], k_refs[g], v_refs[g]
        n_blk = seq // dil // blk
        with_prev = n_blk > 1
        n_keys = 2 * blk if with_prev else blk
        row = lax.broadcasted_iota(jnp.int32, (blk, n_keys), 0)
        col = lax.broadcasted_iota(jnp.int32, (blk, n_keys), 1)
        is_prev = col < (n_keys - blk)
        slack = jnp.where(is_prev, col - row, row + (n_keys - blk) - col)
        cur_ok = slack >= 0
        ones = jnp.ones((n_keys, blk), BF16)

        def rows(start, dil=dil):
            if dil == 1:
                return pl.ds(start, blk)
            return pl.ds(start, blk, stride=dil)

        def body(i, carry, g=g, dil=dil, n_blk=n_blk, q_ref=q_ref, k_ref=k_ref, v_ref=v_ref,
                 rows=rows, with_prev=with_prev, is_prev=is_prev, slack=slack, cur_ok=cur_ok,
                 ones=ones):
            starts, masks, qs, kcs, vcs = [], [], [], [], []
            for u in range(per_step):
                idx = i * per_step + u
                r = idx // n_blk
                n = idx % n_blk
                start = r + dil * blk * n
                starts.append(start)
                qs.append((q_ref[rows(start), :] * ATTN_SCALE).astype(BF16))
                k_cur = k_ref[rows(start), :].astype(BF16)
                v_cur = v_ref[rows(start), :].astype(BF16)
                if with_prev:
                    start_p = r + dil * blk * jnp.maximum(n - 1, 0)
                    first = jnp.where(n > 0, 0, 2 * blk)
                    masks.append(slack - jnp.where(is_prev, first, 0) >= 0)
                    k_cur = jnp.concatenate([k_ref[rows(start_p), :].astype(BF16), k_cur], axis=0)
                    v_cur = jnp.concatenate([v_ref[rows(start_p), :].astype(BF16), v_cur], axis=0)
                else:
                    masks.append(cur_ok)
                kcs.append(k_cur)
                vcs.append(jnp.concatenate([v_cur, ones], axis=1))
            ss = [jnp.where(mask, _dot_nt(q, kc), NEG_BIG) for q, kc, mask in zip(qs, kcs, masks)]
            ms = [jnp.max(s, axis=-1, keepdims=True) for s in ss]
            ps = [jnp.exp(s - m).astype(BF16) for s, m in zip(ss, ms)]
            accs = [jnp.dot(p, vc, preferred_element_type=F32) for p, vc in zip(ps, vcs)]
            for start, acc, m in zip(starts, accs, ms):
                acc_ref[g, rows(start), :] = acc[:, 0:blk]
                l_ref[g, rows(start), :] = acc[:, blk:2 * blk]
                m_ref[g, rows(start), :] = jnp.broadcast_to(m, (blk, blk))
            return carry

        lax.fori_loop(0, dil * n_blk // per_step, body, 0)

    m_all = jnp.maximum(jnp.maximum(m_ref[0], m_ref[1]), m_ref[2])
    num = jnp.zeros((seq, blk), F32)
    den = jnp.zeros((seq, blk), F32)
    for g in range(3):
        w = jnp.exp(m_ref[g] - m_all)
        num = num + w * acc_ref[g]
        den = den + w * l_ref[g]
    o_ref[...] = num / den * _silu(z_ref[...])


def _attn_prompt(q, kvs, z):
    b, seq, _ = q.shape
    blk_spec = lambda f: pl.BlockSpec((None, seq, D_HEAD), f)
    in_specs = [blk_spec(lambda i, h, g=g: (i, 0, g * N_HEADS + h)) for g in range(3)]
    args = [q, q, q]
    for g in range(3):
        in_specs += [blk_spec(lambda i, h: (i, 0, h)), blk_spec(lambda i, h: (i, 0, N_HEADS + h))]
        args += [kvs[g], kvs[g]]
    in_specs.append(blk_spec(lambda i, h: (i, 0, h)))
    args.append(z)
    return pl.pallas_call(
        functools.partial(_attn_prompt_kernel, seq),
        grid=(b, N_HEADS),
        in_specs=in_specs,
        out_specs=blk_spec(lambda i, h: (i, 0, h)),
        out_shape=jax.ShapeDtypeStruct((b, seq, WIDTH), F32),
        scratch_shapes=[pltpu.VMEM((3, seq, D_HEAD), F32)] * 3,
        compiler_params=_cparams(("arbitrary", "arbitrary")),
        name="attn_prompt",
    )(*args)


def _attn_sample_kernel(n_new, *refs):
    q_ref, kv0_ref, kv1_ref, kv2_ref, z_ref, c0_ref, c1_ref, c2_ref, o_ref = refs
    n_t = KEYS_PER_QUERY
    ones = jnp.ones((D_HEAD, D_HEAD), BF16)
    dil1 = DIL_GROUPS[1][1]

    def key_tiles(g, l, kv):
        if g == 0:
            return jnp.concatenate([c0_ref[l:, kv], kv0_ref[0:l + 1, kv]], axis=0)
        if g == 1:
            a, r = divmod(l, dil1)
            parts = [c1_ref[a:, r, kv]]
            if a:
                parts.append(kv1_ref[r:r + 1, kv])
            parts.append(kv1_ref[l:l + 1, kv])
            return jnp.concatenate(parts, axis=0)
        return jnp.concatenate([c2_ref[:, l, kv], kv2_ref[l:l + 1, kv]], axis=0)

    for l in range(n_new):
        m_g, l_g, acc_g = [], [], []
        for g in range(len(DIL_GROUPS)):
            q = q_ref[l, g] * ATTN_SCALE
            prod = (key_tiles(g, l, 0) * q[None]).reshape(n_t * N_HEADS, D_HEAD)
            s = jnp.dot(prod.astype(BF16), ones, preferred_element_type=F32)
            s = s.reshape(n_t, N_HEADS, D_HEAD)
            m = jnp.max(s, axis=0)
            p = jnp.exp(s - m[None])
            m_g.append(m)
            l_g.append(jnp.sum(p, axis=0))
            acc_g.append(jnp.sum(p * key_tiles(g, l, 1), axis=0))
        m_all = jnp.maximum(jnp.maximum(m_g[0], m_g[1]), m_g[2])
        w_g = [jnp.exp(m - m_all) for m in m_g]
        num = w_g[0] * acc_g[0] + w_g[1] * acc_g[1] + w_g[2] * acc_g[2]
        den = w_g[0] * l_g[0] + w_g[1] * l_g[1] + w_g[2] * l_g[2]
        o_ref[l] = num / den * _silu(z_ref[l])


def _attn_sample(q, kvs, z, caches):
    b, n_new = q.shape[:2]
    assert n_new == SUBLANES
    (w0, d0), (w1, d1), (w2, d2) = DIL_GROUPS
    assert caches[0].shape[1] == w0 and caches[1].shape[1] == w1 and caches[2].shape[1] == w2
    assert d0 == 1 and n_new % d1 == 0 and d2 == 2 * n_new
    assert w0 // d0 + 1 == KEYS_PER_QUERY and w1 // d1 + 1 == KEYS_PER_QUERY
    assert w2 // d2 + 1 == KEYS_PER_QUERY
    c1 = caches[1].reshape(b, w1 // d1, d1, 2, N_HEADS, D_HEAD)
    c2 = caches[2].reshape(b, w2 // d2, d2, 2, N_HEADS, D_HEAD)
    tail = (2, N_HEADS, D_HEAD)
    new_spec = pl.BlockSpec((None, n_new) + tail, lambda i: (i, 0, 0, 0, 0))
    return pl.pallas_call(
        functools.partial(_attn_sample_kernel, n_new),
        grid=(b,),
        in_specs=[
            pl.BlockSpec((None, n_new, 3, N_HEADS, D_HEAD), lambda i: (i, 0, 0, 0, 0)),
            new_spec, new_spec, new_spec,
            pl.BlockSpec((None, n_new, N_HEADS, D_HEAD), lambda i: (i, 0, 0, 0)),
            pl.BlockSpec((None, w0) + tail, lambda i: (i, 0, 0, 0, 0)),
            pl.BlockSpec((None, w1 // d1, d1) + tail, lambda i: (i, 0, 0, 0, 0, 0)),
            pl.BlockSpec((None, w2 // d2, n_new) + tail, lambda i: (i, 0, 0, 0, 0, 0)),
        ],
        out_specs=pl.BlockSpec((None, n_new, N_HEADS, D_HEAD), lambda i: (i, 0, 0, 0)),
        out_shape=jax.ShapeDtypeStruct((b, n_new, N_HEADS, D_HEAD), F32),
        compiler_params=_cparams(("arbitrary",)),
        name="attn_sample",
    )(q, kvs[0], kvs[1], kvs[2], z, caches[0], c1, c2)


def kernel(x_prompt, x_sample, state_delta, state_conv, cache_kv_w128, cache_kv_w512, cache_kv_w2048,
           c_prompt, c_sample, norm_g, ada_w, ada_b, a_w_in, a_conv_w, a_A_log, a_dt_bias,
           a_out_norm_g, a_w_out, b_w_in, b_w_out, final_norm_g):
    bp, seq, _ = x_prompt.shape
    bs, n_new, _ = x_sample.shape
    n_s = bs * n_new

    mod = _ada_mod(jnp.concatenate([c_prompt, c_sample], axis=0), ada_w, ada_b)
    mod_p = [mod[l, :bp].reshape(bp, 1, 3 * D_MODEL) for l in range(2)]
    mod_s = [jnp.repeat(mod[l, bp:], n_new, axis=0).reshape(1, n_s, 3 * D_MODEL) for l in range(2)]

    w_in = a_w_in[0]
    w_main = w_in[:, :CONV_DIM + WIDTH].astype(BF16)
    w_ab = jnp.zeros((D_MODEL, LANES), BF16).at[:, :2 * N_HEADS].set(
        w_in[:, CONV_DIM + WIDTH:].astype(BF16))
    w_abt = w_in[:, CONV_DIM + WIDTH:].T.astype(BF16)
    w_out_a = a_w_out[0].astype(BF16)
    xs_flat = x_sample.reshape(1, n_s, D_MODEL)

    qkv_p, z_p, ab_p, abt_p = _project(x_prompt, mod_p[0], norm_g[0], w_main, (3, 1), 1024,
                                       w_ab, w_abt)
    o_p, delta_p, conv_p = _gdn_core(qkv_p, z_p, ab_p, abt_p, a_conv_w[0], a_A_log[0],
                                     a_dt_bias[0], a_out_norm_g[0], 512, 128)
    x1_p = _out_project(x_prompt, o_p, mod_p[0], w_out_a, final_norm_g, 512, False)

    qkv_s, z_s, ab_s, abt_s = _project(xs_flat, mod_s[0], norm_g[0], w_main, (3, 1), n_s,
                                       w_ab, w_abt)
    abt_s = abt_s.reshape(16, bs, n_new).transpose(1, 0, 2)
    o_s, delta_s, conv_s = _gdn_core(
        qkv_s.reshape(bs, n_new, CONV_DIM), z_s.reshape(bs, n_new, WIDTH),
        ab_s.reshape(bs, n_new, LANES), abt_s, a_conv_w[0], a_A_log[0], a_dt_bias[0],
        a_out_norm_g[0], n_new, n_new, state_delta[0], state_conv[0])
    x1_s = _out_project(xs_flat, o_s.reshape(1, n_s, WIDTH), mod_s[0], w_out_a, final_norm_g,
                        n_s, False)

    wb = b_w_in[0]
    n_g = len(DIL_GROUPS)
    cols = [wb[:, :n_g * WIDTH]]
    for g in range(n_g):
        cols += [wb[:, (n_g + g) * WIDTH:(n_g + g + 1) * WIDTH],
                 wb[:, (2 * n_g + g) * WIDTH:(2 * n_g + g + 1) * WIDTH]]
    cols.append(wb[:, 3 * n_g * WIDTH:])
    w_b = jnp.concatenate(cols, axis=1).astype(BF16)
    w_out_b = b_w_out[0].astype(BF16)
    segs = (3, 2, 2, 2, 1)

    q_p, kv0_p, kv1_p, kv2_p, zb_p = _project(x1_p, mod_p[1], norm_g[1], w_b, segs, 512)
    ob_p = _attn_prompt(q_p, (kv0_p, kv1_p, kv2_p), zb_p)
    y_p = _out_project(x1_p, ob_p, mod_p[1], w_out_b, final_norm_g, 512, True)

    q_s, kv0_s, kv1_s, kv2_s, zb_s = _project(x1_s, mod_s[1], norm_g[1], w_b, segs, n_s)
    kvn = [kv.reshape(bs, n_new, 2, N_HEADS, D_HEAD) for kv in (kv0_s, kv1_s, kv2_s)]
    ob_s = _attn_sample(q_s.reshape(bs, n_new, n_g, N_HEADS, D_HEAD), kvn,
                        zb_s.reshape(bs, n_new, N_HEADS, D_HEAD),
                        (cache_kv_w128[0], cache_kv_w512[0], cache_kv_w2048[0]))
    y_s = _out_project(x1_s, ob_s.reshape(1, n_s, WIDTH), mod_s[1], w_out_b, final_norm_g,
                       n_s, True)

    def kv_prompt(kv, window):
        keep = min(window, seq)
        return kv[:, seq - keep:].reshape(1, bp, keep, 2, N_HEADS, D_HEAD)

    return (y_p, y_s.reshape(bs, n_new, D_MODEL),
            delta_p[None], delta_s[None], conv_p[None], conv_s[None],
            kv_prompt(kv0_p, DIL_GROUPS[0][0]), kvn[0][None],
            kv_prompt(kv1_p, DIL_GROUPS[1][0]), kvn[1][None],
            kv_prompt(kv2_p, DIL_GROUPS[2][0]), kvn[2][None])
```

```python
import functools

import jax
import jax.numpy as jnp
from jax import lax
from jax.experimental import pallas as pl
from jax.experimental.pallas import tpu as pltpu

F32 = jnp.float32
BF16 = jnp.bfloat16

D_MODEL = 1024
N_HEADS = 8
D_HEAD = 128
WIDTH = N_HEADS * D_HEAD
CONV_W = 4
CONV_DIM = 3 * WIDTH
DIL_GROUPS = ((128, 1), (512, 4), (2048, 16))
KEYS_PER_QUERY = 129
EPS = 1e-6
ATTN_SCALE = D_HEAD ** -0.5
NEG_BIG = -1e30

LANES = 128
SUBLANES = 8
VMEM_LIMIT_BYTES = 56 * 1024 * 1024


def _cparams(semantics):
    return pltpu.CompilerParams(dimension_semantics=semantics,
                                vmem_limit_bytes=VMEM_LIMIT_BYTES)


def _silu(x):
    return x * (1.0 / (1.0 + jnp.exp(-x)))


def _softplus(x):
    return jnp.maximum(x, 0.0) + jnp.log1p(jnp.exp(-jnp.abs(x)))


def _dot(a, b):
    return jnp.dot(a.astype(BF16), b.astype(BF16), preferred_element_type=F32)


def _dot_nt(a, b):
    return lax.dot_general(a.astype(BF16), b.astype(BF16), (((1,), (1,)), ((), ())),
                           preferred_element_type=F32)


def _dot_tn(a, b):
    return lax.dot_general(a.astype(BF16), b.astype(BF16), (((0,), (0,)), ((), ())),
                           preferred_element_type=F32)


def _split3(x):
    x1 = x.astype(BF16)
    r = x - x1.astype(F32)
    x2 = r.astype(BF16)
    x3 = (r - x2.astype(F32)).astype(BF16)
    return x1, x2, x3


def _dot_sel(sel, x):
    s = sel.astype(BF16)
    out = None
    for p in _split3(x):
        t = jnp.dot(s, p, preferred_element_type=F32)
        out = t if out is None else out + t
    return out


def _dot_x_sel(x, sel):
    s = sel.astype(BF16)
    out = None
    for p in _split3(x):
        t = jnp.dot(p, s, preferred_element_type=F32)
        out = t if out is None else out + t
    return out


def _mod_kernel(c_ref, w_ref, b_ref, o_ref):
    s = _silu(c_ref[...])
    o_ref[...] = _dot(s, w_ref[...]) + b_ref[...]


def _ada_mod(c_all, ada_w, ada_b):
    n_layers = ada_w.shape[0]
    rows = c_all.shape[0]
    tn = 768
    return pl.pallas_call(
        _mod_kernel,
        grid=(n_layers, 3 * D_MODEL // tn),
        in_specs=[
            pl.BlockSpec((rows, D_MODEL), lambda l, j: (0, 0)),
            pl.BlockSpec((None, D_MODEL, tn), lambda l, j: (l, 0, j)),
            pl.BlockSpec((None, 1, tn), lambda l, j: (l, 0, j)),
        ],
        out_specs=pl.BlockSpec((None, rows, tn), lambda l, j: (l, 0, j)),
        out_shape=jax.ShapeDtypeStruct((n_layers, rows, 3 * D_MODEL), F32),
        compiler_params=_cparams(("arbitrary", "arbitrary")),
        name="ada_mod",
    )(c_all, ada_w, ada_b.reshape(n_layers, 1, 3 * D_MODEL))


def _proj_kernel(seg_bounds, has_ab, *refs):
    x_ref, mod_ref, g_ref, w_ref = refs[:4]
    pos = 4
    if has_ab:
        wab_ref, wabt_ref = refs[4:6]
        pos = 6
    n_seg = len(seg_bounds)
    seg_refs = refs[pos:pos + n_seg]
    pos += n_seg
    if has_ab:
        ab_ref, abt_ref = refs[pos:pos + 2]
        pos += 2
    h_ref = refs[pos]
    j = pl.program_id(2)

    @pl.when(j == 0)
    def _():
        hb = _mod_norm(x_ref[...], g_ref[...], mod_ref[...]).astype(BF16)
        h_ref[...] = hb
        if has_ab:
            ab_ref[...] = jnp.dot(hb, wab_ref[...], preferred_element_type=F32)
            abt_ref[...] = lax.dot_general(wabt_ref[...], hb, (((1,), (1,)), ((), ())),
                                           preferred_element_type=F32)

    res = jnp.dot(h_ref[...], w_ref[...], preferred_element_type=F32)
    for (lo, hi), o_ref in zip(seg_bounds, seg_refs):
        @pl.when((j >= lo) & (j < hi))
        def _(o_ref=o_ref):
            o_ref[...] = res


def _project(x, mod, norm_g, w, seg_slabs, tm, w_ab=None, w_abt=None):
    n, t, _ = x.shape
    r = mod.shape[1]
    slab = 1024
    n_slab = w.shape[1] // slab
    assert sum(seg_slabs) == n_slab and t % tm == 0
    has_ab = w_ab is not None
    bounds, lo = [], 0
    for s in seg_slabs:
        bounds.append((lo, lo + s))
        lo += s
    mod_rows = 1 if r == 1 else tm
    mod_map = (lambda b, i, j: (b, 0, 0)) if r == 1 else (lambda b, i, j: (b, i, 0))
    in_specs = [
        pl.BlockSpec((None, tm, D_MODEL), lambda b, i, j: (b, i, 0)),
        pl.BlockSpec((None, mod_rows, 3 * D_MODEL), mod_map),
        pl.BlockSpec((1, D_MODEL), lambda b, i, j: (0, 0)),
        pl.BlockSpec((D_MODEL, slab), lambda b, i, j: (0, j)),
    ]
    args = [x, mod, norm_g.reshape(1, D_MODEL), w]
    if has_ab:
        in_specs += [pl.BlockSpec((D_MODEL, LANES), lambda b, i, j: (0, 0)),
                     pl.BlockSpec((16, D_MODEL), lambda b, i, j: (0, 0))]
        args += [w_ab, w_abt]
    out_specs, out_shapes = [], []
    for (lo, hi) in bounds:
        def seg_map(b, i, j, lo=lo, hi=hi):
            return (b, i, jnp.clip(j - lo, 0, hi - lo - 1))
        out_specs.append(pl.BlockSpec((None, tm, slab), seg_map))
        out_shapes.append(jax.ShapeDtypeStruct((n, t, slab * (hi - lo)), F32))
    if has_ab:
        out_specs += [pl.BlockSpec((None, tm, LANES), lambda b, i, j: (b, i, 0)),
                      pl.BlockSpec((None, 16, tm), lambda b, i, j: (b, 0, i))]
        out_shapes += [jax.ShapeDtypeStruct((n, t, LANES), F32),
                       jax.ShapeDtypeStruct((n, 16, t), F32)]
    return pl.pallas_call(
        functools.partial(_proj_kernel, tuple(bounds), has_ab),
        grid=(n, t // tm, n_slab),
        in_specs=in_specs,
        out_specs=out_specs,
        out_shape=out_shapes,
        scratch_shapes=[pltpu.VMEM((tm, D_MODEL), BF16)],
        compiler_params=_cparams(("arbitrary", "arbitrary", "arbitrary")),
        name="norm_proj",
    )(*args)


def _mod_norm(x, g, mod):
    ms = jnp.mean(x * x, axis=-1, keepdims=True)
    y = x * lax.rsqrt(ms + EPS) * g
    return y * (1.0 + mod[:, D_MODEL:2 * D_MODEL]) + mod[:, 0:D_MODEL]


def _out_kernel(mode, x_ref, o_ref, mod_ref, w_ref, g_ref, mod_next_ref, out_ref, *h_ref):
    y = _dot(o_ref[...], w_ref[...])
    gate = mod_ref[:, 2 * D_MODEL:3 * D_MODEL]
    x = x_ref[...] + gate * y
    if mode == "final_norm":
        ms = jnp.mean(x * x, axis=-1, keepdims=True)
        x = x * lax.rsqrt(ms + EPS) * g_ref[...]
    out_ref[...] = x
    if mode == "next_h":
        h_ref[0][...] = _mod_norm(x, g_ref[...], mod_next_ref[...]).astype(BF16)


def _out_project(x, o, mod, w_out, gain, tm, mode, mod_next=None):
    n, t, _ = x.shape
    r = mod.shape[1]
    mod_rows = 1 if r == 1 else tm
    mod_map = (lambda b, i: (b, 0, 0)) if r == 1 else (lambda b, i: (b, i, 0))
    mod_spec = pl.BlockSpec((None, mod_rows, 3 * D_MODEL), mod_map)
    row_spec = pl.BlockSpec((None, tm, D_MODEL), lambda b, i: (b, i, 0))
    out_specs, out_shape = row_spec, jax.ShapeDtypeStruct((n, t, D_MODEL), F32)
    if mode == "next_h":
        out_specs = [row_spec, row_spec]
        out_shape = [out_shape, jax.ShapeDtypeStruct((n, t, D_MODEL), BF16)]
    return pl.pallas_call(
        functools.partial(_out_kernel, mode),
        grid=(n, t // tm),
        in_specs=[
            row_spec,
            pl.BlockSpec((None, tm, WIDTH), lambda b, i: (b, i, 0)),
            mod_spec,
            pl.BlockSpec((WIDTH, D_MODEL), lambda b, i: (0, 0)),
            pl.BlockSpec((1, D_MODEL), lambda b, i: (0, 0)),
            mod_spec,
        ],
        out_specs=out_specs,
        out_shape=out_shape,
        compiler_params=_cparams(("arbitrary", "arbitrary")),
        name="out_proj",
    )(x, o, mod, w_out, gain.reshape(1, D_MODEL), mod if mod_next is None else mod_next)


def _unit_lower_inverses(n_mats, c):
    row = lax.broadcasted_iota(jnp.int32, (c, c), 0)
    col = lax.broadcasted_iota(jnp.int32, (c, c), 1)
    base = min(16, c)
    blk_id = lambda v, size: jnp.right_shift(v, size.bit_length() - 1)
    eye = (row == col).astype(F32)
    same = blk_id(row, base) == blk_id(col, base)
    ps = [jnp.where(same, -n, 0.0) for n in n_mats]
    invs = [eye + p for p in ps]
    ps = [p.astype(BF16) for p in ps]
    size = 2
    while size < base:
        ps = [_dot(p, p).astype(BF16) for p in ps]
        invs = [inv + _dot(p, inv) for p, inv in zip(ps, invs)]
        size *= 2
    size = base
    while size < c:
        same_next = blk_id(row, 2 * size) == blk_id(col, 2 * size)
        off = same_next & jnp.logical_not(same)
        ls = [jnp.where(off, n, 0.0).astype(BF16) for n in n_mats]
        invs_b = [inv.astype(BF16) for inv in invs]
        ts = [_dot(ib, l) for ib, l in zip(invs_b, ls)]
        invs = [inv - _dot(t, ib) for inv, t, ib in zip(invs, ts, invs_b)]
        same = same_next
        size *= 2
    return invs


def _gdn_kernel(tm, c, has_state, *refs):
    (qkv_ref, z_ref, ab_ref, abt_ref, cw_ref, alog_ref, dtb_ref, alogc_ref, dtbc_ref,
     og_ref) = refs[:10]
    pos = 10
    if has_state:
        s0_ref, conv0_ref = refs[10:12]
        pos = 12
    o_ref, s_ref, conv_ref, ext_ref, g_ref, gt_ref = refs[pos:pos + 6]
    t = pl.program_id(1)
    pad = SUBLANES
    halo = CONV_W - 1

    @pl.when(t == 0)
    def _():
        if has_state:
            s_ref[...] = s0_ref[...]
            ext_ref[pad - halo:pad, :] = conv0_ref[...]
        else:
            s_ref[...] = jnp.zeros_like(s_ref)
            ext_ref[pad - halo:pad, :] = jnp.zeros((halo, CONV_DIM), F32)

    ext_ref[pad:pad + tm, :] = qkv_ref[...]

    ab = ab_ref[...]
    g_ref[...] = -jnp.exp(alog_ref[...]) * _softplus(ab + dtb_ref[...])
    abt = abt_ref[...]
    gt_ref[...] = -jnp.exp(alogc_ref[...]) * _softplus(abt + dtbc_ref[...])

    row = lax.broadcasted_iota(jnp.int32, (c, c), 0)
    col = lax.broadcasted_iota(jnp.int32, (c, c), 1)
    tril = row >= col
    strict = row > col
    tril_f = tril.astype(F32)
    triu_f = (row <= col).astype(F32)

    def chunk(ci, carry):
        r0 = ci * c if isinstance(ci, int) else pl.multiple_of(ci * c, c)
        gc_all = _dot_sel(tril_f, g_ref[pl.ds(r0, c), :])
        gct_all = _dot_x_sel(gt_ref[0:N_HEADS, pl.ds(r0, c)], triu_f)
        beta_all = 1.0 / (1.0 + jnp.exp(-ab_ref[pl.ds(r0, c), :]))
        heads = range(N_HEADS)

        def conv(off, h):
            cols = slice(off + h * D_HEAD, off + (h + 1) * D_HEAD)
            win = ext_ref[pl.ds(r0, c + pad), cols]
            acc = None
            for i in range(CONV_W):
                lo = pad - halo + i
                term = win[lo:lo + c] * cw_ref[i:i + 1, cols]
                acc = term if acc is None else acc + term
            return _silu(acc)

        qs, ks, vs = [], [], []
        for h in heads:
            q = conv(0, h)
            k = conv(WIDTH, h)
            qs.append(q * lax.rsqrt(jnp.sum(q * q, axis=-1, keepdims=True) + EPS)
                      * (D_HEAD ** -0.5))
            ks.append(k * lax.rsqrt(jnp.sum(k * k, axis=-1, keepdims=True) + EPS))
            vs.append(conv(2 * WIDTH, h))
        gcs = [jnp.broadcast_to(gc_all[:, h:h + 1], (c, D_HEAD)) for h in heads]
        betas = [jnp.broadcast_to(beta_all[:, N_HEADS + h:N_HEADS + h + 1], (c, D_HEAD))
                 for h in heads]
        g_lasts = [gc[c - 1:c, :] for gc in gcs]
        decays = [jnp.where(tril, jnp.exp(jnp.where(tril, gcs[h][:, 0:c] - gct_all[h:h + 1, :],
                                                    0.0)), 0.0) for h in heads]
        kbs = [k * beta for k, beta in zip(ks, betas)]
        ks_b = [k.astype(BF16) for k in ks]
        kks = [_dot_nt(kb, k) for kb, k in zip(kbs, ks_b)]
        qks = [_dot_nt(q, k) for q, k in zip(qs, ks_b)]
        n_mats = [jnp.where(strict, kk * decay, 0.0) for kk, decay in zip(kks, decays)]
        qks = [(qk * decay).astype(BF16) for qk, decay in zip(qks, decays)]
        invs = _unit_lower_inverses(n_mats, c)
        e_gcs = [jnp.exp(gc) for gc in gcs]
        rhss = [jnp.concatenate([v * beta, kb * e_gc], axis=1).astype(BF16)
                for v, beta, kb, e_gc in zip(vs, betas, kbs, e_gcs)]
        sols = [_dot(inv, rhs) for inv, rhs in zip(invs, rhss)]
        q_decs = [(q * e_gc).astype(BF16) for q, e_gc in zip(qs, e_gcs)]
        k_decs = [(k * jnp.exp(g_last - gc)).astype(BF16)
                  for k, g_last, gc in zip(ks, g_lasts, gcs)]
        s_olds = [s_ref[h] for h in heads]
        s_bs = [s.astype(BF16) for s in s_olds]
        us = [sol[:, 0:D_HEAD] - _dot(sol[:, D_HEAD:2 * D_HEAD], s_b)
              for sol, s_b in zip(sols, s_bs)]
        us_b = [u.astype(BF16) for u in us]
        os_ = [_dot(q_dec, s_b) + _dot(qk, u_b)
               for q_dec, s_b, qk, u_b in zip(q_decs, s_bs, qks, us_b)]
        for h in heads:
            s_ref[h] = s_olds[h] * jnp.exp(g_lasts[h]) + _dot_tn(k_decs[h], us_b[h])
        for h in heads:
            o = os_[h]
            o = o * lax.rsqrt(jnp.mean(o * o, axis=-1, keepdims=True) + EPS) * og_ref[...]
            hc = slice(h * D_HEAD, (h + 1) * D_HEAD)
            o_ref[pl.ds(r0, c), hc] = o * _silu(z_ref[pl.ds(r0, c), hc])
        return carry

    if tm == c:
        chunk(0, 0)
    else:
        lax.fori_loop(0, tm // c, chunk, 0)

    tail = ext_ref[pad + tm - halo:pad + tm, :]
    ext_ref[pad - halo:pad, :] = tail
    conv_ref[...] = tail


def _gdn_core(qkv, z, ab, abt, conv_w, a_log, dt_bias, out_g, tm, c, s0=None, conv0=None):
    b, t, _ = qkv.shape
    has_state = s0 is not None
    alog_row = jnp.zeros((1, LANES), F32).at[0, :N_HEADS].set(a_log)
    dtb_row = jnp.zeros((1, LANES), F32).at[0, :N_HEADS].set(dt_bias)
    alog_col = jnp.zeros((16, 1), F32).at[:N_HEADS, 0].set(a_log)
    dtb_col = jnp.zeros((16, 1), F32).at[:N_HEADS, 0].set(dt_bias)
    const = lambda shape: pl.BlockSpec(shape, lambda i, j: (0,) * len(shape))
    in_specs = [
        pl.BlockSpec((None, tm, CONV_DIM), lambda i, j: (i, j, 0)),
        pl.BlockSpec((None, tm, WIDTH), lambda i, j: (i, j, 0)),
        pl.BlockSpec((None, tm, LANES), lambda i, j: (i, j, 0)),
        pl.BlockSpec((None, 16, tm), lambda i, j: (i, 0, j)),
        const((CONV_W, CONV_DIM)), const((1, LANES)), const((1, LANES)),
        const((16, 1)), const((16, 1)), const((1, D_HEAD)),
    ]
    args = [qkv, z, ab, abt, conv_w, alog_row, dtb_row, alog_col, dtb_col,
            out_g.reshape(1, D_HEAD)]
    if has_state:
        in_specs += [pl.BlockSpec((None, N_HEADS, D_HEAD, D_HEAD), lambda i, j: (i, 0, 0, 0)),
                     pl.BlockSpec((None, CONV_W - 1, CONV_DIM), lambda i, j: (i, 0, 0))]
        args += [s0, conv0]
    return pl.pallas_call(
        functools.partial(_gdn_kernel, tm, c, has_state),
        grid=(b, t // tm),
        in_specs=in_specs,
        out_specs=[
            pl.BlockSpec((None, tm, WIDTH), lambda i, j: (i, j, 0)),
            pl.BlockSpec((None, N_HEADS, D_HEAD, D_HEAD), lambda i, j: (i, 0, 0, 0)),
            pl.BlockSpec((None, CONV_W - 1, CONV_DIM), lambda i, j: (i, 0, 0)),
        ],
        out_shape=[
            jax.ShapeDtypeStruct((b, t, WIDTH), F32),
            jax.ShapeDtypeStruct((b, N_HEADS, D_HEAD, D_HEAD), F32),
            jax.ShapeDtypeStruct((b, CONV_W - 1, CONV_DIM), F32),
        ],
        scratch_shapes=[
            pltpu.VMEM((tm + SUBLANES, CONV_DIM), F32),
            pltpu.VMEM((tm, LANES), F32),
            pltpu.VMEM((16, tm), F32),
        ],
        compiler_params=_cparams(("arbitrary", "arbitrary")),
        name="gdn_core",
    )(*args)


def _banded_attention(dil, seq, q_ref, k_ref, v_ref, acc_ref, m_ref, l_ref):
    blk = D_HEAD
    per_step = 8
    n_blk = seq // dil // blk
    with_prev = n_blk > 1
    n_keys = 2 * blk if with_prev else blk
    row = lax.broadcasted_iota(jnp.int32, (blk, n_keys), 0)
    col = lax.broadcasted_iota(jnp.int32, (blk, n_keys), 1)
    is_prev = col < (n_keys - blk)
    slack = jnp.where(is_prev, col - row, row + (n_keys - blk) - col)
    cur_ok = slack >= 0
    ones = jnp.ones((n_keys, blk), BF16)

    def rows(start):
        if dil == 1:
            return pl.ds(start, blk)
        return pl.ds(start, blk, stride=dil)

    def body(i, carry):
        starts, masks, qs, kcs, vcs = [], [], [], [], []
        for u in range(per_step):
            idx = i * per_step + u
            r = idx // n_blk
            n = idx % n_blk
            start = r + dil * blk * n
            starts.append(start)
            qs.append(q_ref[rows(start), :].astype(BF16))
            k_cur = k_ref[rows(start), :].astype(BF16)
            v_cur = v_ref[rows(start), :].astype(BF16)
            if with_prev:
                start_p = r + dil * blk * jnp.maximum(n - 1, 0)
                first = jnp.where(n > 0, 0, 2 * blk)
                masks.append(slack - jnp.where(is_prev, first, 0) >= 0)
                k_cur = jnp.concatenate([k_ref[rows(start_p), :].astype(BF16), k_cur], axis=0)
                v_cur = jnp.concatenate([v_ref[rows(start_p), :].astype(BF16), v_cur], axis=0)
            else:
                masks.append(cur_ok)
            kcs.append(k_cur)
            vcs.append(jnp.concatenate([v_cur, ones], axis=1))
        ss = [jnp.where(mask, _dot_nt(q, kc), NEG_BIG) for q, kc, mask in zip(qs, kcs, masks)]
        ms = [jnp.max(s, axis=-1, keepdims=True) for s in ss]
        ps = [jnp.exp(s - m).astype(BF16) for s, m in zip(ss, ms)]
        accs = [jnp.dot(p, vc, preferred_element_type=F32) for p, vc in zip(ps, vcs)]
        for start, acc, m in zip(starts, accs, ms):
            acc_ref[rows(start), :] = acc[:, 0:blk]
            l_ref[rows(start), :] = acc[:, blk:2 * blk]
            m_ref[rows(start), :] = jnp.broadcast_to(m, (blk, blk))
        return carry

    lax.fori_loop(0, dil * n_blk // per_step, body, 0)


def _attn_layer_kernel(seq, h_ref, w_ref, o_ref, kv0_ref, kv1_ref, kv2_ref,
                       q_s, k_s, v_s, z_s, acc_s, m_s, l_s, sems):
    b = pl.program_id(0)
    hp = pl.program_id(1)
    n_g = len(DIL_GROUPS)
    pair = 2 * D_HEAD
    kv_refs = (kv0_ref, kv1_ref, kv2_ref)

    def proj(slab):
        return jnp.dot(h_ref[...], w_ref[:, slab * pair:(slab + 1) * pair],
                       preferred_element_type=F32)

    def kv_copy(g, i, kv):
        keep = kv_refs[g].shape[1]
        src = (k_s, v_s)[kv].at[g, i, pl.ds(seq - keep, keep), :]
        col = pl.multiple_of(kv * WIDTH + (2 * hp + i) * D_HEAD, D_HEAD)
        dst = kv_refs[g].at[b, :, pl.ds(col, D_HEAD)]
        return pltpu.make_async_copy(src, dst, sems.at[(g * 2 + i) * 2 + kv])

    for g in range(n_g):
        k = proj(n_g + 2 * g)
        v = proj(n_g + 2 * g + 1)
        for i in range(2):
            k_s[g, i] = k[:, i * D_HEAD:(i + 1) * D_HEAD]
            v_s[g, i] = v[:, i * D_HEAD:(i + 1) * D_HEAD]
        for i in range(2):
            kv_copy(g, i, 0).start()
            kv_copy(g, i, 1).start()
        q = proj(g) * ATTN_SCALE
        for i in range(2):
            q_s[g, i] = q[:, i * D_HEAD:(i + 1) * D_HEAD]
    z = proj(3 * n_g)
    for i in range(2):
        z_s[i] = z[:, i * D_HEAD:(i + 1) * D_HEAD]

    for i in range(2):
        for g, (_, dil) in enumerate(DIL_GROUPS):
            _banded_attention(dil, seq, q_s.at[g, i], k_s.at[g, i], v_s.at[g, i],
                              acc_s.at[g], m_s.at[g], l_s.at[g])
        m_all = jnp.maximum(jnp.maximum(m_s[0], m_s[1]), m_s[2])
        num = jnp.zeros((seq, D_HEAD), F32)
        den = jnp.zeros((seq, D_HEAD), F32)
        for g in range(n_g):
            w = jnp.exp(m_s[g] - m_all)
            num = num + w * acc_s[g]
            den = den + w * l_s[g]
        o_ref[:, i * D_HEAD:(i + 1) * D_HEAD] = (num / den * _silu(z_s[i])).astype(BF16)

    for g in range(n_g):
        for i in range(2):
            kv_copy(g, i, 0).wait()
            kv_copy(g, i, 1).wait()


def _attn_layer_prompt(h, w_pairs):
    b, seq, _ = h.shape
    n_pair = N_HEADS // 2
    n_g = len(DIL_GROUPS)
    keeps = [min(window, seq) for window, _ in DIL_GROUPS]
    slab = lambda n: pltpu.VMEM((n, seq, D_HEAD), F32)
    return pl.pallas_call(
        functools.partial(_attn_layer_kernel, seq),
        grid=(b, n_pair),
        in_specs=[
            pl.BlockSpec((None, seq, D_MODEL), lambda i, p: (i, 0, 0),
                         pipeline_mode=pl.Buffered(1)),
            pl.BlockSpec((None, D_MODEL, w_pairs.shape[2]), lambda i, p: (p, 0, 0)),
        ],
        out_specs=[pl.BlockSpec((None, seq, 2 * D_HEAD), lambda i, p: (i, 0, p))]
        + [pl.BlockSpec(memory_space=pl.ANY)] * n_g,
        out_shape=[jax.ShapeDtypeStruct((b, seq, WIDTH), BF16)]
        + [jax.ShapeDtypeStruct((b, keep, 2 * WIDTH), F32) for keep in keeps],
        scratch_shapes=[
            pltpu.VMEM((n_g, 2, seq, D_HEAD), F32),
            pltpu.VMEM((n_g, 2, seq, D_HEAD), F32),
            pltpu.VMEM((n_g, 2, seq, D_HEAD), F32),
            slab(2),
            slab(n_g), slab(n_g), slab(n_g),
            pltpu.SemaphoreType.DMA((n_g * 2 * 2,)),
        ],
        compiler_params=_cparams(("arbitrary", "arbitrary")),
        name="attn_layer",
    )(h, w_pairs)


def _attn_sample_kernel(n_new, *refs):
    q_ref, kv0_ref, kv1_ref, kv2_ref, z_ref, c0_ref, c1_ref, c2_ref, o_ref = refs
    n_t = KEYS_PER_QUERY
    ones = jnp.ones((D_HEAD, D_HEAD), BF16)
    dil1 = DIL_GROUPS[1][1]

    def key_tiles(g, l, kv):
        if g == 0:
            return jnp.concatenate([c0_ref[l:, kv], kv0_ref[0:l + 1, kv]], axis=0)
        if g == 1:
            a, r = divmod(l, dil1)
            parts = [c1_ref[a:, r, kv]]
            if a:
                parts.append(kv1_ref[r:r + 1, kv])
            parts.append(kv1_ref[l:l + 1, kv])
            return jnp.concatenate(parts, axis=0)
        return jnp.concatenate([c2_ref[:, l, kv], kv2_ref[l:l + 1, kv]], axis=0)

    for l in range(n_new):
        m_g, l_g, acc_g = [], [], []
        for g in range(len(DIL_GROUPS)):
            q = q_ref[l, g] * ATTN_SCALE
            prod = (key_tiles(g, l, 0) * q[None]).reshape(n_t * N_HEADS, D_HEAD)
            s = jnp.dot(prod.astype(BF16), ones, preferred_element_type=F32)
            s = s.reshape(n_t, N_HEADS, D_HEAD)
            m = jnp.max(s, axis=0)
            p = jnp.exp(s - m[None])
            m_g.append(m)
            l_g.append(jnp.sum(p, axis=0))
            acc_g.append(jnp.sum(p * key_tiles(g, l, 1), axis=0))
        m_all = jnp.maximum(jnp.maximum(m_g[0], m_g[1]), m_g[2])
        w_g = [jnp.exp(m - m_all) for m in m_g]
        num = w_g[0] * acc_g[0] + w_g[1] * acc_g[1] + w_g[2] * acc_g[2]
        den = w_g[0] * l_g[0] + w_g[1] * l_g[1] + w_g[2] * l_g[2]
        o_ref[l] = num / den * _silu(z_ref[l])


def _attn_sample(q, kvs, z, caches):
    b, n_new = q.shape[:2]
    assert n_new == SUBLANES
    (w0, d0), (w1, d1), (w2, d2) = DIL_GROUPS
    assert caches[0].shape[1] == w0 and caches[1].shape[1] == w1 and caches[2].shape[1] == w2
    assert d0 == 1 and n_new % d1 == 0 and d2 == 2 * n_new
    assert w0 // d0 + 1 == KEYS_PER_QUERY and w1 // d1 + 1 == KEYS_PER_QUERY
    assert w2 // d2 + 1 == KEYS_PER_QUERY
    c1 = caches[1].reshape(b, w1 // d1, d1, 2, N_HEADS, D_HEAD)
    c2 = caches[2].reshape(b, w2 // d2, d2, 2, N_HEADS, D_HEAD)
    tail = (2, N_HEADS, D_HEAD)
    new_spec = pl.BlockSpec((None, n_new) + tail, lambda i: (i, 0, 0, 0, 0))
    return pl.pallas_call(
        functools.partial(_attn_sample_kernel, n_new),
        grid=(b,),
        in_specs=[
            pl.BlockSpec((None, n_new, 3, N_HEADS, D_HEAD), lambda i: (i, 0, 0, 0, 0)),
            new_spec, new_spec, new_spec,
            pl.BlockSpec((None, n_new, N_HEADS, D_HEAD), lambda i: (i, 0, 0, 0)),
            pl.BlockSpec((None, w0) + tail, lambda i: (i, 0, 0, 0, 0)),
            pl.BlockSpec((None, w1 // d1, d1) + tail, lambda i: (i, 0, 0, 0, 0, 0)),
            pl.BlockSpec((None, w2 // d2, n_new) + tail, lambda i: (i, 0, 0, 0, 0, 0)),
        ],
        out_specs=pl.BlockSpec((None, n_new, N_HEADS, D_HEAD), lambda i: (i, 0, 0, 0)),
        out_shape=jax.ShapeDtypeStruct((b, n_new, N_HEADS, D_HEAD), F32),
        compiler_params=_cparams(("arbitrary",)),
        name="attn_sample",
    )(q, kvs[0], kvs[1], kvs[2], z, caches[0], c1, c2)


def kernel(x_prompt, x_sample, state_delta, state_conv, cache_kv_w128, cache_kv_w512, cache_kv_w2048,
           c_prompt, c_sample, norm_g, ada_w, ada_b, a_w_in, a_conv_w, a_A_log, a_dt_bias,
           a_out_norm_g, a_w_out, b_w_in, b_w_out, final_norm_g):
    bp, seq, _ = x_prompt.shape
    bs, n_new, _ = x_sample.shape
    n_s = bs * n_new

    mod = _ada_mod(jnp.concatenate([c_prompt, c_sample], axis=0), ada_w, ada_b)
    mod_p = [mod[l, :bp].reshape(bp, 1, 3 * D_MODEL) for l in range(2)]
    mod_s = [jnp.repeat(mod[l, bp:], n_new, axis=0).reshape(1, n_s, 3 * D_MODEL) for l in range(2)]

    w_in = a_w_in[0]
    w_main = w_in[:, :CONV_DIM + WIDTH].astype(BF16)
    w_ab = jnp.zeros((D_MODEL, LANES), BF16).at[:, :2 * N_HEADS].set(
        w_in[:, CONV_DIM + WIDTH:].astype(BF16))
    w_abt = w_in[:, CONV_DIM + WIDTH:].T.astype(BF16)
    w_out_a = a_w_out[0].astype(BF16)
    xs_flat = x_sample.reshape(1, n_s, D_MODEL)

    qkv_p, z_p, ab_p, abt_p = _project(x_prompt, mod_p[0], norm_g[0], w_main, (3, 1), 1024,
                                       w_ab, w_abt)
    o_p, delta_p, conv_p = _gdn_core(qkv_p, z_p, ab_p, abt_p, a_conv_w[0], a_A_log[0],
                                     a_dt_bias[0], a_out_norm_g[0], 512, 128)
    x1_p, h1_p = _out_project(x_prompt, o_p, mod_p[0], w_out_a, norm_g[1], 512, "next_h", mod_p[1])

    qkv_s, z_s, ab_s, abt_s = _project(xs_flat, mod_s[0], norm_g[0], w_main, (3, 1), n_s,
                                       w_ab, w_abt)
    abt_s = abt_s.reshape(16, bs, n_new).transpose(1, 0, 2)
    o_s, delta_s, conv_s = _gdn_core(
        qkv_s.reshape(bs, n_new, CONV_DIM), z_s.reshape(bs, n_new, WIDTH),
        ab_s.reshape(bs, n_new, LANES), abt_s, a_conv_w[0], a_A_log[0], a_dt_bias[0],
        a_out_norm_g[0], n_new, n_new, state_delta[0], state_conv[0])
    x1_s = _out_project(xs_flat, o_s.reshape(1, n_s, WIDTH), mod_s[0], w_out_a, final_norm_g,
                        n_s, "plain")

    wb = b_w_in[0]
    n_g = len(DIL_GROUPS)
    cols = [wb[:, :n_g * WIDTH]]
    for g in range(n_g):
        cols += [wb[:, (n_g + g) * WIDTH:(n_g + g + 1) * WIDTH],
                 wb[:, (2 * n_g + g) * WIDTH:(2 * n_g + g + 1) * WIDTH]]
    cols.append(wb[:, 3 * n_g * WIDTH:])
    w_b = jnp.concatenate(cols, axis=1).astype(BF16)
    w_out_b = b_w_out[0].astype(BF16)
    segs = (3, 2, 2, 2, 1)
    pair = 2 * D_HEAD
    w_pairs = jnp.stack([
        jnp.concatenate([w_b[:, s * WIDTH + p * pair:s * WIDTH + (p + 1) * pair]
                         for s in range(sum(segs))], axis=1)
        for p in range(N_HEADS // 2)])

    ob_p, kv0_p, kv1_p, kv2_p = _attn_layer_prompt(h1_p, w_pairs)
    y_p = _out_project(x1_p, ob_p, mod_p[1], w_out_b, final_norm_g, 512, "final_norm")

    q_s, kv0_s, kv1_s, kv2_s, zb_s = _project(x1_s, mod_s[1], norm_g[1], w_b, segs, n_s)
    kvn = [kv.reshape(bs, n_new, 2, N_HEADS, D_HEAD) for kv in (kv0_s, kv1_s, kv2_s)]
    ob_s = _attn_sample(q_s.reshape(bs, n_new, n_g, N_HEADS, D_HEAD), kvn,
                        zb_s.reshape(bs, n_new, N_HEADS, D_HEAD),
                        (cache_kv_w128[0], cache_kv_w512[0], cache_kv_w2048[0]))
    y_s = _out_project(x1_s, ob_s.reshape(1, n_s, WIDTH), mod_s[1], w_out_b, final_norm_g,
                       n_s, "final_norm")

    def kv_prompt(kv):
        return kv.reshape(1, bp, kv.shape[1], 2, N_HEADS, D_HEAD)

    return (y_p, y_s.reshape(bs, n_new, D_MODEL),
            delta_p[None], delta_s[None], conv_p[None], conv_s[None],
            kv_prompt(kv0_p), kvn[0][None], kv_prompt(kv1_p), kvn[1][None],
            kv_prompt(kv2_p), kvn[2][None])
```

```python
import functools

import jax
import jax.numpy as jnp
from jax import lax
from jax.experimental import pallas as pl
from jax.experimental.pallas import tpu as pltpu

F32 = jnp.float32
BF16 = jnp.bfloat16

D_MODEL = 1024
N_HEADS = 8
D_HEAD = 128
WIDTH = N_HEADS * D_HEAD
CONV_W = 4
CONV_DIM = 3 * WIDTH
DIL_GROUPS = ((128, 1), (512, 4), (2048, 16))
KEYS_PER_QUERY = 129
EPS = 1e-6
ATTN_SCALE = D_HEAD ** -0.5
NEG_BIG = -1e30

LANES = 128
SUBLANES = 8
VMEM_LIMIT_BYTES = 56 * 1024 * 1024


def _cparams(semantics):
    return pltpu.CompilerParams(dimension_semantics=semantics,
                                vmem_limit_bytes=VMEM_LIMIT_BYTES)


def _silu(x):
    return x * (1.0 / (1.0 + jnp.exp(-x)))


def _softplus(x):
    return jnp.maximum(x, 0.0) + jnp.log1p(jnp.exp(-jnp.abs(x)))


def _dot(a, b):
    return jnp.dot(a.astype(BF16), b.astype(BF16), preferred_element_type=F32)


def _dot_nt(a, b):
    return lax.dot_general(a.astype(BF16), b.astype(BF16), (((1,), (1,)), ((), ())),
                           preferred_element_type=F32)


def _dot_tn(a, b):
    return lax.dot_general(a.astype(BF16), b.astype(BF16), (((0,), (0,)), ((), ())),
                           preferred_element_type=F32)


def _split3(x):
    x1 = x.astype(BF16)
    r = x - x1.astype(F32)
    x2 = r.astype(BF16)
    x3 = (r - x2.astype(F32)).astype(BF16)
    return x1, x2, x3


def _dot_sel(sel, x):
    s = sel.astype(BF16)
    out = None
    for p in _split3(x):
        t = jnp.dot(s, p, preferred_element_type=F32)
        out = t if out is None else out + t
    return out


def _dot_x_sel(x, sel):
    s = sel.astype(BF16)
    out = None
    for p in _split3(x):
        t = jnp.dot(p, s, preferred_element_type=F32)
        out = t if out is None else out + t
    return out


def _mod_kernel(c_ref, w_ref, b_ref, o_ref):
    s = _silu(c_ref[...])
    o_ref[...] = _dot(s, w_ref[...]) + b_ref[...]


def _ada_mod(c_all, ada_w, ada_b):
    n_layers = ada_w.shape[0]
    rows = c_all.shape[0]
    tn = 768
    return pl.pallas_call(
        _mod_kernel,
        grid=(n_layers, 3 * D_MODEL // tn),
        in_specs=[
            pl.BlockSpec((rows, D_MODEL), lambda l, j: (0, 0)),
            pl.BlockSpec((None, D_MODEL, tn), lambda l, j: (l, 0, j)),
            pl.BlockSpec((None, 1, tn), lambda l, j: (l, 0, j)),
        ],
        out_specs=pl.BlockSpec((None, rows, tn), lambda l, j: (l, 0, j)),
        out_shape=jax.ShapeDtypeStruct((n_layers, rows, 3 * D_MODEL), F32),
        compiler_params=_cparams(("arbitrary", "arbitrary")),
        name="ada_mod",
    )(c_all, ada_w, ada_b.reshape(n_layers, 1, 3 * D_MODEL))


def _proj_kernel(seg_bounds, has_ab, *refs):
    x_ref, mod_ref, g_ref, w_ref = refs[:4]
    pos = 4
    if has_ab:
        wab_ref, wabt_ref = refs[4:6]
        pos = 6
    n_seg = len(seg_bounds)
    seg_refs = refs[pos:pos + n_seg]
    pos += n_seg
    if has_ab:
        ab_ref, abt_ref = refs[pos:pos + 2]
        pos += 2
    h_ref = refs[pos]
    j = pl.program_id(2)

    @pl.when(j == 0)
    def _():
        hb = _mod_norm(x_ref[...], g_ref[...], mod_ref[...]).astype(BF16)
        h_ref[...] = hb
        if has_ab:
            ab_ref[...] = jnp.dot(hb, wab_ref[...], preferred_element_type=F32)
            abt_ref[...] = lax.dot_general(wabt_ref[...], hb, (((1,), (1,)), ((), ())),
                                           preferred_element_type=F32)

    res = jnp.dot(h_ref[...], w_ref[...], preferred_element_type=F32)
    for (lo, hi), o_ref in zip(seg_bounds, seg_refs):
        @pl.when((j >= lo) & (j < hi))
        def _(o_ref=o_ref):
            o_ref[...] = res


def _project(x, mod, norm_g, w, seg_slabs, tm, w_ab=None, w_abt=None):
    n, t, _ = x.shape
    r = mod.shape[1]
    slab = 1024
    n_slab = w.shape[1] // slab
    assert sum(seg_slabs) == n_slab and t % tm == 0
    has_ab = w_ab is not None
    bounds, lo = [], 0
    for s in seg_slabs:
        bounds.append((lo, lo + s))
        lo += s
    mod_rows = 1 if r == 1 else tm
    mod_map = (lambda b, i, j: (b, 0, 0)) if r == 1 else (lambda b, i, j: (b, i, 0))
    in_specs = [
        pl.BlockSpec((None, tm, D_MODEL), lambda b, i, j: (b, i, 0)),
        pl.BlockSpec((None, mod_rows, 3 * D_MODEL), mod_map),
        pl.BlockSpec((1, D_MODEL), lambda b, i, j: (0, 0)),
        pl.BlockSpec((D_MODEL, slab), lambda b, i, j: (0, j)),
    ]
    args = [x, mod, norm_g.reshape(1, D_MODEL), w]
    if has_ab:
        in_specs += [pl.BlockSpec((D_MODEL, LANES), lambda b, i, j: (0, 0)),
                     pl.BlockSpec((16, D_MODEL), lambda b, i, j: (0, 0))]
        args += [w_ab, w_abt]
    out_specs, out_shapes = [], []
    for (lo, hi) in bounds:
        def seg_map(b, i, j, lo=lo, hi=hi):
            return (b, i, jnp.clip(j - lo, 0, hi - lo - 1))
        out_specs.append(pl.BlockSpec((None, tm, slab), seg_map))
        out_shapes.append(jax.ShapeDtypeStruct((n, t, slab * (hi - lo)), F32))
    if has_ab:
        out_specs += [pl.BlockSpec((None, tm, LANES), lambda b, i, j: (b, i, 0)),
                      pl.BlockSpec((None, 16, tm), lambda b, i, j: (b, 0, i))]
        out_shapes += [jax.ShapeDtypeStruct((n, t, LANES), F32),
                       jax.ShapeDtypeStruct((n, 16, t), F32)]
    return pl.pallas_call(
        functools.partial(_proj_kernel, tuple(bounds), has_ab),
        grid=(n, t // tm, n_slab),
        in_specs=in_specs,
        out_specs=out_specs,
        out_shape=out_shapes,
        scratch_shapes=[pltpu.VMEM((tm, D_MODEL), BF16)],
        compiler_params=_cparams(("arbitrary", "arbitrary", "arbitrary")),
        name="norm_proj",
    )(*args)


def _mod_norm(x, g, mod):
    ms = jnp.mean(x * x, axis=-1, keepdims=True)
    y = x * lax.rsqrt(ms + EPS) * g
    return y * (1.0 + mod[:, D_MODEL:2 * D_MODEL]) + mod[:, 0:D_MODEL]


def _out_kernel(final_norm, x_ref, o_ref, mod_ref, w_ref, fg_ref, out_ref):
    y = _dot(o_ref[...], w_ref[...])
    gate = mod_ref[:, 2 * D_MODEL:3 * D_MODEL]
    x = x_ref[...] + gate * y
    if final_norm:
        ms = jnp.mean(x * x, axis=-1, keepdims=True)
        x = x * lax.rsqrt(ms + EPS) * fg_ref[...]
    out_ref[...] = x


def _out_project(x, o, mod, w_out, final_g, tm, final_norm):
    n, t, _ = x.shape
    r = mod.shape[1]
    mod_rows = 1 if r == 1 else tm
    mod_map = (lambda b, i: (b, 0, 0)) if r == 1 else (lambda b, i: (b, i, 0))
    row_spec = pl.BlockSpec((None, tm, D_MODEL), lambda b, i: (b, i, 0))
    return pl.pallas_call(
        functools.partial(_out_kernel, final_norm),
        grid=(n, t // tm),
        in_specs=[
            row_spec,
            pl.BlockSpec((None, tm, WIDTH), lambda b, i: (b, i, 0)),
            pl.BlockSpec((None, mod_rows, 3 * D_MODEL), mod_map),
            pl.BlockSpec((WIDTH, D_MODEL), lambda b, i: (0, 0)),
            pl.BlockSpec((1, D_MODEL), lambda b, i: (0, 0)),
        ],
        out_specs=row_spec,
        out_shape=jax.ShapeDtypeStruct((n, t, D_MODEL), F32),
        compiler_params=_cparams(("arbitrary", "arbitrary")),
        name="out_proj",
    )(x, o, mod, w_out, final_g.reshape(1, D_MODEL))


def _unit_lower_inverses(n_mats, c):
    row = lax.broadcasted_iota(jnp.int32, (c, c), 0)
    col = lax.broadcasted_iota(jnp.int32, (c, c), 1)
    base = min(16, c)
    blk_id = lambda v, size: jnp.right_shift(v, size.bit_length() - 1)
    eye = (row == col).astype(F32)
    same = blk_id(row, base) == blk_id(col, base)
    ps = [jnp.where(same, -n, 0.0) for n in n_mats]
    invs = [eye + p for p in ps]
    ps = [p.astype(BF16) for p in ps]
    size = 2
    while size < base:
        ps = [_dot(p, p).astype(BF16) for p in ps]
        invs = [inv + _dot(p, inv) for p, inv in zip(ps, invs)]
        size *= 2
    size = base
    while size < c:
        same_next = blk_id(row, 2 * size) == blk_id(col, 2 * size)
        off = same_next & jnp.logical_not(same)
        ls = [jnp.where(off, n, 0.0).astype(BF16) for n in n_mats]
        invs_b = [inv.astype(BF16) for inv in invs]
        ts = [_dot(ib, l) for ib, l in zip(invs_b, ls)]
        invs = [inv - _dot(t, ib) for inv, t, ib in zip(invs, ts, invs_b)]
        same = same_next
        size *= 2
    return invs


def _aligned(v, m):
    return v if isinstance(v, int) else pl.multiple_of(v, m)


def _chunk_masks(c):
    row = lax.broadcasted_iota(jnp.int32, (c, c), 0)
    col = lax.broadcasted_iota(jnp.int32, (c, c), 1)
    tril_sel = jnp.where(row >= col, 1.0, 0.0).astype(BF16)
    triu_sel = jnp.where(row <= col, 1.0, 0.0).astype(BF16)
    return row >= col, row > col, tril_sel, triu_sel


def _gdn_chunk(c, r0, masks, ext_ref, z_ref, ab_ref, g_ref, gt_ref, cw_ref, og_ref, s_ref):
    tril, strict, tril_sel, triu_sel = masks
    pad = SUBLANES
    halo = CONV_W - 1
    heads = range(N_HEADS)
    gc_all = _dot_sel(tril_sel, g_ref[pl.ds(r0, c), :])
    gct_all = _dot_x_sel(gt_ref[0:N_HEADS, pl.ds(r0, c)], triu_sel)
    beta_all = 1.0 / (1.0 + jnp.exp(-ab_ref[pl.ds(r0, c), :]))

    def conv(off, h):
        cols = slice(off + h * D_HEAD, off + (h + 1) * D_HEAD)
        win = ext_ref.at[pl.ds(r0, c + pad)]
        acc = None
        for i in range(CONV_W):
            lo = pad - halo + i
            term = win[lo:lo + c, cols] * cw_ref[i:i + 1, cols]
            acc = term if acc is None else acc + term
        return _silu(acc)

    qs, ks, vs = [], [], []
    for h in heads:
        q = conv(0, h)
        k = conv(WIDTH, h)
        qs.append(q * lax.rsqrt(jnp.sum(q * q, axis=-1, keepdims=True) + EPS)
                  * (D_HEAD ** -0.5))
        ks.append(k * lax.rsqrt(jnp.sum(k * k, axis=-1, keepdims=True) + EPS))
        vs.append(conv(2 * WIDTH, h))
    gcs = [jnp.broadcast_to(gc_all[:, h:h + 1], (c, D_HEAD)) for h in heads]
    betas = [jnp.broadcast_to(beta_all[:, N_HEADS + h:N_HEADS + h + 1], (c, D_HEAD))
             for h in heads]
    g_lasts = [gc[c - 1:c, :] for gc in gcs]
    decays = [jnp.where(tril, jnp.exp(jnp.where(tril, gcs[h][:, 0:c] - gct_all[h:h + 1, :],
                                                0.0)), 0.0) for h in heads]
    kbs = [k * beta for k, beta in zip(ks, betas)]
    ks_b = [k.astype(BF16) for k in ks]
    kks = [_dot_nt(kb, k) for kb, k in zip(kbs, ks_b)]
    qks = [_dot_nt(q, k) for q, k in zip(qs, ks_b)]
    n_mats = [jnp.where(strict, kk * decay, 0.0) for kk, decay in zip(kks, decays)]
    qks = [(qk * decay).astype(BF16) for qk, decay in zip(qks, decays)]
    invs = _unit_lower_inverses(n_mats, c)
    e_gcs = [jnp.exp(gc) for gc in gcs]
    rhss = [jnp.concatenate([v * beta, kb * e_gc], axis=1).astype(BF16)
            for v, beta, kb, e_gc in zip(vs, betas, kbs, e_gcs)]
    sols = [_dot(inv, rhs) for inv, rhs in zip(invs, rhss)]
    q_decs = [(q * e_gc).astype(BF16) for q, e_gc in zip(qs, e_gcs)]
    k_decs = [(k * jnp.exp(g_last - gc)).astype(BF16)
              for k, g_last, gc in zip(ks, g_lasts, gcs)]
    s_olds = [s_ref[h] for h in heads]
    s_bs = [s.astype(BF16) for s in s_olds]
    us = [sol[:, 0:D_HEAD] - _dot(sol[:, D_HEAD:2 * D_HEAD], s_b)
          for sol, s_b in zip(sols, s_bs)]
    us_b = [u.astype(BF16) for u in us]
    os_ = [_dot(q_dec, s_b) + _dot(qk, u_b)
           for q_dec, s_b, qk, u_b in zip(q_decs, s_bs, qks, us_b)]
    for h in heads:
        s_ref[h] = s_olds[h] * jnp.exp(g_lasts[h]) + _dot_tn(k_decs[h], us_b[h])
    outs = []
    for h in heads:
        o = os_[h]
        o = o * lax.rsqrt(jnp.mean(o * o, axis=-1, keepdims=True) + EPS) * og_ref[...]
        outs.append(o * _silu(z_ref[pl.ds(r0, c), h * D_HEAD:(h + 1) * D_HEAD]))
    return outs


def _decay_logits(ab, alog, dtb):
    return -jnp.exp(alog) * _softplus(ab + dtb)


def _gdn_kernel(tm, c, has_state, *refs):
    (qkv_ref, z_ref, ab_ref, abt_ref, cw_ref, alog_ref, dtb_ref, alogc_ref, dtbc_ref,
     og_ref) = refs[:10]
    pos = 10
    if has_state:
        s0_ref, conv0_ref = refs[10:12]
        pos = 12
    o_ref, s_ref, conv_ref, ext_ref, g_ref, gt_ref = refs[pos:pos + 6]
    t = pl.program_id(1)
    pad = SUBLANES
    halo = CONV_W - 1

    @pl.when(t == 0)
    def _():
        if has_state:
            s_ref[...] = s0_ref[...]
            ext_ref[pad - halo:pad, :] = conv0_ref[...]
        else:
            s_ref[...] = jnp.zeros_like(s_ref)
            ext_ref[pad - halo:pad, :] = jnp.zeros((halo, CONV_DIM), F32)

    ext_ref[pad:pad + tm, :] = qkv_ref[...]

    g_ref[...] = _decay_logits(ab_ref[...], alog_ref[...], dtb_ref[...])
    gt_ref[...] = _decay_logits(abt_ref[...], alogc_ref[...], dtbc_ref[...])

    masks = _chunk_masks(c)

    def chunk(ci, carry):
        r0 = _aligned(ci * c, c)
        outs = _gdn_chunk(c, r0, masks, ext_ref, z_ref, ab_ref, g_ref, gt_ref, cw_ref, og_ref, s_ref)
        for h, o in enumerate(outs):
            o_ref[pl.ds(r0, c), h * D_HEAD:(h + 1) * D_HEAD] = o
        return carry

    if tm == c:
        chunk(0, 0)
    else:
        lax.fori_loop(0, tm // c, chunk, 0)

    tail = ext_ref[pad + tm - halo:pad + tm, :]
    ext_ref[pad - halo:pad, :] = tail
    conv_ref[...] = tail


def _gdn_core(qkv, z, ab, abt, conv_w, a_log, dt_bias, out_g, tm, c, s0=None, conv0=None):
    b, t, _ = qkv.shape
    has_state = s0 is not None
    const = lambda shape: pl.BlockSpec(shape, lambda i, j: (0,) * len(shape))
    in_specs = [
        pl.BlockSpec((None, tm, CONV_DIM), lambda i, j: (i, j, 0)),
        pl.BlockSpec((None, tm, WIDTH), lambda i, j: (i, j, 0)),
        pl.BlockSpec((None, tm, LANES), lambda i, j: (i, j, 0)),
        pl.BlockSpec((None, 16, tm), lambda i, j: (i, 0, j)),
        const((CONV_W, CONV_DIM)), const((1, LANES)), const((1, LANES)),
        const((16, 1)), const((16, 1)), const((1, D_HEAD)),
    ]
    args = [qkv, z, ab, abt, conv_w, *_head_params(a_log, dt_bias), out_g.reshape(1, D_HEAD)]
    if has_state:
        in_specs += [pl.BlockSpec((None, N_HEADS, D_HEAD, D_HEAD), lambda i, j: (i, 0, 0, 0)),
                     pl.BlockSpec((None, CONV_W - 1, CONV_DIM), lambda i, j: (i, 0, 0))]
        args += [s0, conv0]
    return pl.pallas_call(
        functools.partial(_gdn_kernel, tm, c, has_state),
        grid=(b, t // tm),
        in_specs=in_specs,
        out_specs=[
            pl.BlockSpec((None, tm, WIDTH), lambda i, j: (i, j, 0)),
            pl.BlockSpec((None, N_HEADS, D_HEAD, D_HEAD), lambda i, j: (i, 0, 0, 0)),
            pl.BlockSpec((None, CONV_W - 1, CONV_DIM), lambda i, j: (i, 0, 0)),
        ],
        out_shape=[
            jax.ShapeDtypeStruct((b, t, WIDTH), F32),
            jax.ShapeDtypeStruct((b, N_HEADS, D_HEAD, D_HEAD), F32),
            jax.ShapeDtypeStruct((b, CONV_W - 1, CONV_DIM), F32),
        ],
        scratch_shapes=[
            pltpu.VMEM((tm + SUBLANES, CONV_DIM), F32),
            pltpu.VMEM((tm, LANES), F32),
            pltpu.VMEM((16, tm), F32),
        ],
        compiler_params=_cparams(("arbitrary", "arbitrary")),
        name="gdn_core",
    )(*args)


def _head_params(a_log, dt_bias):
    row = lambda v: jnp.zeros((1, LANES), F32).at[0, :N_HEADS].set(v)
    col = lambda v: jnp.zeros((16, 1), F32).at[:N_HEADS, 0].set(v)
    return row(a_log), row(dt_bias), col(a_log), col(dt_bias)


def _gdn_layer_kernel(tm, c, x_ref, mod_ref, modn_ref, g_ref, gn_ref, w_ref, wab_ref, wabt_ref,
                      wout_ref, cw_ref, alog_ref, dtb_ref, alogc_ref, dtbc_ref, og_ref,
                      x1_ref, h1_ref, s_ref, conv_ref, ext_ref, z_s, ab_s, gl_s, gt_s):
    t = pl.program_id(1)
    pad = SUBLANES
    halo = CONV_W - 1
    n_chunks = tm // c
    n_col = 512

    @pl.when(t == 0)
    def _():
        s_ref[...] = jnp.zeros_like(s_ref)
        ext_ref[pad - halo:pad, :] = jnp.zeros((halo, CONV_DIM), F32)

    def project(r0):
        hb = _mod_norm(x_ref[pl.ds(r0, c), :], g_ref[...], mod_ref[...]).astype(BF16)
        for lo in range(0, CONV_DIM + WIDTH, n_col):
            res = jnp.dot(hb, w_ref[:, lo:lo + n_col], preferred_element_type=F32)
            if lo < CONV_DIM:
                ext_ref[pl.ds(_aligned(pad + r0, pad), c), lo:lo + n_col] = res
            else:
                z_s[pl.ds(r0, c), lo - CONV_DIM:lo - CONV_DIM + n_col] = res
        ab = jnp.dot(hb, wab_ref[...], preferred_element_type=F32)
        ab_s[pl.ds(r0, c), :] = ab
        gl_s[pl.ds(r0, c), :] = _decay_logits(ab, alog_ref[...], dtb_ref[...])
        abt = lax.dot_general(wabt_ref[...], hb, (((1,), (1,)), ((), ())),
                              preferred_element_type=F32)
        gt_s[:, pl.ds(r0, c)] = _decay_logits(abt, alogc_ref[...], dtbc_ref[...])

    project(0)
    masks = _chunk_masks(c)

    def chunk(ci, carry):
        r0 = pl.multiple_of(ci * c, c)
        outs = _gdn_chunk(c, r0, masks, ext_ref, z_s, ab_s, gl_s, gt_s, cw_ref, og_ref, s_ref)
        o = jnp.concatenate([o.astype(BF16) for o in outs], axis=1)
        y = jnp.dot(o, wout_ref[...], preferred_element_type=F32)
        x1 = x_ref[pl.ds(r0, c), :] + mod_ref[:, 2 * D_MODEL:3 * D_MODEL] * y
        x1_ref[pl.ds(r0, c), :] = x1
        h1_ref[pl.ds(r0, c), :] = _mod_norm(x1, gn_ref[...], modn_ref[...]).astype(BF16)
        project(pl.multiple_of(jnp.minimum(ci + 1, n_chunks - 1) * c, c))
        return carry

    lax.fori_loop(0, n_chunks, chunk, 0)

    tail = ext_ref[pad + tm - halo:pad + tm, :]
    ext_ref[pad - halo:pad, :] = tail
    conv_ref[...] = tail


def _gdn_layer_prompt(x, mod, mod_next, norm_g, norm_g_next, w_main, w_ab, w_abt, w_out, conv_w,
                      a_log, dt_bias, out_g, tm, c):
    b, t, _ = x.shape
    const = lambda shape: pl.BlockSpec(shape, lambda i, j: (0,) * len(shape))
    resident = lambda shape: pl.BlockSpec(shape, lambda i, j: (0,) * len(shape),
                                          pipeline_mode=pl.Buffered(1))
    mod_spec = pl.BlockSpec((None, 1, 3 * D_MODEL), lambda i, j: (i, 0, 0))
    row_spec = pl.BlockSpec((None, tm, D_MODEL), lambda i, j: (i, j, 0))
    return pl.pallas_call(
        functools.partial(_gdn_layer_kernel, tm, c),
        grid=(b, t // tm),
        in_specs=[
            row_spec, mod_spec, mod_spec, const((1, D_MODEL)), const((1, D_MODEL)),
            resident((D_MODEL, CONV_DIM + WIDTH)), const((D_MODEL, LANES)), const((16, D_MODEL)),
            resident((WIDTH, D_MODEL)), const((CONV_W, CONV_DIM)),
            const((1, LANES)), const((1, LANES)), const((16, 1)), const((16, 1)),
            const((1, D_HEAD)),
        ],
        out_specs=[
            row_spec, row_spec,
            pl.BlockSpec((None, N_HEADS, D_HEAD, D_HEAD), lambda i, j: (i, 0, 0, 0)),
            pl.BlockSpec((None, CONV_W - 1, CONV_DIM), lambda i, j: (i, 0, 0)),
        ],
        out_shape=[
            jax.ShapeDtypeStruct((b, t, D_MODEL), F32),
            jax.ShapeDtypeStruct((b, t, D_MODEL), BF16),
            jax.ShapeDtypeStruct((b, N_HEADS, D_HEAD, D_HEAD), F32),
            jax.ShapeDtypeStruct((b, CONV_W - 1, CONV_DIM), F32),
        ],
        scratch_shapes=[
            pltpu.VMEM((tm + SUBLANES, CONV_DIM), F32),
            pltpu.VMEM((tm, WIDTH), F32),
            pltpu.VMEM((tm, LANES), F32),
            pltpu.VMEM((tm, LANES), F32),
            pltpu.VMEM((16, tm), F32),
        ],
        compiler_params=_cparams(("arbitrary", "arbitrary")),
        name="gdn_layer",
    )(x, mod, mod_next, norm_g.reshape(1, D_MODEL), norm_g_next.reshape(1, D_MODEL),
      w_main, w_ab, w_abt, w_out, conv_w, *_head_params(a_log, dt_bias), out_g.reshape(1, D_HEAD))


def _banded_attention(dil, seq, q_ref, k_ref, v_ref, acc_ref, m_ref, l_ref):
    blk = D_HEAD
    per_step = 8
    n_blk = seq // dil // blk
    with_prev = n_blk > 1
    n_keys = 2 * blk if with_prev else blk
    row = lax.broadcasted_iota(jnp.int32, (blk, n_keys), 0)
    col = lax.broadcasted_iota(jnp.int32, (blk, n_keys), 1)
    is_prev = col < (n_keys - blk)
    slack = jnp.where(is_prev, col - row, row + (n_keys - blk) - col)
    cur_ok = slack >= 0
    ones = jnp.ones((n_keys, blk), BF16)

    def rows(start):
        if dil == 1:
            return pl.ds(start, blk)
        return pl.ds(start, blk, stride=dil)

    def body(i, carry):
        starts, masks, qs, kcs, vcs = [], [], [], [], []
        for u in range(per_step):
            idx = i * per_step + u
            r = idx // n_blk
            n = idx % n_blk
            start = r + dil * blk * n
            starts.append(start)
            qs.append(q_ref[rows(start), :].astype(BF16))
            k_cur = k_ref[rows(start), :].astype(BF16)
            v_cur = v_ref[rows(start), :].astype(BF16)
            if with_prev:
                start_p = r + dil * blk * jnp.maximum(n - 1, 0)
                first = jnp.where(n > 0, 0, 2 * blk)
                masks.append(slack - jnp.where(is_prev, first, 0) >= 0)
                k_cur = jnp.concatenate([k_ref[rows(start_p), :].astype(BF16), k_cur], axis=0)
                v_cur = jnp.concatenate([v_ref[rows(start_p), :].astype(BF16), v_cur], axis=0)
            else:
                masks.append(cur_ok)
            kcs.append(k_cur)
            vcs.append(jnp.concatenate([v_cur, ones], axis=1))
        ss = [jnp.where(mask, _dot_nt(q, kc), NEG_BIG) for q, kc, mask in zip(qs, kcs, masks)]
        ms = [jnp.max(s, axis=-1, keepdims=True) for s in ss]
        ps = [jnp.exp(s - m).astype(BF16) for s, m in zip(ss, ms)]
        accs = [jnp.dot(p, vc, preferred_element_type=F32) for p, vc in zip(ps, vcs)]
        for start, acc, m in zip(starts, accs, ms):
            acc_ref[rows(start), :] = acc[:, 0:blk]
            l_ref[rows(start), :] = acc[:, blk:2 * blk]
            m_ref[rows(start), :] = jnp.broadcast_to(m, (blk, blk))
        return carry

    lax.fori_loop(0, dil * n_blk // per_step, body, 0)


def _attn_layer_kernel(seq, h_ref, w_ref, o_ref, kv0_ref, kv1_ref, kv2_ref,
                       q_s, k_s, v_s, z_s, acc_s, m_s, l_s, sems):
    b = pl.program_id(0)
    hp = pl.program_id(1)
    n_g = len(DIL_GROUPS)
    pair = 2 * D_HEAD
    kv_refs = (kv0_ref, kv1_ref, kv2_ref)

    def proj(slab):
        return jnp.dot(h_ref[...], w_ref[:, slab * pair:(slab + 1) * pair],
                       preferred_element_type=F32)

    def kv_copy(g, i, kv):
        keep = kv_refs[g].shape[1]
        src = (k_s, v_s)[kv].at[g, i, pl.ds(seq - keep, keep), :]
        col = pl.multiple_of(kv * WIDTH + (2 * hp + i) * D_HEAD, D_HEAD)
        dst = kv_refs[g].at[b, :, pl.ds(col, D_HEAD)]
        return pltpu.make_async_copy(src, dst, sems.at[(g * 2 + i) * 2 + kv])

    for g in range(n_g):
        k = proj(n_g + 2 * g)
        v = proj(n_g + 2 * g + 1)
        for i in range(2):
            k_s[g, i] = k[:, i * D_HEAD:(i + 1) * D_HEAD]
            v_s[g, i] = v[:, i * D_HEAD:(i + 1) * D_HEAD]
        for i in range(2):
            kv_copy(g, i, 0).start()
            kv_copy(g, i, 1).start()
        q = proj(g) * ATTN_SCALE
        for i in range(2):
            q_s[g, i] = q[:, i * D_HEAD:(i + 1) * D_HEAD]
    z = proj(3 * n_g)
    for i in range(2):
        z_s[i] = z[:, i * D_HEAD:(i + 1) * D_HEAD]

    for i in range(2):
        for g, (_, dil) in enumerate(DIL_GROUPS):
            _banded_attention(dil, seq, q_s.at[g, i], k_s.at[g, i], v_s.at[g, i],
                              acc_s.at[g], m_s.at[g], l_s.at[g])
        m_all = jnp.maximum(jnp.maximum(m_s[0], m_s[1]), m_s[2])
        num = jnp.zeros((seq, D_HEAD), F32)
        den = jnp.zeros((seq, D_HEAD), F32)
        for g in range(n_g):
            w = jnp.exp(m_s[g] - m_all)
            num = num + w * acc_s[g]
            den = den + w * l_s[g]
        o_ref[:, i * D_HEAD:(i + 1) * D_HEAD] = (num / den * _silu(z_s[i])).astype(BF16)

    for g in range(n_g):
        for i in range(2):
            kv_copy(g, i, 0).wait()
            kv_copy(g, i, 1).wait()


def _attn_layer_prompt(h, w_pairs):
    b, seq, _ = h.shape
    n_pair = N_HEADS // 2
    n_g = len(DIL_GROUPS)
    keeps = [min(window, seq) for window, _ in DIL_GROUPS]
    slab = lambda n: pltpu.VMEM((n, seq, D_HEAD), F32)
    return pl.pallas_call(
        functools.partial(_attn_layer_kernel, seq),
        grid=(b, n_pair),
        in_specs=[
            pl.BlockSpec((None, seq, D_MODEL), lambda i, p: (i, 0, 0),
                         pipeline_mode=pl.Buffered(1)),
            pl.BlockSpec((None, D_MODEL, w_pairs.shape[2]), lambda i, p: (p, 0, 0)),
        ],
        out_specs=[pl.BlockSpec((None, seq, 2 * D_HEAD), lambda i, p: (i, 0, p))]
        + [pl.BlockSpec(memory_space=pl.ANY)] * n_g,
        out_shape=[jax.ShapeDtypeStruct((b, seq, WIDTH), BF16)]
        + [jax.ShapeDtypeStruct((b, keep, 2 * WIDTH), F32) for keep in keeps],
        scratch_shapes=[
            pltpu.VMEM((n_g, 2, seq, D_HEAD), F32),
            pltpu.VMEM((n_g, 2, seq, D_HEAD), F32),
            pltpu.VMEM((n_g, 2, seq, D_HEAD), F32),
            slab(2),
            slab(n_g), slab(n_g), slab(n_g),
            pltpu.SemaphoreType.DMA((n_g * 2 * 2,)),
        ],
        compiler_params=_cparams(("arbitrary", "arbitrary")),
        name="attn_layer",
    )(h, w_pairs)


def _attn_sample_kernel(n_new, *refs):
    q_ref, kv0_ref, kv1_ref, kv2_ref, z_ref, c0_ref, c1_ref, c2_ref, o_ref = refs
    n_t = KEYS_PER_QUERY
    ones = jnp.ones((D_HEAD, D_HEAD), BF16)
    dil1 = DIL_GROUPS[1][1]

    def key_tiles(g, l, kv):
        if g == 0:
            return jnp.concatenate([c0_ref[l:, kv], kv0_ref[0:l + 1, kv]], axis=0)
        if g == 1:
            a, r = divmod(l, dil1)
            parts = [c1_ref[a:, r, kv]]
            if a:
                parts.append(kv1_ref[r:r + 1, kv])
            parts.append(kv1_ref[l:l + 1, kv])
            return jnp.concatenate(parts, axis=0)
        return jnp.concatenate([c2_ref[:, l, kv], kv2_ref[l:l + 1, kv]], axis=0)

    for l in range(n_new):
        m_g, l_g, acc_g = [], [], []
        for g in range(len(DIL_GROUPS)):
            q = q_ref[l, g] * ATTN_SCALE
            prod = (key_tiles(g, l, 0) * q[None]).reshape(n_t * N_HEADS, D_HEAD)
            s = jnp.dot(prod.astype(BF16), ones, preferred_element_type=F32)
            s = s.reshape(n_t, N_HEADS, D_HEAD)
            m = jnp.max(s, axis=0)
            p = jnp.exp(s - m[None])
            m_g.append(m)
            l_g.append(jnp.sum(p, axis=0))
            acc_g.append(jnp.sum(p * key_tiles(g, l, 1), axis=0))
        m_all = jnp.maximum(jnp.maximum(m_g[0], m_g[1]), m_g[2])
        w_g = [jnp.exp(m - m_all) for m in m_g]
        num = w_g[0] * acc_g[0] + w_g[1] * acc_g[1] + w_g[2] * acc_g[2]
        den = w_g[0] * l_g[0] + w_g[1] * l_g[1] + w_g[2] * l_g[2]
        o_ref[l] = num / den * _silu(z_ref[l])


def _attn_sample(q, kvs, z, caches):
    b, n_new = q.shape[:2]
    assert n_new == SUBLANES
    (w0, d0), (w1, d1), (w2, d2) = DIL_GROUPS
    assert caches[0].shape[1] == w0 and caches[1].shape[1] == w1 and caches[2].shape[1] == w2
    assert d0 == 1 and n_new % d1 == 0 and d2 == 2 * n_new
    assert w0 // d0 + 1 == KEYS_PER_QUERY and w1 // d1 + 1 == KEYS_PER_QUERY
    assert w2 // d2 + 1 == KEYS_PER_QUERY
    c1 = caches[1].reshape(b, w1 // d1, d1, 2, N_HEADS, D_HEAD)
    c2 = caches[2].reshape(b, w2 // d2, d2, 2, N_HEADS, D_HEAD)
    tail = (2, N_HEADS, D_HEAD)
    new_spec = pl.BlockSpec((None, n_new) + tail, lambda i: (i, 0, 0, 0, 0))
    return pl.pallas_call(
        functools.partial(_attn_sample_kernel, n_new),
        grid=(b,),
        in_specs=[
            pl.BlockSpec((None, n_new, 3, N_HEADS, D_HEAD), lambda i: (i, 0, 0, 0, 0)),
            new_spec, new_spec, new_spec,
            pl.BlockSpec((None, n_new, N_HEADS, D_HEAD), lambda i: (i, 0, 0, 0)),
            pl.BlockSpec((None, w0) + tail, lambda i: (i, 0, 0, 0, 0)),
            pl.BlockSpec((None, w1 // d1, d1) + tail, lambda i: (i, 0, 0, 0, 0, 0)),
            pl.BlockSpec((None, w2 // d2, n_new) + tail, lambda i: (i, 0, 0, 0, 0, 0)),
        ],
        out_specs=pl.BlockSpec((None, n_new, N_HEADS, D_HEAD), lambda i: (i, 0, 0, 0)),
        out_shape=jax.ShapeDtypeStruct((b, n_new, N_HEADS, D_HEAD), F32),
        compiler_params=_cparams(("arbitrary",)),
        name="attn_sample",
    )(q, kvs[0], kvs[1], kvs[2], z, caches[0], c1, c2)


def kernel(x_prompt, x_sample, state_delta, state_conv, cache_kv_w128, cache_kv_w512, cache_kv_w2048,
           c_prompt, c_sample, norm_g, ada_w, ada_b, a_w_in, a_conv_w, a_A_log, a_dt_bias,
           a_out_norm_g, a_w_out, b_w_in, b_w_out, final_norm_g):
    bp, seq, _ = x_prompt.shape
    bs, n_new, _ = x_sample.shape
    n_s = bs * n_new

    mod = _ada_mod(jnp.concatenate([c_prompt, c_sample], axis=0), ada_w, ada_b)
    mod_p = [mod[l, :bp].reshape(bp, 1, 3 * D_MODEL) for l in range(2)]
    mod_s = [jnp.repeat(mod[l, bp:], n_new, axis=0).reshape(1, n_s, 3 * D_MODEL) for l in range(2)]

    w_in = a_w_in[0]
    w_main = w_in[:, :CONV_DIM + WIDTH].astype(BF16)
    w_ab = jnp.zeros((D_MODEL, LANES), BF16).at[:, :2 * N_HEADS].set(
        w_in[:, CONV_DIM + WIDTH:].astype(BF16))
    w_abt = w_in[:, CONV_DIM + WIDTH:].T.astype(BF16)
    w_out_a = a_w_out[0].astype(BF16)
    xs_flat = x_sample.reshape(1, n_s, D_MODEL)

    x1_p, h1_p, delta_p, conv_p = _gdn_layer_prompt(
        x_prompt, mod_p[0], mod_p[1], norm_g[0], norm_g[1], w_main, w_ab, w_abt, w_out_a,
        a_conv_w[0], a_A_log[0], a_dt_bias[0], a_out_norm_g[0], 512, 128)

    qkv_s, z_s, ab_s, abt_s = _project(xs_flat, mod_s[0], norm_g[0], w_main, (3, 1), n_s,
                                       w_ab, w_abt)
    abt_s = abt_s.reshape(16, bs, n_new).transpose(1, 0, 2)
    o_s, delta_s, conv_s = _gdn_core(
        qkv_s.reshape(bs, n_new, CONV_DIM), z_s.reshape(bs, n_new, WIDTH),
        ab_s.reshape(bs, n_new, LANES), abt_s, a_conv_w[0], a_A_log[0], a_dt_bias[0],
        a_out_norm_g[0], n_new, n_new, state_delta[0], state_conv[0])
    x1_s = _out_project(xs_flat, o_s.reshape(1, n_s, WIDTH), mod_s[0], w_out_a, final_norm_g,
                        n_s, False)

    wb = b_w_in[0]
    n_g = len(DIL_GROUPS)
    cols = [wb[:, :n_g * WIDTH]]
    for g in range(n_g):
        cols += [wb[:, (n_g + g) * WIDTH:(n_g + g + 1) * WIDTH],
                 wb[:, (2 * n_g + g) * WIDTH:(2 * n_g + g + 1) * WIDTH]]
    cols.append(wb[:, 3 * n_g * WIDTH:])
    w_b = jnp.concatenate(cols, axis=1).astype(BF16)
    w_out_b = b_w_out[0].astype(BF16)
    segs = (3, 2, 2, 2, 1)
    pair = 2 * D_HEAD
    w_pairs = jnp.stack([
        jnp.concatenate([w_b[:, s * WIDTH + p * pair:s * WIDTH + (p + 1) * pair]
                         for s in range(sum(segs))], axis=1)
        for p in range(N_HEADS // 2)])

    ob_p, kv0_p, kv1_p, kv2_p = _attn_layer_prompt(h1_p, w_pairs)
    y_p = _out_project(x1_p, ob_p, mod_p[1], w_out_b, final_norm_g, 512, True)

    q_s, kv0_s, kv1_s, kv2_s, zb_s = _project(x1_s, mod_s[1], norm_g[1], w_b, segs, n_s)
    kvn = [kv.reshape(bs, n_new, 2, N_HEADS, D_HEAD) for kv in (kv0_s, kv1_s, kv2_s)]
    ob_s = _attn_sample(q_s.reshape(bs, n_new, n_g, N_HEADS, D_HEAD), kvn,
                        zb_s.reshape(bs, n_new, N_HEADS, D_HEAD),
                        (cache_kv_w128[0], cache_kv_w512[0], cache_kv_w2048[0]))
    y_s = _out_project(x1_s, ob_s.reshape(1, n_s, WIDTH), mod_s[1], w_out_b, final_norm_g,
                       n_s, True)

    def kv_prompt(kv):
        return kv.reshape(1, bp, kv.shape[1], 2, N_HEADS, D_HEAD)

    return (y_p, y_s.reshape(bs, n_new, D_MODEL),
            delta_p[None], delta_s[None], conv_p[None], conv_s[None],
            kv_prompt(kv0_p), kvn[0][None], kv_prompt(kv1_p), kvn[1][None],
            kv_prompt(kv2_p), kvn[2][None])
```

```python
import functools

import jax
import jax.numpy as jnp
from jax import lax
from jax.experimental import pallas as pl
from jax.experimental.pallas import tpu as pltpu

F32 = jnp.float32
BF16 = jnp.bfloat16

D_MODEL = 1024
N_HEADS = 8
D_HEAD = 128
WIDTH = N_HEADS * D_HEAD
CONV_W = 4
CONV_DIM = 3 * WIDTH
DIL_GROUPS = ((128, 1), (512, 4), (2048, 16))
KEYS_PER_QUERY = 129
EPS = 1e-6
ATTN_SCALE = D_HEAD ** -0.5
NEG_BIG = -1e30

LANES = 128
SUBLANES = 8
VMEM_LIMIT_BYTES = 56 * 1024 * 1024


def _cparams(semantics):
    return pltpu.CompilerParams(dimension_semantics=semantics,
                                vmem_limit_bytes=VMEM_LIMIT_BYTES)


def _silu(x):
    return x * (1.0 / (1.0 + jnp.exp(-x)))


def _softplus(x):
    return jnp.maximum(x, 0.0) + jnp.log1p(jnp.exp(-jnp.abs(x)))


def _dot(a, b):
    return jnp.dot(a.astype(BF16), b.astype(BF16), preferred_element_type=F32)


def _dot_nt(a, b):
    return lax.dot_general(a.astype(BF16), b.astype(BF16), (((1,), (1,)), ((), ())),
                           preferred_element_type=F32)


def _dot_tn(a, b):
    return lax.dot_general(a.astype(BF16), b.astype(BF16), (((0,), (0,)), ((), ())),
                           preferred_element_type=F32)


def _split3(x):
    x1 = x.astype(BF16)
    r = x - x1.astype(F32)
    x2 = r.astype(BF16)
    x3 = (r - x2.astype(F32)).astype(BF16)
    return x1, x2, x3


def _dot_sel(sel, x):
    s = sel.astype(BF16)
    out = None
    for p in _split3(x):
        t = jnp.dot(s, p, preferred_element_type=F32)
        out = t if out is None else out + t
    return out


def _dot_x_sel(x, sel):
    s = sel.astype(BF16)
    out = None
    for p in _split3(x):
        t = jnp.dot(p, s, preferred_element_type=F32)
        out = t if out is None else out + t
    return out


def _mod_kernel(c_ref, w_ref, b_ref, o_ref):
    s = _silu(c_ref[...])
    o_ref[...] = _dot(s, w_ref[...]) + b_ref[...]


def _ada_mod(c_all, ada_w, ada_b):
    n_layers = ada_w.shape[0]
    rows = c_all.shape[0]
    tn = 768
    return pl.pallas_call(
        _mod_kernel,
        grid=(n_layers, 3 * D_MODEL // tn),
        in_specs=[
            pl.BlockSpec((rows, D_MODEL), lambda l, j: (0, 0)),
            pl.BlockSpec((None, D_MODEL, tn), lambda l, j: (l, 0, j)),
            pl.BlockSpec((None, 1, tn), lambda l, j: (l, 0, j)),
        ],
        out_specs=pl.BlockSpec((None, rows, tn), lambda l, j: (l, 0, j)),
        out_shape=jax.ShapeDtypeStruct((n_layers, rows, 3 * D_MODEL), F32),
        compiler_params=_cparams(("arbitrary", "arbitrary")),
        name="ada_mod",
    )(c_all, ada_w, ada_b.reshape(n_layers, 1, 3 * D_MODEL))


def _proj_kernel(seg_bounds, has_ab, *refs):
    x_ref, mod_ref, g_ref, w_ref = refs[:4]
    pos = 4
    if has_ab:
        wab_ref, wabt_ref = refs[4:6]
        pos = 6
    n_seg = len(seg_bounds)
    seg_refs = refs[pos:pos + n_seg]
    pos += n_seg
    if has_ab:
        ab_ref, abt_ref = refs[pos:pos + 2]
        pos += 2
    h_ref = refs[pos]
    j = pl.program_id(2)

    @pl.when(j == 0)
    def _():
        hb = _mod_norm(x_ref[...], g_ref[...], mod_ref[...]).astype(BF16)
        h_ref[...] = hb
        if has_ab:
            ab_ref[...] = jnp.dot(hb, wab_ref[...], preferred_element_type=F32)
            abt_ref[...] = lax.dot_general(wabt_ref[...], hb, (((1,), (1,)), ((), ())),
                                           preferred_element_type=F32)

    res = jnp.dot(h_ref[...], w_ref[...], preferred_element_type=F32)
    for (lo, hi), o_ref in zip(seg_bounds, seg_refs):
        @pl.when((j >= lo) & (j < hi))
        def _(o_ref=o_ref):
            o_ref[...] = res


def _project(x, mod, norm_g, w, seg_slabs, tm, w_ab=None, w_abt=None):
    n, t, _ = x.shape
    r = mod.shape[1]
    slab = 1024
    n_slab = w.shape[1] // slab
    assert sum(seg_slabs) == n_slab and t % tm == 0
    has_ab = w_ab is not None
    bounds, lo = [], 0
    for s in seg_slabs:
        bounds.append((lo, lo + s))
        lo += s
    mod_rows = 1 if r == 1 else tm
    mod_map = (lambda b, i, j: (b, 0, 0)) if r == 1 else (lambda b, i, j: (b, i, 0))
    in_specs = [
        pl.BlockSpec((None, tm, D_MODEL), lambda b, i, j: (b, i, 0)),
        pl.BlockSpec((None, mod_rows, 3 * D_MODEL), mod_map),
        pl.BlockSpec((1, D_MODEL), lambda b, i, j: (0, 0)),
        pl.BlockSpec((D_MODEL, slab), lambda b, i, j: (0, j)),
    ]
    args = [x, mod, norm_g.reshape(1, D_MODEL), w]
    if has_ab:
        in_specs += [pl.BlockSpec((D_MODEL, LANES), lambda b, i, j: (0, 0)),
                     pl.BlockSpec((16, D_MODEL), lambda b, i, j: (0, 0))]
        args += [w_ab, w_abt]
    out_specs, out_shapes = [], []
    for (lo, hi) in bounds:
        def seg_map(b, i, j, lo=lo, hi=hi):
            return (b, i, jnp.clip(j - lo, 0, hi - lo - 1))
        out_specs.append(pl.BlockSpec((None, tm, slab), seg_map))
        out_shapes.append(jax.ShapeDtypeStruct((n, t, slab * (hi - lo)), F32))
    if has_ab:
        out_specs += [pl.BlockSpec((None, tm, LANES), lambda b, i, j: (b, i, 0)),
                      pl.BlockSpec((None, 16, tm), lambda b, i, j: (b, 0, i))]
        out_shapes += [jax.ShapeDtypeStruct((n, t, LANES), F32),
                       jax.ShapeDtypeStruct((n, 16, t), F32)]
    return pl.pallas_call(
        functools.partial(_proj_kernel, tuple(bounds), has_ab),
        grid=(n, t // tm, n_slab),
        in_specs=in_specs,
        out_specs=out_specs,
        out_shape=out_shapes,
        scratch_shapes=[pltpu.VMEM((tm, D_MODEL), BF16)],
        compiler_params=_cparams(("arbitrary", "arbitrary", "arbitrary")),
        name="norm_proj",
    )(*args)


def _mod_norm(x, g, mod):
    ms = jnp.mean(x * x, axis=-1, keepdims=True)
    y = x * lax.rsqrt(ms + EPS) * g
    return y * (1.0 + mod[:, D_MODEL:2 * D_MODEL]) + mod[:, 0:D_MODEL]


def _out_kernel(final_norm, x_ref, o_ref, mod_ref, w_ref, fg_ref, out_ref):
    y = _dot(o_ref[...], w_ref[...])
    gate = mod_ref[:, 2 * D_MODEL:3 * D_MODEL]
    x = x_ref[...] + gate * y
    if final_norm:
        ms = jnp.mean(x * x, axis=-1, keepdims=True)
        x = x * lax.rsqrt(ms + EPS) * fg_ref[...]
    out_ref[...] = x


def _out_project(x, o, mod, w_out, final_g, tm, final_norm):
    n, t, _ = x.shape
    r = mod.shape[1]
    mod_rows = 1 if r == 1 else tm
    mod_map = (lambda b, i: (b, 0, 0)) if r == 1 else (lambda b, i: (b, i, 0))
    row_spec = pl.BlockSpec((None, tm, D_MODEL), lambda b, i: (b, i, 0))
    return pl.pallas_call(
        functools.partial(_out_kernel, final_norm),
        grid=(n, t // tm),
        in_specs=[
            row_spec,
            pl.BlockSpec((None, tm, WIDTH), lambda b, i: (b, i, 0)),
            pl.BlockSpec((None, mod_rows, 3 * D_MODEL), mod_map),
            pl.BlockSpec((WIDTH, D_MODEL), lambda b, i: (0, 0)),
            pl.BlockSpec((1, D_MODEL), lambda b, i: (0, 0)),
        ],
        out_specs=row_spec,
        out_shape=jax.ShapeDtypeStruct((n, t, D_MODEL), F32),
        compiler_params=_cparams(("arbitrary", "arbitrary")),
        name="out_proj",
    )(x, o, mod, w_out, final_g.reshape(1, D_MODEL))


def _unit_lower_inverses(n_mats, c):
    row = lax.broadcasted_iota(jnp.int32, (c, c), 0)
    col = lax.broadcasted_iota(jnp.int32, (c, c), 1)
    base = min(16, c)
    blk_id = lambda v, size: jnp.right_shift(v, size.bit_length() - 1)
    eye = (row == col).astype(F32)
    same = blk_id(row, base) == blk_id(col, base)
    ps = [jnp.where(same, -n, 0.0) for n in n_mats]
    invs = [eye + p for p in ps]
    ps = [p.astype(BF16) for p in ps]
    size = 2
    while size < base:
        ps = [_dot(p, p).astype(BF16) for p in ps]
        invs = [inv + _dot(p, inv) for p, inv in zip(ps, invs)]
        size *= 2
    size = base
    while size < c:
        same_next = blk_id(row, 2 * size) == blk_id(col, 2 * size)
        off = same_next & jnp.logical_not(same)
        ls = [jnp.where(off, n, 0.0).astype(BF16) for n in n_mats]
        invs_b = [inv.astype(BF16) for inv in invs]
        ts = [_dot(ib, l) for ib, l in zip(invs_b, ls)]
        invs = [inv - _dot(t, ib) for inv, t, ib in zip(invs, ts, invs_b)]
        same = same_next
        size *= 2
    return invs


def _aligned(v, m):
    return v if isinstance(v, int) else pl.multiple_of(v, m)


def _chunk_masks(c):
    row = lax.broadcasted_iota(jnp.int32, (c, c), 0)
    col = lax.broadcasted_iota(jnp.int32, (c, c), 1)
    tril_sel = jnp.where(row >= col, 1.0, 0.0).astype(BF16)
    triu_sel = jnp.where(row <= col, 1.0, 0.0).astype(BF16)
    return row >= col, row > col, tril_sel, triu_sel


def _gdn_chunk(c, r0, masks, requests, cw_ref, og_ref):
    tril, strict, tril_sel, triu_sel = masks
    pad = SUBLANES
    halo = CONV_W - 1
    n_req = len(requests)
    heads = [(i, h) for i in range(n_req) for h in range(N_HEADS)]
    gc_alls, gct_alls, beta_alls = [], [], []
    for _, _, ab_ref, g_ref, gt_ref, _ in requests:
        gc_alls.append(_dot_sel(tril_sel, g_ref[pl.ds(r0, c), :]))
        gct_alls.append(_dot_x_sel(gt_ref[0:N_HEADS, pl.ds(r0, c)], triu_sel))
        beta_alls.append(1.0 / (1.0 + jnp.exp(-ab_ref[pl.ds(r0, c), :])))

    def conv(off, i, h):
        cols = slice(off + h * D_HEAD, off + (h + 1) * D_HEAD)
        win = requests[i][0].at[pl.ds(r0, c + pad)]
        acc = None
        for tap in range(CONV_W):
            lo = pad - halo + tap
            term = win[lo:lo + c, cols] * cw_ref[tap:tap + 1, cols]
            acc = term if acc is None else acc + term
        return _silu(acc)

    qs, ks, vs = [], [], []
    for i, h in heads:
        q = conv(0, i, h)
        k = conv(WIDTH, i, h)
        qs.append(q * lax.rsqrt(jnp.sum(q * q, axis=-1, keepdims=True) + EPS)
                  * (D_HEAD ** -0.5))
        ks.append(k * lax.rsqrt(jnp.sum(k * k, axis=-1, keepdims=True) + EPS))
        vs.append(conv(2 * WIDTH, i, h))
    gcs = [jnp.broadcast_to(gc_alls[i][:, h:h + 1], (c, D_HEAD)) for i, h in heads]
    betas = [jnp.broadcast_to(beta_alls[i][:, N_HEADS + h:N_HEADS + h + 1], (c, D_HEAD))
             for i, h in heads]
    g_lasts = [gc[c - 1:c, :] for gc in gcs]
    decays = [jnp.where(tril, jnp.exp(jnp.where(tril, gc[:, 0:c] - gct_alls[i][h:h + 1, :], 0.0)),
                        0.0) for gc, (i, h) in zip(gcs, heads)]
    kbs = [k * beta for k, beta in zip(ks, betas)]
    ks_b = [k.astype(BF16) for k in ks]
    kks = [_dot_nt(kb, k) for kb, k in zip(kbs, ks_b)]
    qks = [_dot_nt(q, k) for q, k in zip(qs, ks_b)]
    n_mats = [jnp.where(strict, kk * decay, 0.0) for kk, decay in zip(kks, decays)]
    qks = [(qk * decay).astype(BF16) for qk, decay in zip(qks, decays)]
    invs = _unit_lower_inverses(n_mats, c)
    e_gcs = [jnp.exp(gc) for gc in gcs]
    rhss = [jnp.concatenate([v * beta, kb * e_gc], axis=1).astype(BF16)
            for v, beta, kb, e_gc in zip(vs, betas, kbs, e_gcs)]
    sols = [_dot(inv, rhs) for inv, rhs in zip(invs, rhss)]
    q_decs = [(q * e_gc).astype(BF16) for q, e_gc in zip(qs, e_gcs)]
    k_decs = [(k * jnp.exp(g_last - gc)).astype(BF16)
              for k, g_last, gc in zip(ks, g_lasts, gcs)]
    s_olds = [requests[i][5][h] for i, h in heads]
    s_bs = [s.astype(BF16) for s in s_olds]
    us = [sol[:, 0:D_HEAD] - _dot(sol[:, D_HEAD:2 * D_HEAD], s_b)
          for sol, s_b in zip(sols, s_bs)]
    us_b = [u.astype(BF16) for u in us]
    os_ = [_dot(q_dec, s_b) + _dot(qk, u_b)
           for q_dec, s_b, qk, u_b in zip(q_decs, s_bs, qks, us_b)]
    for n, (i, h) in enumerate(heads):
        requests[i][5][h] = s_olds[n] * jnp.exp(g_lasts[n]) + _dot_tn(k_decs[n], us_b[n])
    outs = [[] for _ in range(n_req)]
    for n, (i, h) in enumerate(heads):
        o = os_[n]
        o = o * lax.rsqrt(jnp.mean(o * o, axis=-1, keepdims=True) + EPS) * og_ref[...]
        z = requests[i][1][pl.ds(r0, c), h * D_HEAD:(h + 1) * D_HEAD]
        outs[i].append(o * _silu(z))
    return outs


def _decay_logits(ab, alog, dtb):
    return -jnp.exp(alog) * _softplus(ab + dtb)


def _gdn_decode_kernel(c, n_req, qkv_ref, z_ref, ab_ref, abt_ref, cw_ref, alog_ref, dtb_ref,
                       alogc_ref, dtbc_ref, og_ref, s0_ref, conv0_ref,
                       o_ref, s_ref, conv_ref, ext_ref, g_ref, gt_ref):
    pad = SUBLANES
    halo = CONV_W - 1
    s_ref[...] = s0_ref[...]
    ext_ref[:, pad - halo:pad, :] = conv0_ref[...]
    ext_ref[:, pad:pad + c, :] = qkv_ref[...]
    g_ref[...] = _decay_logits(ab_ref[...], alog_ref[...], dtb_ref[...])
    gt_ref[...] = _decay_logits(abt_ref[...], alogc_ref[...], dtbc_ref[...])
    requests = [(ext_ref.at[i], z_ref.at[i], ab_ref.at[i], g_ref.at[i], gt_ref.at[i], s_ref.at[i])
                for i in range(n_req)]
    outs = _gdn_chunk(c, 0, _chunk_masks(c), requests, cw_ref, og_ref)
    for i in range(n_req):
        for h, o in enumerate(outs[i]):
            o_ref[i, :, h * D_HEAD:(h + 1) * D_HEAD] = o
    conv_ref[...] = ext_ref[:, pad + c - halo:pad + c, :]


def _gdn_core_decode(qkv, z, ab, abt, conv_w, a_log, dt_bias, out_g, s0, conv0, n_req):
    b, c, _ = qkv.shape
    assert b % n_req == 0
    const = lambda shape: pl.BlockSpec(shape, lambda i: (0,) * len(shape))
    per_req = lambda *tail: pl.BlockSpec((n_req,) + tail, lambda i: (i,) + (0,) * len(tail))
    return pl.pallas_call(
        functools.partial(_gdn_decode_kernel, c, n_req),
        grid=(b // n_req,),
        in_specs=[
            per_req(c, CONV_DIM), per_req(c, WIDTH), per_req(c, LANES), per_req(16, c),
            const((CONV_W, CONV_DIM)), const((1, LANES)), const((1, LANES)),
            const((16, 1)), const((16, 1)), const((1, D_HEAD)),
            per_req(N_HEADS, D_HEAD, D_HEAD), per_req(CONV_W - 1, CONV_DIM),
        ],
        out_specs=[per_req(c, WIDTH), per_req(N_HEADS, D_HEAD, D_HEAD),
                   per_req(CONV_W - 1, CONV_DIM)],
        out_shape=[
            jax.ShapeDtypeStruct((b, c, WIDTH), F32),
            jax.ShapeDtypeStruct((b, N_HEADS, D_HEAD, D_HEAD), F32),
            jax.ShapeDtypeStruct((b, CONV_W - 1, CONV_DIM), F32),
        ],
        scratch_shapes=[
            pltpu.VMEM((n_req, c + SUBLANES, CONV_DIM), F32),
            pltpu.VMEM((n_req, c, LANES), F32),
            pltpu.VMEM((n_req, 16, c), F32),
        ],
        compiler_params=_cparams(("arbitrary",)),
        name="gdn_core",
    )(qkv, z, ab, abt, conv_w, *_head_params(a_log, dt_bias), out_g.reshape(1, D_HEAD),
      s0, conv0)


def _head_params(a_log, dt_bias):
    row = lambda v: jnp.pad(v.reshape(1, N_HEADS), ((0, 0), (0, LANES - N_HEADS)))
    col = lambda v: jnp.pad(v.reshape(N_HEADS, 1), ((0, 16 - N_HEADS), (0, 0)))
    return row(a_log), row(dt_bias), col(a_log), col(dt_bias)


def _gdn_layer_kernel(tm, c, x_ref, mod_ref, modn_ref, g_ref, gn_ref, w_ref, wab_ref, wabt_ref,
                      wout_ref, cw_ref, alog_ref, dtb_ref, alogc_ref, dtbc_ref, og_ref,
                      x1_ref, h1_ref, s_ref, conv_ref, ext_ref, z_s, ab_s, gl_s, gt_s):
    t = pl.program_id(1)
    pad = SUBLANES
    halo = CONV_W - 1
    n_chunks = tm // c
    n_col = 512

    @pl.when(t == 0)
    def _():
        s_ref[...] = jnp.zeros_like(s_ref)
        ext_ref[pad - halo:pad, :] = jnp.zeros((halo, CONV_DIM), F32)

    def project(r0):
        hb = _mod_norm(x_ref[pl.ds(r0, c), :], g_ref[...], mod_ref[...]).astype(BF16)
        for lo in range(0, CONV_DIM + WIDTH, n_col):
            res = jnp.dot(hb, w_ref[:, lo:lo + n_col], preferred_element_type=F32)
            if lo < CONV_DIM:
                ext_ref[pl.ds(_aligned(pad + r0, pad), c), lo:lo + n_col] = res
            else:
                z_s[pl.ds(r0, c), lo - CONV_DIM:lo - CONV_DIM + n_col] = res
        ab = jnp.dot(hb, wab_ref[...], preferred_element_type=F32)
        ab_s[pl.ds(r0, c), :] = ab
        gl_s[pl.ds(r0, c), :] = _decay_logits(ab, alog_ref[...], dtb_ref[...])
        abt = lax.dot_general(wabt_ref[...], hb, (((1,), (1,)), ((), ())),
                              preferred_element_type=F32)
        gt_s[:, pl.ds(r0, c)] = _decay_logits(abt, alogc_ref[...], dtbc_ref[...])

    project(0)
    masks = _chunk_masks(c)

    def chunk(ci, carry):
        r0 = pl.multiple_of(ci * c, c)
        outs, = _gdn_chunk(c, r0, masks, [(ext_ref, z_s, ab_s, gl_s, gt_s, s_ref)], cw_ref, og_ref)
        o = jnp.concatenate([o.astype(BF16) for o in outs], axis=1)
        y = jnp.dot(o, wout_ref[...], preferred_element_type=F32)
        x1 = x_ref[pl.ds(r0, c), :] + mod_ref[:, 2 * D_MODEL:3 * D_MODEL] * y
        x1_ref[pl.ds(r0, c), :] = x1
        h1_ref[pl.ds(r0, c), :] = _mod_norm(x1, gn_ref[...], modn_ref[...]).astype(BF16)
        project(pl.multiple_of(jnp.minimum(ci + 1, n_chunks - 1) * c, c))
        return carry

    lax.fori_loop(0, n_chunks, chunk, 0)

    tail = ext_ref[pad + tm - halo:pad + tm, :]
    ext_ref[pad - halo:pad, :] = tail
    conv_ref[...] = tail


def _gdn_layer_prompt(x, mod, mod_next, norm_g, norm_g_next, w_main, w_ab, w_abt, w_out, conv_w,
                      a_log, dt_bias, out_g, tm, c):
    b, t, _ = x.shape
    const = lambda shape: pl.BlockSpec(shape, lambda i, j: (0,) * len(shape))
    resident = lambda shape: pl.BlockSpec(shape, lambda i, j: (0,) * len(shape),
                                          pipeline_mode=pl.Buffered(1))
    mod_spec = pl.BlockSpec((None, 1, 3 * D_MODEL), lambda i, j: (i, 0, 0))
    row_spec = pl.BlockSpec((None, tm, D_MODEL), lambda i, j: (i, j, 0))
    return pl.pallas_call(
        functools.partial(_gdn_layer_kernel, tm, c),
        grid=(b, t // tm),
        in_specs=[
            row_spec, mod_spec, mod_spec, const((1, D_MODEL)), const((1, D_MODEL)),
            resident((D_MODEL, CONV_DIM + WIDTH)), const((D_MODEL, LANES)), const((16, D_MODEL)),
            resident((WIDTH, D_MODEL)), const((CONV_W, CONV_DIM)),
            const((1, LANES)), const((1, LANES)), const((16, 1)), const((16, 1)),
            const((1, D_HEAD)),
        ],
        out_specs=[
            row_spec, row_spec,
            pl.BlockSpec((None, N_HEADS, D_HEAD, D_HEAD), lambda i, j: (i, 0, 0, 0)),
            pl.BlockSpec((None, CONV_W - 1, CONV_DIM), lambda i, j: (i, 0, 0)),
        ],
        out_shape=[
            jax.ShapeDtypeStruct((b, t, D_MODEL), F32),
            jax.ShapeDtypeStruct((b, t, D_MODEL), BF16),
            jax.ShapeDtypeStruct((b, N_HEADS, D_HEAD, D_HEAD), F32),
            jax.ShapeDtypeStruct((b, CONV_W - 1, CONV_DIM), F32),
        ],
        scratch_shapes=[
            pltpu.VMEM((tm + SUBLANES, CONV_DIM), F32),
            pltpu.VMEM((tm, WIDTH), F32),
            pltpu.VMEM((tm, LANES), F32),
            pltpu.VMEM((tm, LANES), F32),
            pltpu.VMEM((16, tm), F32),
        ],
        compiler_params=_cparams(("arbitrary", "arbitrary")),
        name="gdn_layer",
    )(x, mod, mod_next, norm_g.reshape(1, D_MODEL), norm_g_next.reshape(1, D_MODEL),
      w_main, w_ab, w_abt, w_out, conv_w, *_head_params(a_log, dt_bias), out_g.reshape(1, D_HEAD))


def _banded_attention(dil, seq, q_ref, k_ref, v_ref, acc_ref, m_ref, l_ref):
    blk = D_HEAD
    per_step = 8
    n_blk = seq // dil // blk
    with_prev = n_blk > 1
    n_keys = 2 * blk if with_prev else blk
    row = lax.broadcasted_iota(jnp.int32, (blk, n_keys), 0)
    col = lax.broadcasted_iota(jnp.int32, (blk, n_keys), 1)
    is_prev = col < (n_keys - blk)
    slack = jnp.where(is_prev, col - row, row + (n_keys - blk) - col)
    cur_ok = slack >= 0
    ones = jnp.ones((n_keys, blk), BF16)

    def rows(start):
        if dil == 1:
            return pl.ds(start, blk)
        return pl.ds(start, blk, stride=dil)

    def body(i, carry):
        starts, masks, qs, kcs, vcs = [], [], [], [], []
        for u in range(per_step):
            idx = i * per_step + u
            r = idx // n_blk
            n = idx % n_blk
            start = r + dil * blk * n
            starts.append(start)
            qs.append(q_ref[rows(start), :].astype(BF16))
            k_cur = k_ref[rows(start), :].astype(BF16)
            v_cur = v_ref[rows(start), :].astype(BF16)
            if with_prev:
                start_p = r + dil * blk * jnp.maximum(n - 1, 0)
                first = jnp.where(n > 0, 0, 2 * blk)
                masks.append(slack - jnp.where(is_prev, first, 0) >= 0)
                k_cur = jnp.concatenate([k_ref[rows(start_p), :].astype(BF16), k_cur], axis=0)
                v_cur = jnp.concatenate([v_ref[rows(start_p), :].astype(BF16), v_cur], axis=0)
            else:
                masks.append(cur_ok)
            kcs.append(k_cur)
            vcs.append(jnp.concatenate([v_cur, ones], axis=1))
        ss = [jnp.where(mask, _dot_nt(q, kc), NEG_BIG) for q, kc, mask in zip(qs, kcs, masks)]
        ms = [jnp.max(s, axis=-1, keepdims=True) for s in ss]
        ps = [jnp.exp(s - m).astype(BF16) for s, m in zip(ss, ms)]
        accs = [jnp.dot(p, vc, preferred_element_type=F32) for p, vc in zip(ps, vcs)]
        for start, acc, m in zip(starts, accs, ms):
            acc_ref[rows(start), :] = acc[:, 0:blk]
            l_ref[rows(start), :] = acc[:, blk:2 * blk]
            m_ref[rows(start), :] = jnp.broadcast_to(m, (blk, blk))
        return carry

    lax.fori_loop(0, dil * n_blk // per_step, body, 0)


def _attn_layer_kernel(seq, h_ref, *refs):
    n_g = len(DIL_GROUPS)
    w_refs = refs[:3 * n_g + 1]
    (o_ref, kv0_ref, kv1_ref, kv2_ref, q_s, k_s, v_s, z_s, acc_s, m_s, l_s,
     sems) = refs[3 * n_g + 1:]
    b = pl.program_id(0)
    hp = pl.program_id(1)
    kv_refs = (kv0_ref, kv1_ref, kv2_ref)

    def proj(slab):
        return jnp.dot(h_ref[...], w_refs[slab][...], preferred_element_type=F32)

    def kv_copy(g, i, kv):
        keep = kv_refs[g].shape[1]
        src = (k_s, v_s)[kv].at[g, i, pl.ds(seq - keep, keep), :]
        col = pl.multiple_of(kv * WIDTH + (2 * hp + i) * D_HEAD, D_HEAD)
        dst = kv_refs[g].at[b, :, pl.ds(col, D_HEAD)]
        return pltpu.make_async_copy(src, dst, sems.at[(g * 2 + i) * 2 + kv])

    for g in range(n_g):
        k = proj(n_g + g)
        v = proj(2 * n_g + g)
        for i in range(2):
            k_s[g, i] = k[:, i * D_HEAD:(i + 1) * D_HEAD]
            v_s[g, i] = v[:, i * D_HEAD:(i + 1) * D_HEAD]
        for i in range(2):
            kv_copy(g, i, 0).start()
            kv_copy(g, i, 1).start()
        q = proj(g) * ATTN_SCALE
        for i in range(2):
            q_s[g, i] = q[:, i * D_HEAD:(i + 1) * D_HEAD]
    z = proj(3 * n_g)
    for i in range(2):
        z_s[i] = z[:, i * D_HEAD:(i + 1) * D_HEAD]

    for i in range(2):
        for g, (_, dil) in enumerate(DIL_GROUPS):
            _banded_attention(dil, seq, q_s.at[g, i], k_s.at[g, i], v_s.at[g, i],
                              acc_s.at[g], m_s.at[g], l_s.at[g])
        m_all = jnp.maximum(jnp.maximum(m_s[0], m_s[1]), m_s[2])
        num = jnp.zeros((seq, D_HEAD), F32)
        den = jnp.zeros((seq, D_HEAD), F32)
        for g in range(n_g):
            w = jnp.exp(m_s[g] - m_all)
            num = num + w * acc_s[g]
            den = den + w * l_s[g]
        o_ref[:, i * D_HEAD:(i + 1) * D_HEAD] = (num / den * _silu(z_s[i])).astype(BF16)

    for g in range(n_g):
        for i in range(2):
            kv_copy(g, i, 0).wait()
            kv_copy(g, i, 1).wait()


def _attn_layer_prompt(h, w):
    b, seq, _ = h.shape
    n_pair = N_HEADS // 2
    n_g = len(DIL_GROUPS)
    pair = 2 * D_HEAD
    keeps = [min(window, seq) for window, _ in DIL_GROUPS]
    slab = lambda n: pltpu.VMEM((n, seq, D_HEAD), F32)
    w_specs = [pl.BlockSpec((D_MODEL, pair), lambda i, p, s=s: (0, s * n_pair + p))
               for s in range(3 * n_g + 1)]
    return pl.pallas_call(
        functools.partial(_attn_layer_kernel, seq),
        grid=(b, n_pair),
        in_specs=[
            pl.BlockSpec((None, seq, D_MODEL), lambda i, p: (i, 0, 0),
                         pipeline_mode=pl.Buffered(1)),
        ] + w_specs,
        out_specs=[pl.BlockSpec((None, seq, 2 * D_HEAD), lambda i, p: (i, 0, p))]
        + [pl.BlockSpec(memory_space=pl.ANY)] * n_g,
        out_shape=[jax.ShapeDtypeStruct((b, seq, WIDTH), BF16)]
        + [jax.ShapeDtypeStruct((b, keep, 2 * WIDTH), F32) for keep in keeps],
        scratch_shapes=[
            pltpu.VMEM((n_g, 2, seq, D_HEAD), F32),
            pltpu.VMEM((n_g, 2, seq, D_HEAD), F32),
            pltpu.VMEM((n_g, 2, seq, D_HEAD), F32),
            slab(2),
            slab(n_g), slab(n_g), slab(n_g),
            pltpu.SemaphoreType.DMA((n_g * 2 * 2,)),
        ],
        compiler_params=_cparams(("arbitrary", "arbitrary")),
        name="attn_layer",
    )(h, *([w] * (3 * n_g + 1)))


def _attn_sample_kernel(n_new, *refs):
    q_ref, kv0_ref, kv1_ref, kv2_ref, z_ref, c0_ref, c1_ref, c2_ref, o_ref = refs
    n_t = KEYS_PER_QUERY
    ones = jnp.ones((D_HEAD, D_HEAD), BF16)
    dil1 = DIL_GROUPS[1][1]

    def key_tiles(g, l, kv):
        if g == 0:
            return jnp.concatenate([c0_ref[l:, kv], kv0_ref[0:l + 1, kv]], axis=0)
        if g == 1:
            a, r = divmod(l, dil1)
            parts = [c1_ref[a:, r, kv]]
            if a:
                parts.append(kv1_ref[r:r + 1, kv])
            parts.append(kv1_ref[l:l + 1, kv])
            return jnp.concatenate(parts, axis=0)
        return jnp.concatenate([c2_ref[:, l, kv], kv2_ref[l:l + 1, kv]], axis=0)

    for l in range(n_new):
        m_g, l_g, acc_g = [], [], []
        for g in range(len(DIL_GROUPS)):
            q = q_ref[l, g] * ATTN_SCALE
            prod = (key_tiles(g, l, 0) * q[None]).reshape(n_t * N_HEADS, D_HEAD)
            s = jnp.dot(prod.astype(BF16), ones, preferred_element_type=F32)
            s = s.reshape(n_t, N_HEADS, D_HEAD)
            m = jnp.max(s, axis=0)
            p = jnp.exp(s - m[None])
            m_g.append(m)
            l_g.append(jnp.sum(p, axis=0))
            acc_g.append(jnp.sum(p * key_tiles(g, l, 1), axis=0))
        m_all = jnp.maximum(jnp.maximum(m_g[0], m_g[1]), m_g[2])
        w_g = [jnp.exp(m - m_all) for m in m_g]
        num = w_g[0] * acc_g[0] + w_g[1] * acc_g[1] + w_g[2] * acc_g[2]
        den = w_g[0] * l_g[0] + w_g[1] * l_g[1] + w_g[2] * l_g[2]
        o_ref[l] = num / den * _silu(z_ref[l])


def _attn_sample(q, kvs, z, caches):
    b, n_new = q.shape[:2]
    assert n_new == SUBLANES
    (w0, d0), (w1, d1), (w2, d2) = DIL_GROUPS
    assert caches[0].shape[1] == w0 and caches[1].shape[1] == w1 and caches[2].shape[1] == w2
    assert d0 == 1 and n_new % d1 == 0 and d2 == 2 * n_new
    assert w0 // d0 + 1 == KEYS_PER_QUERY and w1 // d1 + 1 == KEYS_PER_QUERY
    assert w2 // d2 + 1 == KEYS_PER_QUERY
    c1 = caches[1].reshape(b, w1 // d1, d1, 2, N_HEADS, D_HEAD)
    c2 = caches[2].reshape(b, w2 // d2, d2, 2, N_HEADS, D_HEAD)
    tail = (2, N_HEADS, D_HEAD)
    new_spec = pl.BlockSpec((None, n_new) + tail, lambda i: (i, 0, 0, 0, 0))
    return pl.pallas_call(
        functools.partial(_attn_sample_kernel, n_new),
        grid=(b,),
        in_specs=[
            pl.BlockSpec((None, n_new, 3, N_HEADS, D_HEAD), lambda i: (i, 0, 0, 0, 0)),
            new_spec, new_spec, new_spec,
            pl.BlockSpec((None, n_new, N_HEADS, D_HEAD), lambda i: (i, 0, 0, 0)),
            pl.BlockSpec((None, w0) + tail, lambda i: (i, 0, 0, 0, 0)),
            pl.BlockSpec((None, w1 // d1, d1) + tail, lambda i: (i, 0, 0, 0, 0, 0)),
            pl.BlockSpec((None, w2 // d2, n_new) + tail, lambda i: (i, 0, 0, 0, 0, 0)),
        ],
        out_specs=pl.BlockSpec((None, n_new, N_HEADS, D_HEAD), lambda i: (i, 0, 0, 0)),
        out_shape=jax.ShapeDtypeStruct((b, n_new, N_HEADS, D_HEAD), F32),
        compiler_params=_cparams(("arbitrary",)),
        name="attn_sample",
    )(q, kvs[0], kvs[1], kvs[2], z, caches[0], c1, c2)


def kernel(x_prompt, x_sample, state_delta, state_conv, cache_kv_w128, cache_kv_w512, cache_kv_w2048,
           c_prompt, c_sample, norm_g, ada_w, ada_b, a_w_in, a_conv_w, a_A_log, a_dt_bias,
           a_out_norm_g, a_w_out, b_w_in, b_w_out, final_norm_g):
    bp, seq, _ = x_prompt.shape
    bs, n_new, _ = x_sample.shape
    n_s = bs * n_new

    mod = _ada_mod(jnp.concatenate([c_prompt, c_sample], axis=0), ada_w, ada_b)
    mod_p = [mod[l, :bp].reshape(bp, 1, 3 * D_MODEL) for l in range(2)]
    mod_s = [jnp.repeat(mod[l, bp:], n_new, axis=0).reshape(1, n_s, 3 * D_MODEL) for l in range(2)]

    w_in = a_w_in[0]
    w_main = w_in[:, :CONV_DIM + WIDTH].astype(BF16)
    w_ab = jnp.pad(w_in[:, CONV_DIM + WIDTH:].astype(BF16), ((0, 0), (0, LANES - 2 * N_HEADS)))
    w_abt = w_in[:, CONV_DIM + WIDTH:].T.astype(BF16)
    w_out_a = a_w_out[0].astype(BF16)
    xs_flat = x_sample.reshape(1, n_s, D_MODEL)

    x1_p, h1_p, delta_p, conv_p = _gdn_layer_prompt(
        x_prompt, mod_p[0], mod_p[1], norm_g[0], norm_g[1], w_main, w_ab, w_abt, w_out_a,
        a_conv_w[0], a_A_log[0], a_dt_bias[0], a_out_norm_g[0], 1024, 128)

    qkv_s, z_s, ab_s, abt_s = _project(xs_flat, mod_s[0], norm_g[0], w_main, (3, 1), n_s,
                                       w_ab, w_abt)
    abt_s = abt_s.reshape(16, bs, n_new).transpose(1, 0, 2)
    o_s, delta_s, conv_s = _gdn_core_decode(
        qkv_s.reshape(bs, n_new, CONV_DIM), z_s.reshape(bs, n_new, WIDTH),
        ab_s.reshape(bs, n_new, LANES), abt_s, a_conv_w[0], a_A_log[0], a_dt_bias[0],
        a_out_norm_g[0], state_delta[0], state_conv[0], 4)
    x1_s = _out_project(xs_flat, o_s.reshape(1, n_s, WIDTH), mod_s[0], w_out_a, final_norm_g,
                        n_s, False)

    n_g = len(DIL_GROUPS)
    w_b = b_w_in[0].astype(BF16)
    w_out_b = b_w_out[0].astype(BF16)

    ob_p, kv0_p, kv1_p, kv2_p = _attn_layer_prompt(h1_p, w_b)
    y_p = _out_project(x1_p, ob_p, mod_p[1], w_out_b, final_norm_g, 512, True)

    q_s, k_s, v_s, zb_s = _project(x1_s, mod_s[1], norm_g[1], w_b, (n_g, n_g, n_g, 1), n_s)
    k_s = k_s.reshape(bs, n_new, n_g, N_HEADS, D_HEAD)
    v_s = v_s.reshape(bs, n_new, n_g, N_HEADS, D_HEAD)
    kvn = [jnp.stack([k_s[:, :, g], v_s[:, :, g]], axis=2) for g in range(n_g)]
    ob_s = _attn_sample(q_s.reshape(bs, n_new, n_g, N_HEADS, D_HEAD), kvn,
                        zb_s.reshape(bs, n_new, N_HEADS, D_HEAD),
                        (cache_kv_w128[0], cache_kv_w512[0], cache_kv_w2048[0]))
    y_s = _out_project(x1_s, ob_s.reshape(1, n_s, WIDTH), mod_s[1], w_out_b, final_norm_g,
                       n_s, True)

    def kv_prompt(kv):
        return kv.reshape(1, bp, kv.shape[1], 2, N_HEADS, D_HEAD)

    return (y_p, y_s.reshape(bs, n_new, D_MODEL),
            delta_p[None], delta_s[None], conv_p[None], conv_s[None],
            kv_prompt(kv0_p), kvn[0][None], kv_prompt(kv1_p), kvn[1][None],
            kv_prompt(kv2_p), kvn[2][None])
```

```python
import functools

import jax
import jax.numpy as jnp
from jax import lax
from jax.experimental import pallas as pl
from jax.experimental.pallas import tpu as pltpu

F32 = jnp.float32
BF16 = jnp.bfloat16

D_MODEL = 1024
N_HEADS = 8
D_HEAD = 128
WIDTH = N_HEADS * D_HEAD
CONV_W = 4
CONV_DIM = 3 * WIDTH
DIL_GROUPS = ((128, 1), (512, 4), (2048, 16))
KEYS_PER_QUERY = 129
EPS = 1e-6
ATTN_SCALE = D_HEAD ** -0.5
NEG_BIG = -1e30

LANES = 128
SUBLANES = 8
VMEM_LIMIT_BYTES = 56 * 1024 * 1024


def _cparams(semantics):
    return pltpu.CompilerParams(dimension_semantics=semantics,
                                vmem_limit_bytes=VMEM_LIMIT_BYTES)


def _silu(x):
    return x * (1.0 / (1.0 + jnp.exp(-x)))


def _softplus(x):
    return jnp.maximum(x, 0.0) + jnp.log1p(jnp.exp(-jnp.abs(x)))


def _dot(a, b):
    return jnp.dot(a.astype(BF16), b.astype(BF16), preferred_element_type=F32)


def _dot_nt(a, b):
    return lax.dot_general(a.astype(BF16), b.astype(BF16), (((1,), (1,)), ((), ())),
                           preferred_element_type=F32)


def _dot_tn(a, b):
    return lax.dot_general(a.astype(BF16), b.astype(BF16), (((0,), (0,)), ((), ())),
                           preferred_element_type=F32)


def _split3(x):
    x1 = x.astype(BF16)
    r = x - x1.astype(F32)
    x2 = r.astype(BF16)
    x3 = (r - x2.astype(F32)).astype(BF16)
    return x1, x2, x3


def _dot_sel(sel, x):
    s = sel.astype(BF16)
    out = None
    for p in _split3(x):
        t = jnp.dot(s, p, preferred_element_type=F32)
        out = t if out is None else out + t
    return out


def _dot_x_sel(x, sel):
    s = sel.astype(BF16)
    out = None
    for p in _split3(x):
        t = jnp.dot(p, s, preferred_element_type=F32)
        out = t if out is None else out + t
    return out


def _mod_kernel(c_ref, w_ref, b_ref, o_ref):
    s = _silu(c_ref[...])
    o_ref[...] = _dot(s, w_ref[...]) + b_ref[...]


def _ada_mod(c_all, ada_w, ada_b):
    n_layers = ada_w.shape[0]
    rows = c_all.shape[0]
    tn = 768
    return pl.pallas_call(
        _mod_kernel,
        grid=(n_layers, 3 * D_MODEL // tn),
        in_specs=[
            pl.BlockSpec((rows, D_MODEL), lambda l, j: (0, 0)),
            pl.BlockSpec((None, D_MODEL, tn), lambda l, j: (l, 0, j)),
            pl.BlockSpec((None, 1, tn), lambda l, j: (l, 0, j)),
        ],
        out_specs=pl.BlockSpec((None, rows, tn), lambda l, j: (l, 0, j)),
        out_shape=jax.ShapeDtypeStruct((n_layers, rows, 3 * D_MODEL), F32),
        compiler_params=_cparams(("arbitrary", "arbitrary")),
        name="ada_mod",
    )(c_all, ada_w, ada_b.reshape(n_layers, 1, 3 * D_MODEL))


def _proj_kernel(seg_bounds, has_ab, *refs):
    x_ref, mod_ref, g_ref, w_ref = refs[:4]
    pos = 4
    if has_ab:
        wab_ref, wabt_ref = refs[4:6]
        pos = 6
    n_seg = len(seg_bounds)
    seg_refs = refs[pos:pos + n_seg]
    pos += n_seg
    if has_ab:
        ab_ref, abt_ref = refs[pos:pos + 2]
        pos += 2
    h_ref = refs[pos]
    j = pl.program_id(2)

    @pl.when(j == 0)
    def _():
        hb = _mod_norm(x_ref[...], g_ref[...], mod_ref[...]).astype(BF16)
        h_ref[...] = hb
        if has_ab:
            ab_ref[...] = jnp.dot(hb, wab_ref[...], preferred_element_type=F32)
            abt_ref[...] = lax.dot_general(wabt_ref[...], hb, (((1,), (1,)), ((), ())),
                                           preferred_element_type=F32)

    res = jnp.dot(h_ref[...], w_ref[...], preferred_element_type=F32)
    for (lo, hi), o_ref in zip(seg_bounds, seg_refs):
        @pl.when((j >= lo) & (j < hi))
        def _(o_ref=o_ref):
            o_ref[...] = res


def _project(x, mod, norm_g, w, seg_slabs, tm, w_ab=None, w_abt=None):
    n, t, _ = x.shape
    r = mod.shape[1]
    slab = 1024
    n_slab = w.shape[1] // slab
    assert sum(seg_slabs) == n_slab and t % tm == 0
    has_ab = w_ab is not None
    bounds, lo = [], 0
    for s in seg_slabs:
        bounds.append((lo, lo + s))
        lo += s
    mod_rows = 1 if r == 1 else tm
    mod_map = (lambda b, i, j: (b, 0, 0)) if r == 1 else (lambda b, i, j: (b, i, 0))
    in_specs = [
        pl.BlockSpec((None, tm, D_MODEL), lambda b, i, j: (b, i, 0)),
        pl.BlockSpec((None, mod_rows, 3 * D_MODEL), mod_map),
        pl.BlockSpec((1, D_MODEL), lambda b, i, j: (0, 0)),
        pl.BlockSpec((D_MODEL, slab), lambda b, i, j: (0, j)),
    ]
    args = [x, mod, norm_g.reshape(1, D_MODEL), w]
    if has_ab:
        in_specs += [pl.BlockSpec((D_MODEL, LANES), lambda b, i, j: (0, 0)),
                     pl.BlockSpec((16, D_MODEL), lambda b, i, j: (0, 0))]
        args += [w_ab, w_abt]
    out_specs, out_shapes = [], []
    for (lo, hi) in bounds:
        def seg_map(b, i, j, lo=lo, hi=hi):
            return (b, i, jnp.clip(j - lo, 0, hi - lo - 1))
        out_specs.append(pl.BlockSpec((None, tm, slab), seg_map))
        out_shapes.append(jax.ShapeDtypeStruct((n, t, slab * (hi - lo)), F32))
    if has_ab:
        out_specs += [pl.BlockSpec((None, tm, LANES), lambda b, i, j: (b, i, 0)),
                      pl.BlockSpec((None, 16, tm), lambda b, i, j: (b, 0, i))]
        out_shapes += [jax.ShapeDtypeStruct((n, t, LANES), F32),
                       jax.ShapeDtypeStruct((n, 16, t), F32)]
    return pl.pallas_call(
        functools.partial(_proj_kernel, tuple(bounds), has_ab),
        grid=(n, t // tm, n_slab),
        in_specs=in_specs,
        out_specs=out_specs,
        out_shape=out_shapes,
        scratch_shapes=[pltpu.VMEM((tm, D_MODEL), BF16)],
        compiler_params=_cparams(("arbitrary", "arbitrary", "arbitrary")),
        name="norm_proj",
    )(*args)


def _mod_norm(x, g, mod):
    ms = jnp.mean(x * x, axis=-1, keepdims=True)
    y = x * lax.rsqrt(ms + EPS) * g
    return y * (1.0 + mod[:, D_MODEL:2 * D_MODEL]) + mod[:, 0:D_MODEL]


def _out_kernel(final_norm, x_ref, o_ref, mod_ref, w_ref, fg_ref, out_ref):
    y = _dot(o_ref[...], w_ref[...])
    gate = mod_ref[:, 2 * D_MODEL:3 * D_MODEL]
    x = x_ref[...] + gate * y
    if final_norm:
        ms = jnp.mean(x * x, axis=-1, keepdims=True)
        x = x * lax.rsqrt(ms + EPS) * fg_ref[...]
    out_ref[...] = x


def _out_project(x, o, mod, w_out, final_g, tm, final_norm):
    n, t, _ = x.shape
    r = mod.shape[1]
    mod_rows = 1 if r == 1 else tm
    mod_map = (lambda b, i: (b, 0, 0)) if r == 1 else (lambda b, i: (b, i, 0))
    row_spec = pl.BlockSpec((None, tm, D_MODEL), lambda b, i: (b, i, 0))
    return pl.pallas_call(
        functools.partial(_out_kernel, final_norm),
        grid=(n, t // tm),
        in_specs=[
            row_spec,
            pl.BlockSpec((None, tm, WIDTH), lambda b, i: (b, i, 0)),
            pl.BlockSpec((None, mod_rows, 3 * D_MODEL), mod_map),
            pl.BlockSpec((WIDTH, D_MODEL), lambda b, i: (0, 0)),
            pl.BlockSpec((1, D_MODEL), lambda b, i: (0, 0)),
        ],
        out_specs=row_spec,
        out_shape=jax.ShapeDtypeStruct((n, t, D_MODEL), F32),
        compiler_params=_cparams(("arbitrary", "arbitrary")),
        name="out_proj",
    )(x, o, mod, w_out, final_g.reshape(1, D_MODEL))


def _unit_lower_inverses(n_mats, c):
    row = lax.broadcasted_iota(jnp.int32, (c, c), 0)
    col = lax.broadcasted_iota(jnp.int32, (c, c), 1)
    base = min(16, c)
    blk_id = lambda v, size: jnp.right_shift(v, size.bit_length() - 1)
    eye = (row == col).astype(F32)
    same = blk_id(row, base) == blk_id(col, base)
    ps = [jnp.where(same, -n, 0.0) for n in n_mats]
    invs = [eye + p for p in ps]
    ps = [p.astype(BF16) for p in ps]
    size = 2
    while size < base:
        ps = [_dot(p, p).astype(BF16) for p in ps]
        invs = [inv + _dot(p, inv) for p, inv in zip(ps, invs)]
        size *= 2
    size = base
    while size < c:
        same_next = blk_id(row, 2 * size) == blk_id(col, 2 * size)
        off = same_next & jnp.logical_not(same)
        ls = [jnp.where(off, n, 0.0).astype(BF16) for n in n_mats]
        invs_b = [inv.astype(BF16) for inv in invs]
        ts = [_dot(ib, l) for ib, l in zip(invs_b, ls)]
        invs = [inv - _dot(t, ib) for inv, t, ib in zip(invs, ts, invs_b)]
        same = same_next
        size *= 2
    return invs


def _aligned(v, m):
    return v if isinstance(v, int) else pl.multiple_of(v, m)


EXT_SLABS = CONV_DIM // D_HEAD
EXT_LEAD = SUBLANES


def _ext_rows(start, n):
    return pl.ds(2 * start, n, stride=2)


def _ext_shape(rows):
    return (EXT_SLABS, 2 * (rows + EXT_LEAD), D_HEAD)


def _chunk_masks(c):
    row = lax.broadcasted_iota(jnp.int32, (c, c), 0)
    col = lax.broadcasted_iota(jnp.int32, (c, c), 1)
    tril_sel = jnp.where(row >= col, 1.0, 0.0).astype(BF16)
    triu_sel = jnp.where(row <= col, 1.0, 0.0).astype(BF16)
    return row >= col, row > col, tril_sel, triu_sel


def _gdn_chunk(c, r0, masks, requests, cw_ref, og_ref):
    tril, strict, tril_sel, triu_sel = masks
    pad = EXT_LEAD
    halo = CONV_W - 1
    n_req = len(requests)
    heads = [(i, h) for i in range(n_req) for h in range(N_HEADS)]
    gc_alls, gct_alls, beta_alls = [], [], []
    for _, _, ab_ref, g_ref, gt_ref, _ in requests:
        gc_alls.append(_dot_sel(tril_sel, g_ref[pl.ds(r0, c), :]))
        gct_alls.append(_dot_x_sel(gt_ref[0:N_HEADS, pl.ds(r0, c)], triu_sel))
        beta_alls.append(1.0 / (1.0 + jnp.exp(-ab_ref[pl.ds(r0, c), :])))

    def conv(off, i, h):
        slab = off // D_HEAD + h
        cols = slice(slab * D_HEAD, (slab + 1) * D_HEAD)
        acc = None
        for tap in range(CONV_W):
            term = requests[i][0][slab, _ext_rows(r0 + pad - halo + tap, c), :] * cw_ref[tap:tap + 1, cols]
            acc = term if acc is None else acc + term
        return _silu(acc)

    qs, ks, vs = [], [], []
    for i, h in heads:
        q = conv(0, i, h)
        k = conv(WIDTH, i, h)
        qs.append(q * lax.rsqrt(jnp.sum(q * q, axis=-1, keepdims=True) + EPS)
                  * (D_HEAD ** -0.5))
        ks.append(k * lax.rsqrt(jnp.sum(k * k, axis=-1, keepdims=True) + EPS))
        vs.append(conv(2 * WIDTH, i, h))
    gcs = [jnp.broadcast_to(gc_alls[i][:, h:h + 1], (c, D_HEAD)) for i, h in heads]
    betas = [jnp.broadcast_to(beta_alls[i][:, N_HEADS + h:N_HEADS + h + 1], (c, D_HEAD))
             for i, h in heads]
    g_lasts = [gc[c - 1:c, :] for gc in gcs]
    decays = [jnp.where(tril, jnp.exp(jnp.where(tril, gc[:, 0:c] - gct_alls[i][h:h + 1, :], 0.0)),
                        0.0) for gc, (i, h) in zip(gcs, heads)]
    kbs = [k * beta for k, beta in zip(ks, betas)]
    ks_b = [k.astype(BF16) for k in ks]
    kks = [_dot_nt(kb, k) for kb, k in zip(kbs, ks_b)]
    qks = [_dot_nt(q, k) for q, k in zip(qs, ks_b)]
    n_mats = [jnp.where(strict, kk * decay, 0.0) for kk, decay in zip(kks, decays)]
    qks = [(qk * decay).astype(BF16) for qk, decay in zip(qks, decays)]
    invs = _unit_lower_inverses(n_mats, c)
    e_gcs = [jnp.exp(gc) for gc in gcs]
    rhss = [jnp.concatenate([v * beta, kb * e_gc], axis=1).astype(BF16)
            for v, beta, kb, e_gc in zip(vs, betas, kbs, e_gcs)]
    sols = [_dot(inv, rhs) for inv, rhs in zip(invs, rhss)]
    q_decs = [(q * e_gc).astype(BF16) for q, e_gc in zip(qs, e_gcs)]
    k_decs = [(k * jnp.exp(g_last - gc)).astype(BF16)
              for k, g_last, gc in zip(ks, g_lasts, gcs)]
    s_olds = [requests[i][5][h] for i, h in heads]
    s_bs = [s.astype(BF16) for s in s_olds]
    us = [sol[:, 0:D_HEAD] - _dot(sol[:, D_HEAD:2 * D_HEAD], s_b)
          for sol, s_b in zip(sols, s_bs)]
    us_b = [u.astype(BF16) for u in us]
    os_ = [_dot(q_dec, s_b) + _dot(qk, u_b)
           for q_dec, s_b, qk, u_b in zip(q_decs, s_bs, qks, us_b)]
    for n, (i, h) in enumerate(heads):
        requests[i][5][h] = s_olds[n] * jnp.exp(g_lasts[n]) + _dot_tn(k_decs[n], us_b[n])
    outs = [[] for _ in range(n_req)]
    for n, (i, h) in enumerate(heads):
        o = os_[n]
        o = o * lax.rsqrt(jnp.mean(o * o, axis=-1, keepdims=True) + EPS) * og_ref[...]
        z = requests[i][1][pl.ds(r0, c), h * D_HEAD:(h + 1) * D_HEAD]
        outs[i].append(o * _silu(z))
    return outs


def _decay_logits(ab, alog, dtb):
    return -jnp.exp(alog) * _softplus(ab + dtb)


def _gdn_decode_kernel(c, n_req, qkv_ref, z_ref, ab_ref, abt_ref, cw_ref, alog_ref, dtb_ref,
                       alogc_ref, dtbc_ref, og_ref, s0_ref, conv0_ref,
                       o_ref, s_ref, conv_ref, ext_ref, g_ref, gt_ref):
    halo = CONV_W - 1
    s_ref[...] = s0_ref[...]
    lead_fill = jnp.zeros((EXT_LEAD - halo, D_HEAD), F32)
    for i in range(n_req):
        for slab in range(EXT_SLABS):
            cols = slice(slab * D_HEAD, (slab + 1) * D_HEAD)
            ext_ref[i, slab, _ext_rows(0, EXT_LEAD), :] = jnp.concatenate(
                [lead_fill, conv0_ref[i, :, cols]], axis=0)
            ext_ref[i, slab, _ext_rows(EXT_LEAD, c), :] = qkv_ref[i, :, cols]
    g_ref[...] = _decay_logits(ab_ref[...], alog_ref[...], dtb_ref[...])
    gt_ref[...] = _decay_logits(abt_ref[...], alogc_ref[...], dtbc_ref[...])
    requests = [(ext_ref.at[i], z_ref.at[i], ab_ref.at[i], g_ref.at[i], gt_ref.at[i], s_ref.at[i])
                for i in range(n_req)]
    outs = _gdn_chunk(c, 0, _chunk_masks(c), requests, cw_ref, og_ref)
    for i in range(n_req):
        for h, o in enumerate(outs[i]):
            o_ref[i, :, h * D_HEAD:(h + 1) * D_HEAD] = o
        for slab in range(EXT_SLABS):
            last = ext_ref[i, slab, _ext_rows(c, EXT_LEAD), :]
            conv_ref[i, :, slab * D_HEAD:(slab + 1) * D_HEAD] = last[EXT_LEAD - halo:]


def _gdn_core_decode(qkv, z, ab, abt, conv_w, a_log, dt_bias, out_g, s0, conv0, n_req):
    b, c, _ = qkv.shape
    assert b % n_req == 0
    const = lambda shape: pl.BlockSpec(shape, lambda i: (0,) * len(shape))
    per_req = lambda *tail: pl.BlockSpec((n_req,) + tail, lambda i: (i,) + (0,) * len(tail))
    return pl.pallas_call(
        functools.partial(_gdn_decode_kernel, c, n_req),
        grid=(b // n_req,),
        in_specs=[
            per_req(c, CONV_DIM), per_req(c, WIDTH), per_req(c, LANES), per_req(16, c),
            const((CONV_W, CONV_DIM)), const((1, LANES)), const((1, LANES)),
            const((16, 1)), const((16, 1)), const((1, D_HEAD)),
            per_req(N_HEADS, D_HEAD, D_HEAD), per_req(CONV_W - 1, CONV_DIM),
        ],
        out_specs=[per_req(c, WIDTH), per_req(N_HEADS, D_HEAD, D_HEAD),
                   per_req(CONV_W - 1, CONV_DIM)],
        out_shape=[
            jax.ShapeDtypeStruct((b, c, WIDTH), F32),
            jax.ShapeDtypeStruct((b, N_HEADS, D_HEAD, D_HEAD), F32),
            jax.ShapeDtypeStruct((b, CONV_W - 1, CONV_DIM), F32),
        ],
        scratch_shapes=[
            pltpu.VMEM((n_req,) + _ext_shape(c), F32),
            pltpu.VMEM((n_req, c, LANES), F32),
            pltpu.VMEM((n_req, 16, c), F32),
        ],
        compiler_params=_cparams(("arbitrary",)),
        name="gdn_core",
    )(qkv, z, ab, abt, conv_w, *_head_params(a_log, dt_bias), out_g.reshape(1, D_HEAD),
      s0, conv0)


def _head_params(a_log, dt_bias):
    row = lambda v: jnp.pad(v.reshape(1, N_HEADS), ((0, 0), (0, LANES - N_HEADS)))
    col = lambda v: jnp.pad(v.reshape(N_HEADS, 1), ((0, 16 - N_HEADS), (0, 0)))
    return row(a_log), row(dt_bias), col(a_log), col(dt_bias)


def _gdn_layer_kernel(tm, c, x_ref, mod_ref, modn_ref, g_ref, gn_ref, w_ref, wab_ref, wabt_ref,
                      wout_ref, cw_ref, alog_ref, dtb_ref, alogc_ref, dtbc_ref, og_ref,
                      x1_ref, h1_ref, s_ref, conv_ref, ext_ref, z_s, ab_s, gl_s, gt_s):
    t = pl.program_id(1)
    pad = EXT_LEAD
    halo = CONV_W - 1
    n_chunks = tm // c
    n_col = 512

    @pl.when(t == 0)
    def _():
        s_ref[...] = jnp.zeros_like(s_ref)
        ext_ref[:, 0:2 * pad, :] = jnp.zeros((EXT_SLABS, 2 * pad, D_HEAD), F32)

    per_step = 1
    rows = per_step * c
    n_steps = n_chunks // per_step

    def project(r0):
        hb = _mod_norm(x_ref[pl.ds(r0, rows), :], g_ref[...], mod_ref[...]).astype(BF16)
        for lo in range(0, CONV_DIM + WIDTH, n_col):
            res = jnp.dot(hb, w_ref[:, lo:lo + n_col], preferred_element_type=F32)
            if lo < CONV_DIM:
                for j in range(n_col // D_HEAD):
                    ext_ref[lo // D_HEAD + j, _ext_rows(pad + r0, rows), :] = (
                        res[:, j * D_HEAD:(j + 1) * D_HEAD])
            else:
                z_s[pl.ds(r0, rows), lo - CONV_DIM:lo - CONV_DIM + n_col] = res
        ab = jnp.dot(hb, wab_ref[...], preferred_element_type=F32)
        ab_s[pl.ds(r0, rows), :] = ab
        gl_s[pl.ds(r0, rows), :] = _decay_logits(ab, alog_ref[...], dtb_ref[...])
        abt = lax.dot_general(wabt_ref[...], hb, (((1,), (1,)), ((), ())),
                              preferred_element_type=F32)
        gt_s[:, pl.ds(r0, rows)] = _decay_logits(abt, alogc_ref[...], dtbc_ref[...])

    project(0)
    masks = _chunk_masks(c)

    def step(si, carry):
        for half in range(per_step):
            r0 = pl.multiple_of((si * per_step + half) * c, c)
            outs, = _gdn_chunk(c, r0, masks, [(ext_ref, z_s, ab_s, gl_s, gt_s, s_ref)], cw_ref,
                               og_ref)
            o = jnp.concatenate([o.astype(BF16) for o in outs], axis=1)
            y = jnp.dot(o, wout_ref[...], preferred_element_type=F32)
            x1 = x_ref[pl.ds(r0, c), :] + mod_ref[:, 2 * D_MODEL:3 * D_MODEL] * y
            x1_ref[pl.ds(r0, c), :] = x1
            h1_ref[pl.ds(r0, c), :] = _mod_norm(x1, gn_ref[...], modn_ref[...]).astype(BF16)
        project(pl.multiple_of(jnp.minimum(si + 1, n_steps - 1) * rows, rows))
        return carry

    lax.fori_loop(0, n_steps, step, 0)

    for slab in range(EXT_SLABS):
        last = ext_ref[slab, _ext_rows(tm, pad), :]
        ext_ref[slab, _ext_rows(0, pad), :] = last
        conv_ref[:, slab * D_HEAD:(slab + 1) * D_HEAD] = last[pad - halo:]


def _gdn_layer_prompt(x, mod, mod_next, norm_g, norm_g_next, w_main, w_ab, w_abt, w_out, conv_w,
                      a_log, dt_bias, out_g, tm, c):
    b, t, _ = x.shape
    const = lambda shape: pl.BlockSpec(shape, lambda i, j: (0,) * len(shape))
    resident = lambda shape: pl.BlockSpec(shape, lambda i, j: (0,) * len(shape),
                                          pipeline_mode=pl.Buffered(1))
    mod_spec = pl.BlockSpec((None, 1, 3 * D_MODEL), lambda i, j: (i, 0, 0))
    row_spec = pl.BlockSpec((None, tm, D_MODEL), lambda i, j: (i, j, 0))
    return pl.pallas_call(
        functools.partial(_gdn_layer_kernel, tm, c),
        grid=(b, t // tm),
        in_specs=[
            row_spec, mod_spec, mod_spec, const((1, D_MODEL)), const((1, D_MODEL)),
            resident((D_MODEL, CONV_DIM + WIDTH)), const((D_MODEL, LANES)), const((16, D_MODEL)),
            resident((WIDTH, D_MODEL)), const((CONV_W, CONV_DIM)),
            const((1, LANES)), const((1, LANES)), const((16, 1)), const((16, 1)),
            const((1, D_HEAD)),
        ],
        out_specs=[
            row_spec, row_spec,
            pl.BlockSpec((None, N_HEADS, D_HEAD, D_HEAD), lambda i, j: (i, 0, 0, 0)),
            pl.BlockSpec((None, CONV_W - 1, CONV_DIM), lambda i, j: (i, 0, 0)),
        ],
        out_shape=[
            jax.ShapeDtypeStruct((b, t, D_MODEL), F32),
            jax.ShapeDtypeStruct((b, t, D_MODEL), BF16),
            jax.ShapeDtypeStruct((b, N_HEADS, D_HEAD, D_HEAD), F32),
            jax.ShapeDtypeStruct((b, CONV_W - 1, CONV_DIM), F32),
        ],
        scratch_shapes=[
            pltpu.VMEM(_ext_shape(tm), F32),
            pltpu.VMEM((tm, WIDTH), F32),
            pltpu.VMEM((tm, LANES), F32),
            pltpu.VMEM((tm, LANES), F32),
            pltpu.VMEM((16, tm), F32),
        ],
        compiler_params=_cparams(("arbitrary", "arbitrary")),
        name="gdn_layer",
    )(x, mod, mod_next, norm_g.reshape(1, D_MODEL), norm_g_next.reshape(1, D_MODEL),
      w_main, w_ab, w_abt, w_out, conv_w, *_head_params(a_log, dt_bias), out_g.reshape(1, D_HEAD))


def _banded_attention(dil, seq, q_ref, k_ref, v_ref, acc_ref, m_ref, l_ref):
    blk = D_HEAD
    per_step = 8
    n_blk = seq // dil // blk
    with_prev = n_blk > 1
    n_keys = 2 * blk if with_prev else blk
    row = lax.broadcasted_iota(jnp.int32, (blk, n_keys), 0)
    col = lax.broadcasted_iota(jnp.int32, (blk, n_keys), 1)
    is_prev = col < (n_keys - blk)
    slack = jnp.where(is_prev, col - row, row + (n_keys - blk) - col)
    cur_ok = slack >= 0
    ones = jnp.ones((n_keys, blk), BF16)

    def rows(start):
        if dil == 1:
            return pl.ds(start, blk)
        return pl.ds(start, blk, stride=dil)

    def body(i, carry):
        starts, masks, qs, kcs, vcs = [], [], [], [], []
        for u in range(per_step):
            idx = i * per_step + u
            r = idx // n_blk
            n = idx % n_blk
            start = r + dil * blk * n
            starts.append(start)
            qs.append(q_ref[rows(start), :].astype(BF16))
            k_cur = k_ref[rows(start), :].astype(BF16)
            v_cur = v_ref[rows(start), :].astype(BF16)
            if with_prev:
                start_p = r + dil * blk * jnp.maximum(n - 1, 0)
                first = jnp.where(n > 0, 0, 2 * blk)
                masks.append(slack - jnp.where(is_prev, first, 0) >= 0)
                k_cur = jnp.concatenate([k_ref[rows(start_p), :].astype(BF16), k_cur], axis=0)
                v_cur = jnp.concatenate([v_ref[rows(start_p), :].astype(BF16), v_cur], axis=0)
            else:
                masks.append(cur_ok)
            kcs.append(k_cur)
            vcs.append(jnp.concatenate([v_cur, ones], axis=1))
        ss = [jnp.where(mask, _dot_nt(q, kc), NEG_BIG) for q, kc, mask in zip(qs, kcs, masks)]
        ms = [jnp.max(s, axis=-1, keepdims=True) for s in ss]
        ps = [jnp.exp(s - m).astype(BF16) for s, m in zip(ss, ms)]
        accs = [jnp.dot(p, vc, preferred_element_type=F32) for p, vc in zip(ps, vcs)]
        for start, acc, m in zip(starts, accs, ms):
            acc_ref[rows(start), :] = acc[:, 0:blk]
            l_ref[rows(start), :] = acc[:, blk:2 * blk]
            m_ref[rows(start), :] = jnp.broadcast_to(m, (blk, blk))
        return carry

    lax.fori_loop(0, dil * n_blk // per_step, body, 0)


def _attn_layer_kernel(seq, h_ref, *refs):
    n_g = len(DIL_GROUPS)
    w_refs = refs[:3 * n_g + 1]
    (o_ref, kv0_ref, kv1_ref, kv2_ref, q_s, k_s, v_s, z_s, acc_s, m_s, l_s,
     sems) = refs[3 * n_g + 1:]
    b = pl.program_id(0)
    hp = pl.program_id(1)
    kv_refs = (kv0_ref, kv1_ref, kv2_ref)

    def proj(slab):
        return jnp.dot(h_ref[...], w_refs[slab][...], preferred_element_type=F32)

    def kv_copy(g, i, kv):
        keep = kv_refs[g].shape[1]
        src = (k_s, v_s)[kv].at[g, i, pl.ds(seq - keep, keep), :]
        col = pl.multiple_of(kv * WIDTH + (2 * hp + i) * D_HEAD, D_HEAD)
        dst = kv_refs[g].at[b, :, pl.ds(col, D_HEAD)]
        return pltpu.make_async_copy(src, dst, sems.at[(g * 2 + i) * 2 + kv])

    for g in range(n_g):
        k = proj(n_g + g)
        v = proj(2 * n_g + g)
        for i in range(2):
            k_s[g, i] = k[:, i * D_HEAD:(i + 1) * D_HEAD]
            v_s[g, i] = v[:, i * D_HEAD:(i + 1) * D_HEAD]
        for i in range(2):
            kv_copy(g, i, 0).start()
            kv_copy(g, i, 1).start()
        q = proj(g) * ATTN_SCALE
        for i in range(2):
            q_s[g, i] = q[:, i * D_HEAD:(i + 1) * D_HEAD]
    z = proj(3 * n_g)
    for i in range(2):
        z_s[i] = z[:, i * D_HEAD:(i + 1) * D_HEAD]

    for i in range(2):
        for g, (_, dil) in enumerate(DIL_GROUPS):
            _banded_attention(dil, seq, q_s.at[g, i], k_s.at[g, i], v_s.at[g, i],
                              acc_s.at[g], m_s.at[g], l_s.at[g])
        m_all = jnp.maximum(jnp.maximum(m_s[0], m_s[1]), m_s[2])
        num = jnp.zeros((seq, D_HEAD), F32)
        den = jnp.zeros((seq, D_HEAD), F32)
        for g in range(n_g):
            w = jnp.exp(m_s[g] - m_all)
            num = num + w * acc_s[g]
            den = den + w * l_s[g]
        o_ref[:, i * D_HEAD:(i + 1) * D_HEAD] = (num / den * _silu(z_s[i])).astype(BF16)

    for g in range(n_g):
        for i in range(2):
            kv_copy(g, i, 0).wait()
            kv_copy(g, i, 1).wait()


def _attn_layer_prompt(h, w):
    b, seq, _ = h.shape
    n_pair = N_HEADS // 2
    n_g = len(DIL_GROUPS)
    pair = 2 * D_HEAD
    keeps = [min(window, seq) for window, _ in DIL_GROUPS]
    slab = lambda n: pltpu.VMEM((n, seq, D_HEAD), F32)
    w_specs = [pl.BlockSpec((D_MODEL, pair), lambda i, p, s=s: (0, s * n_pair + p))
               for s in range(3 * n_g + 1)]
    return pl.pallas_call(
        functools.partial(_attn_layer_kernel, seq),
        grid=(b, n_pair),
        in_specs=[
            pl.BlockSpec((None, seq, D_MODEL), lambda i, p: (i, 0, 0),
                         pipeline_mode=pl.Buffered(1)),
        ] + w_specs,
        out_specs=[pl.BlockSpec((None, seq, 2 * D_HEAD), lambda i, p: (i, 0, p))]
        + [pl.BlockSpec(memory_space=pl.ANY)] * n_g,
        out_shape=[jax.ShapeDtypeStruct((b, seq, WIDTH), BF16)]
        + [jax.ShapeDtypeStruct((b, keep, 2 * WIDTH), F32) for keep in keeps],
        scratch_shapes=[
            pltpu.VMEM((n_g, 2, seq, D_HEAD), F32),
            pltpu.VMEM((n_g, 2, seq, D_HEAD), F32),
            pltpu.VMEM((n_g, 2, seq, D_HEAD), F32),
            slab(2),
            slab(n_g), slab(n_g), slab(n_g),
            pltpu.SemaphoreType.DMA((n_g * 2 * 2,)),
        ],
        compiler_params=_cparams(("arbitrary", "arbitrary")),
        name="attn_layer",
    )(h, *([w] * (3 * n_g + 1)))


def _attn_sample_kernel(n_new, *refs):
    q_ref, kv0_ref, kv1_ref, kv2_ref, z_ref, c0_ref, c1_ref, c2_ref, o_ref = refs
    n_t = KEYS_PER_QUERY
    ones = jnp.ones((D_HEAD, D_HEAD), BF16)
    dil1 = DIL_GROUPS[1][1]

    def key_tiles(g, l, kv):
        if g == 0:
            return jnp.concatenate([c0_ref[l:, kv], kv0_ref[0:l + 1, kv]], axis=0)
        if g == 1:
            a, r = divmod(l, dil1)
            parts = [c1_ref[a:, r, kv]]
            if a:
                parts.append(kv1_ref[r:r + 1, kv])
            parts.append(kv1_ref[l:l + 1, kv])
            return jnp.concatenate(parts, axis=0)
        return jnp.concatenate([c2_ref[:, l, kv], kv2_ref[l:l + 1, kv]], axis=0)

    for l in range(n_new):
        m_g, l_g, acc_g = [], [], []
        for g in range(len(DIL_GROUPS)):
            q = q_ref[l, g] * ATTN_SCALE
            prod = (key_tiles(g, l, 0) * q[None]).reshape(n_t * N_HEADS, D_HEAD)
            s = jnp.dot(prod.astype(BF16), ones, preferred_element_type=F32)
            s = s.reshape(n_t, N_HEADS, D_HEAD)
            m = jnp.max(s, axis=0)
            p = jnp.exp(s - m[None])
            m_g.append(m)
            l_g.append(jnp.sum(p, axis=0))
            acc_g.append(jnp.sum(p * key_tiles(g, l, 1), axis=0))
        m_all = jnp.maximum(jnp.maximum(m_g[0], m_g[1]), m_g[2])
        w_g = [jnp.exp(m - m_all) for m in m_g]
        num = w_g[0] * acc_g[0] + w_g[1] * acc_g[1] + w_g[2] * acc_g[2]
        den = w_g[0] * l_g[0] + w_g[1] * l_g[1] + w_g[2] * l_g[2]
        o_ref[l] = num / den * _silu(z_ref[l])


def _attn_sample(q, kvs, z, caches):
    b, n_new = q.shape[:2]
    assert n_new == SUBLANES
    (w0, d0), (w1, d1), (w2, d2) = DIL_GROUPS
    assert caches[0].shape[1] == w0 and caches[1].shape[1] == w1 and caches[2].shape[1] == w2
    assert d0 == 1 and n_new % d1 == 0 and d2 == 2 * n_new
    assert w0 // d0 + 1 == KEYS_PER_QUERY and w1 // d1 + 1 == KEYS_PER_QUERY
    assert w2 // d2 + 1 == KEYS_PER_QUERY
    c1 = caches[1].reshape(b, w1 // d1, d1, 2, N_HEADS, D_HEAD)
    c2 = caches[2].reshape(b, w2 // d2, d2, 2, N_HEADS, D_HEAD)
    tail = (2, N_HEADS, D_HEAD)
    new_spec = pl.BlockSpec((None, n_new) + tail, lambda i: (i, 0, 0, 0, 0))
    return pl.pallas_call(
        functools.partial(_attn_sample_kernel, n_new),
        grid=(b,),
        in_specs=[
            pl.BlockSpec((None, n_new, 3, N_HEADS, D_HEAD), lambda i: (i, 0, 0, 0, 0)),
            new_spec, new_spec, new_spec,
            pl.BlockSpec((None, n_new, N_HEADS, D_HEAD), lambda i: (i, 0, 0, 0)),
            pl.BlockSpec((None, w0) + tail, lambda i: (i, 0, 0, 0, 0)),
            pl.BlockSpec((None, w1 // d1, d1) + tail, lambda i: (i, 0, 0, 0, 0, 0)),
            pl.BlockSpec((None, w2 // d2, n_new) + tail, lambda i: (i, 0, 0, 0, 0, 0)),
        ],
        out_specs=pl.BlockSpec((None, n_new, N_HEADS, D_HEAD), lambda i: (i, 0, 0, 0)),
        out_shape=jax.ShapeDtypeStruct((b, n_new, N_HEADS, D_HEAD), F32),
        compiler_params=_cparams(("arbitrary",)),
        name="attn_sample",
    )(q, kvs[0], kvs[1], kvs[2], z, caches[0], c1, c2)


def kernel(x_prompt, x_sample, state_delta, state_conv, cache_kv_w128, cache_kv_w512, cache_kv_w2048,
           c_prompt, c_sample, norm_g, ada_w, ada_b, a_w_in, a_conv_w, a_A_log, a_dt_bias,
           a_out_norm_g, a_w_out, b_w_in, b_w_out, final_norm_g):
    bp, seq, _ = x_prompt.shape
    bs, n_new, _ = x_sample.shape
    n_s = bs * n_new

    mod = _ada_mod(jnp.concatenate([c_prompt, c_sample], axis=0), ada_w, ada_b)
    mod_p = [mod[l, :bp].reshape(bp, 1, 3 * D_MODEL) for l in range(2)]
    mod_s = [jnp.repeat(mod[l, bp:], n_new, axis=0).reshape(1, n_s, 3 * D_MODEL) for l in range(2)]

    w_in = a_w_in[0]
    w_main = w_in[:, :CONV_DIM + WIDTH].astype(BF16)
    w_ab = jnp.pad(w_in[:, CONV_DIM + WIDTH:].astype(BF16), ((0, 0), (0, LANES - 2 * N_HEADS)))
    w_abt = w_in[:, CONV_DIM + WIDTH:].T.astype(BF16)
    w_out_a = a_w_out[0].astype(BF16)
    xs_flat = x_sample.reshape(1, n_s, D_MODEL)

    x1_p, h1_p, delta_p, conv_p = _gdn_layer_prompt(
        x_prompt, mod_p[0], mod_p[1], norm_g[0], norm_g[1], w_main, w_ab, w_abt, w_out_a,
        a_conv_w[0], a_A_log[0], a_dt_bias[0], a_out_norm_g[0], 512, 128)

    qkv_s, z_s, ab_s, abt_s = _project(xs_flat, mod_s[0], norm_g[0], w_main, (3, 1), n_s,
                                       w_ab, w_abt)
    abt_s = abt_s.reshape(16, bs, n_new).transpose(1, 0, 2)
    o_s, delta_s, conv_s = _gdn_core_decode(
        qkv_s.reshape(bs, n_new, CONV_DIM), z_s.reshape(bs, n_new, WIDTH),
        ab_s.reshape(bs, n_new, LANES), abt_s, a_conv_w[0], a_A_log[0], a_dt_bias[0],
        a_out_norm_g[0], state_delta[0], state_conv[0], 4)
    x1_s = _out_project(xs_flat, o_s.reshape(1, n_s, WIDTH), mod_s[0], w_out_a, final_norm_g,
                        n_s, False)

    n_g = len(DIL_GROUPS)
    w_b = b_w_in[0].astype(BF16)
    w_out_b = b_w_out[0].astype(BF16)

    ob_p, kv0_p, kv1_p, kv2_p = _attn_layer_prompt(h1_p, w_b)
    y_p = _out_project(x1_p, ob_p, mod_p[1], w_out_b, final_norm_g, 512, True)

    q_s, k_s, v_s, zb_s = _project(x1_s, mod_s[1], norm_g[1], w_b, (n_g, n_g, n_g, 1), n_s)
    k_s = k_s.reshape(bs, n_new, n_g, N_HEADS, D_HEAD)
    v_s = v_s.reshape(bs, n_new, n_g, N_HEADS, D_HEAD)
    kvn = [jnp.stack([k_s[:, :, g], v_s[:, :, g]], axis=2) for g in range(n_g)]
    ob_s = _attn_sample(q_s.reshape(bs, n_new, n_g, N_HEADS, D_HEAD), kvn,
                        zb_s.reshape(bs, n_new, N_HEADS, D_HEAD),
                        (cache_kv_w128[0], cache_kv_w512[0], cache_kv_w2048[0]))
    y_s = _out_project(x1_s, ob_s.reshape(1, n_s, WIDTH), mod_s[1], w_out_b, final_norm_g,
                       n_s, True)

    def kv_prompt(kv):
        return kv.reshape(1, bp, kv.shape[1], 2, N_HEADS, D_HEAD)

    return (y_p, y_s.reshape(bs, n_new, D_MODEL),
            delta_p[None], delta_s[None], conv_p[None], conv_s[None],
            kv_prompt(kv0_p), kvn[0][None], kv_prompt(kv1_p), kvn[1][None],
            kv_prompt(kv2_p), kvn[2][None])
```

```python
import functools

import jax
import jax.numpy as jnp
from jax import lax
from jax.experimental import pallas as pl
from jax.experimental.pallas import tpu as pltpu

F32 = jnp.float32
BF16 = jnp.bfloat16

D_MODEL = 1024
N_HEADS = 8
D_HEAD = 128
WIDTH = N_HEADS * D_HEAD
CONV_W = 4
CONV_DIM = 3 * WIDTH
DIL_GROUPS = ((128, 1), (512, 4), (2048, 16))
KEYS_PER_QUERY = 129
EPS = 1e-6
ATTN_SCALE = D_HEAD ** -0.5
LOG2_E = 1.4426950408889634
NEG_BIG = -1e30

LANES = 128
SUBLANES = 8
VMEM_LIMIT_BYTES = 56 * 1024 * 1024


def _cparams(semantics):
    return pltpu.CompilerParams(dimension_semantics=semantics,
                                vmem_limit_bytes=VMEM_LIMIT_BYTES)


def _silu(x):
    return x * (1.0 / (1.0 + jnp.exp(-x)))


def _softplus(x):
    return jnp.maximum(x, 0.0) + jnp.log1p(jnp.exp(-jnp.abs(x)))


def _dot(a, b):
    return jnp.dot(a.astype(BF16), b.astype(BF16), preferred_element_type=F32)


def _dot_nt(a, b):
    return lax.dot_general(a.astype(BF16), b.astype(BF16), (((1,), (1,)), ((), ())),
                           preferred_element_type=F32)


def _dot_tn(a, b):
    return lax.dot_general(a.astype(BF16), b.astype(BF16), (((0,), (0,)), ((), ())),
                           preferred_element_type=F32)


def _split3(x):
    x1 = x.astype(BF16)
    r = x - x1.astype(F32)
    x2 = r.astype(BF16)
    x3 = (r - x2.astype(F32)).astype(BF16)
    return x1, x2, x3


def _dot_sel(sel, x):
    s = sel.astype(BF16)
    out = None
    for p in _split3(x):
        t = jnp.dot(s, p, preferred_element_type=F32)
        out = t if out is None else out + t
    return out


def _dot_x_sel(x, sel):
    s = sel.astype(BF16)
    out = None
    for p in _split3(x):
        t = jnp.dot(p, s, preferred_element_type=F32)
        out = t if out is None else out + t
    return out


def _mod_kernel(c_ref, w_ref, b_ref, o_ref):
    s = _silu(c_ref[...])
    o_ref[...] = _dot(s, w_ref[...]) + b_ref[...]


def _ada_mod(c_all, ada_w, ada_b):
    n_layers = ada_w.shape[0]
    rows = c_all.shape[0]
    tn = 768
    return pl.pallas_call(
        _mod_kernel,
        grid=(n_layers, 3 * D_MODEL // tn),
        in_specs=[
            pl.BlockSpec((rows, D_MODEL), lambda l, j: (0, 0)),
            pl.BlockSpec((None, D_MODEL, tn), lambda l, j: (l, 0, j)),
            pl.BlockSpec((None, 1, tn), lambda l, j: (l, 0, j)),
        ],
        out_specs=pl.BlockSpec((None, rows, tn), lambda l, j: (l, 0, j)),
        out_shape=jax.ShapeDtypeStruct((n_layers, rows, 3 * D_MODEL), F32),
        compiler_params=_cparams(("arbitrary", "arbitrary")),
        name="ada_mod",
    )(c_all, ada_w, ada_b.reshape(n_layers, 1, 3 * D_MODEL))


def _proj_kernel(seg_bounds, has_ab, *refs):
    x_ref, mod_ref, g_ref, w_ref = refs[:4]
    pos = 4
    if has_ab:
        wab_ref, wabt_ref = refs[4:6]
        pos = 6
    n_seg = len(seg_bounds)
    seg_refs = refs[pos:pos + n_seg]
    pos += n_seg
    if has_ab:
        ab_ref, abt_ref = refs[pos:pos + 2]
        pos += 2
    h_ref = refs[pos]
    j = pl.program_id(2)

    @pl.when(j == 0)
    def _():
        hb = _mod_norm(x_ref[...], g_ref[...], mod_ref[...]).astype(BF16)
        h_ref[...] = hb
        if has_ab:
            ab_ref[...] = jnp.dot(hb, wab_ref[...], preferred_element_type=F32)
            abt_ref[...] = lax.dot_general(wabt_ref[...], hb, (((1,), (1,)), ((), ())),
                                           preferred_element_type=F32)

    res = jnp.dot(h_ref[...], w_ref[...], preferred_element_type=F32)
    for (lo, hi), o_ref in zip(seg_bounds, seg_refs):
        @pl.when((j >= lo) & (j < hi))
        def _(o_ref=o_ref):
            o_ref[...] = res


def _project(x, mod, norm_g, w, seg_slabs, tm, w_ab=None, w_abt=None):
    n, t, _ = x.shape
    r = mod.shape[1]
    slab = 1024
    n_slab = w.shape[1] // slab
    assert sum(seg_slabs) == n_slab and t % tm == 0
    has_ab = w_ab is not None
    bounds, lo = [], 0
    for s in seg_slabs:
        bounds.append((lo, lo + s))
        lo += s
    mod_rows = 1 if r == 1 else tm
    mod_map = (lambda b, i, j: (b, 0, 0)) if r == 1 else (lambda b, i, j: (b, i, 0))
    in_specs = [
        pl.BlockSpec((None, tm, D_MODEL), lambda b, i, j: (b, i, 0)),
        pl.BlockSpec((None, mod_rows, 3 * D_MODEL), mod_map),
        pl.BlockSpec((1, D_MODEL), lambda b, i, j: (0, 0)),
        pl.BlockSpec((D_MODEL, slab), lambda b, i, j: (0, j)),
    ]
    args = [x, mod, norm_g.reshape(1, D_MODEL), w]
    if has_ab:
        in_specs += [pl.BlockSpec((D_MODEL, LANES), lambda b, i, j: (0, 0)),
                     pl.BlockSpec((16, D_MODEL), lambda b, i, j: (0, 0))]
        args += [w_ab, w_abt]
    out_specs, out_shapes = [], []
    for (lo, hi) in bounds:
        def seg_map(b, i, j, lo=lo, hi=hi):
            return (b, i, jnp.clip(j - lo, 0, hi - lo - 1))
        out_specs.append(pl.BlockSpec((None, tm, slab), seg_map))
        out_shapes.append(jax.ShapeDtypeStruct((n, t, slab * (hi - lo)), F32))
    if has_ab:
        out_specs += [pl.BlockSpec((None, tm, LANES), lambda b, i, j: (b, i, 0)),
                      pl.BlockSpec((None, 16, tm), lambda b, i, j: (b, 0, i))]
        out_shapes += [jax.ShapeDtypeStruct((n, t, LANES), F32),
                       jax.ShapeDtypeStruct((n, 16, t), F32)]
    return pl.pallas_call(
        functools.partial(_proj_kernel, tuple(bounds), has_ab),
        grid=(n, t // tm, n_slab),
        in_specs=in_specs,
        out_specs=out_specs,
        out_shape=out_shapes,
        scratch_shapes=[pltpu.VMEM((tm, D_MODEL), BF16)],
        compiler_params=_cparams(("arbitrary", "arbitrary", "arbitrary")),
        name="norm_proj",
    )(*args)


def _mod_norm(x, g, mod):
    ms = jnp.mean(x * x, axis=-1, keepdims=True)
    y = x * lax.rsqrt(ms + EPS) * g
    return y * (1.0 + mod[:, D_MODEL:2 * D_MODEL]) + mod[:, 0:D_MODEL]


def _out_kernel(final_norm, x_ref, o_ref, mod_ref, w_ref, fg_ref, out_ref):
    y = _dot(o_ref[...], w_ref[...])
    gate = mod_ref[:, 2 * D_MODEL:3 * D_MODEL]
    x = x_ref[...] + gate * y
    if final_norm:
        ms = jnp.mean(x * x, axis=-1, keepdims=True)
        x = x * lax.rsqrt(ms + EPS) * fg_ref[...]
    out_ref[...] = x


def _out_project(x, o, mod, w_out, final_g, tm, final_norm):
    n, t, _ = x.shape
    r = mod.shape[1]
    mod_rows = 1 if r == 1 else tm
    mod_map = (lambda b, i: (b, 0, 0)) if r == 1 else (lambda b, i: (b, i, 0))
    row_spec = pl.BlockSpec((None, tm, D_MODEL), lambda b, i: (b, i, 0))
    return pl.pallas_call(
        functools.partial(_out_kernel, final_norm),
        grid=(n, t // tm),
        in_specs=[
            row_spec,
            pl.BlockSpec((None, tm, WIDTH), lambda b, i: (b, i, 0)),
            pl.BlockSpec((None, mod_rows, 3 * D_MODEL), mod_map),
            pl.BlockSpec((WIDTH, D_MODEL), lambda b, i: (0, 0)),
            pl.BlockSpec((1, D_MODEL), lambda b, i: (0, 0)),
        ],
        out_specs=row_spec,
        out_shape=jax.ShapeDtypeStruct((n, t, D_MODEL), F32),
        compiler_params=_cparams(("arbitrary", "arbitrary")),
        name="out_proj",
    )(x, o, mod, w_out, final_g.reshape(1, D_MODEL))


def _unit_lower_inverses(n_mats, c):
    row = lax.broadcasted_iota(jnp.int32, (c, c), 0)
    col = lax.broadcasted_iota(jnp.int32, (c, c), 1)
    base = min(16, c)
    blk_id = lambda v, size: jnp.right_shift(v, size.bit_length() - 1)
    eye = (row == col).astype(F32)
    same = blk_id(row, base) == blk_id(col, base)
    ps = [jnp.where(same, -n, 0.0) for n in n_mats]
    invs = [eye + p for p in ps]
    ps = [p.astype(BF16) for p in ps]
    size = 2
    while size < base:
        ps = [_dot(p, p).astype(BF16) for p in ps]
        invs = [inv + _dot(p, inv) for p, inv in zip(ps, invs)]
        size *= 2
    size = base
    while size < c:
        same_next = blk_id(row, 2 * size) == blk_id(col, 2 * size)
        off = same_next & jnp.logical_not(same)
        ls = [jnp.where(off, n, 0.0).astype(BF16) for n in n_mats]
        invs_b = [inv.astype(BF16) for inv in invs]
        ts = [_dot(ib, l) for ib, l in zip(invs_b, ls)]
        invs = [inv - _dot(t, ib) for inv, t, ib in zip(invs, ts, invs_b)]
        same = same_next
        size *= 2
    return invs


def _aligned(v, m):
    return v if isinstance(v, int) else pl.multiple_of(v, m)


EXT_SLABS = CONV_DIM // D_HEAD
EXT_LEAD = SUBLANES


def _ext_idx(slab, start, n):
    return (slab // 2, pl.ds(2 * start + slab % 2, n, stride=2), slice(None))


def _ext_shape(rows):
    return (EXT_SLABS // 2, 2 * (rows + EXT_LEAD), D_HEAD)


def _chunk_masks(c):
    row = lax.broadcasted_iota(jnp.int32, (c, c), 0)
    col = lax.broadcasted_iota(jnp.int32, (c, c), 1)
    tril_sel = jnp.where(row >= col, 1.0, 0.0).astype(BF16)
    triu_sel = jnp.where(row <= col, 1.0, 0.0).astype(BF16)
    return row >= col, row > col, tril_sel, triu_sel


def _gdn_chunk(c, r0, masks, requests, cw_ref, og_ref):
    tril, strict, tril_sel, triu_sel = masks
    pad = EXT_LEAD
    halo = CONV_W - 1
    n_req = len(requests)
    heads = [(i, h) for i in range(n_req) for h in range(N_HEADS)]
    gc_alls, gct_alls, beta_alls = [], [], []
    for _, _, ab_ref, g_ref, gt_ref, _ in requests:
        gc_alls.append(_dot_sel(tril_sel, g_ref[pl.ds(r0, c), :]))
        gct_alls.append(_dot_x_sel(gt_ref[0:N_HEADS, pl.ds(r0, c)], triu_sel))
        beta_alls.append(1.0 / (1.0 + jnp.exp(-ab_ref[pl.ds(r0, c), :])))

    def conv(off, i, h):
        slab = off // D_HEAD + h
        cols = slice(slab * D_HEAD, (slab + 1) * D_HEAD)
        acc = None
        for tap in range(CONV_W):
            rows = _ext_idx(slab, r0 + pad - halo + tap, c)
            term = requests[i][0][rows] * cw_ref[tap:tap + 1, cols]
            acc = term if acc is None else acc + term
        return _silu(acc)

    qs, ks, vs = [], [], []
    for i, h in heads:
        q = conv(0, i, h)
        k = conv(WIDTH, i, h)
        qs.append(q * lax.rsqrt(jnp.sum(q * q, axis=-1, keepdims=True) + EPS)
                  * (D_HEAD ** -0.5))
        ks.append(k * lax.rsqrt(jnp.sum(k * k, axis=-1, keepdims=True) + EPS))
        vs.append(conv(2 * WIDTH, i, h))
    gcs = [jnp.broadcast_to(gc_alls[i][:, h:h + 1], (c, D_HEAD)) for i, h in heads]
    betas = [jnp.broadcast_to(beta_alls[i][:, N_HEADS + h:N_HEADS + h + 1], (c, D_HEAD))
             for i, h in heads]
    g_lasts = [gc[c - 1:c, :] for gc in gcs]
    decays = [jnp.where(tril, jnp.exp2(jnp.where(tril, gc[:, 0:c] - gct_alls[i][h:h + 1, :], 0.0)),
                        0.0) for gc, (i, h) in zip(gcs, heads)]
    kbs = [k * beta for k, beta in zip(ks, betas)]
    ks_b = [k.astype(BF16) for k in ks]
    kks = [_dot_nt(kb, k) for kb, k in zip(kbs, ks_b)]
    qks = [_dot_nt(q, k) for q, k in zip(qs, ks_b)]
    n_mats = [jnp.where(strict, kk * decay, 0.0) for kk, decay in zip(kks, decays)]
    qks = [(qk * decay).astype(BF16) for qk, decay in zip(qks, decays)]
    invs = _unit_lower_inverses(n_mats, c)
    e_gcs = [jnp.exp2(gc) for gc in gcs]
    rhss = [jnp.concatenate([v * beta, kb * e_gc], axis=1).astype(BF16)
            for v, beta, kb, e_gc in zip(vs, betas, kbs, e_gcs)]
    sols = [_dot(inv, rhs) for inv, rhs in zip(invs, rhss)]
    q_decs = [(q * e_gc).astype(BF16) for q, e_gc in zip(qs, e_gcs)]
    k_decs = [(k * jnp.exp2(g_last - gc)).astype(BF16)
              for k, g_last, gc in zip(ks, g_lasts, gcs)]
    s_olds = [requests[i][5][h] for i, h in heads]
    s_bs = [s.astype(BF16) for s in s_olds]
    us = [sol[:, 0:D_HEAD] - _dot(sol[:, D_HEAD:2 * D_HEAD], s_b)
          for sol, s_b in zip(sols, s_bs)]
    us_b = [u.astype(BF16) for u in us]
    os_ = [_dot(q_dec, s_b) + _dot(qk, u_b)
           for q_dec, s_b, qk, u_b in zip(q_decs, s_bs, qks, us_b)]
    for n, (i, h) in enumerate(heads):
        requests[i][5][h] = s_olds[n] * jnp.exp2(g_lasts[n]) + _dot_tn(k_decs[n], us_b[n])
    outs = [[] for _ in range(n_req)]
    for n, (i, h) in enumerate(heads):
        o = os_[n]
        o = o * lax.rsqrt(jnp.mean(o * o, axis=-1, keepdims=True) + EPS) * og_ref[...]
        z = requests[i][1][pl.ds(r0, c), h * D_HEAD:(h + 1) * D_HEAD]
        outs[i].append(o * _silu(z))
    return outs


def _decay_logits(ab, alog, dtb):
    return (-LOG2_E * jnp.exp(alog)) * _softplus(ab + dtb)


def _gdn_decode_kernel(c, n_req, qkv_ref, z_ref, ab_ref, abt_ref, cw_ref, alog_ref, dtb_ref,
                       alogc_ref, dtbc_ref, og_ref, s0_ref, conv0_ref,
                       o_ref, s_ref, conv_ref, ext_ref, g_ref, gt_ref):
    halo = CONV_W - 1
    s_ref[...] = s0_ref[...]
    lead_fill = jnp.zeros((EXT_LEAD - halo, D_HEAD), F32)
    for i in range(n_req):
        for slab in range(EXT_SLABS):
            cols = slice(slab * D_HEAD, (slab + 1) * D_HEAD)
            ext_ref[(i,) + _ext_idx(slab, 0, EXT_LEAD)] = jnp.concatenate(
                [lead_fill, conv0_ref[i, :, cols]], axis=0)
            ext_ref[(i,) + _ext_idx(slab, EXT_LEAD, c)] = qkv_ref[i, :, cols]
    g_ref[...] = _decay_logits(ab_ref[...], alog_ref[...], dtb_ref[...])
    gt_ref[...] = _decay_logits(abt_ref[...], alogc_ref[...], dtbc_ref[...])
    requests = [(ext_ref.at[i], z_ref.at[i], ab_ref.at[i], g_ref.at[i], gt_ref.at[i], s_ref.at[i])
                for i in range(n_req)]
    outs = _gdn_chunk(c, 0, _chunk_masks(c), requests, cw_ref, og_ref)
    for i in range(n_req):
        for h, o in enumerate(outs[i]):
            o_ref[i, :, h * D_HEAD:(h + 1) * D_HEAD] = o
        for slab in range(EXT_SLABS):
            last = ext_ref[(i,) + _ext_idx(slab, c, EXT_LEAD)]
            conv_ref[i, :, slab * D_HEAD:(slab + 1) * D_HEAD] = last[EXT_LEAD - halo:]


def _gdn_core_decode(qkv, z, ab, abt, conv_w, a_log, dt_bias, out_g, s0, conv0, n_req):
    b, c, _ = qkv.shape
    assert b % n_req == 0
    const = lambda shape: pl.BlockSpec(shape, lambda i: (0,) * len(shape))
    per_req = lambda *tail: pl.BlockSpec((n_req,) + tail, lambda i: (i,) + (0,) * len(tail))
    return pl.pallas_call(
        functools.partial(_gdn_decode_kernel, c, n_req),
        grid=(b // n_req,),
        in_specs=[
            per_req(c, CONV_DIM), per_req(c, WIDTH), per_req(c, LANES), per_req(16, c),
            const((CONV_W, CONV_DIM)), const((1, LANES)), const((1, LANES)),
            const((16, 1)), const((16, 1)), const((1, D_HEAD)),
            per_req(N_HEADS, D_HEAD, D_HEAD), per_req(CONV_W - 1, CONV_DIM),
        ],
        out_specs=[per_req(c, WIDTH), per_req(N_HEADS, D_HEAD, D_HEAD),
                   per_req(CONV_W - 1, CONV_DIM)],
        out_shape=[
            jax.ShapeDtypeStruct((b, c, WIDTH), F32),
            jax.ShapeDtypeStruct((b, N_HEADS, D_HEAD, D_HEAD), F32),
            jax.ShapeDtypeStruct((b, CONV_W - 1, CONV_DIM), F32),
        ],
        scratch_shapes=[
            pltpu.VMEM((n_req,) + _ext_shape(c), F32),
            pltpu.VMEM((n_req, c, LANES), F32),
            pltpu.VMEM((n_req, 16, c), F32),
        ],
        compiler_params=_cparams(("arbitrary",)),
        name="gdn_core",
    )(qkv, z, ab, abt, conv_w, *_head_params(a_log, dt_bias), out_g.reshape(1, D_HEAD),
      s0, conv0)


def _head_params(a_log, dt_bias):
    row = lambda v: jnp.pad(v.reshape(1, N_HEADS), ((0, 0), (0, LANES - N_HEADS)))
    col = lambda v: jnp.pad(v.reshape(N_HEADS, 1), ((0, 16 - N_HEADS), (0, 0)))
    return row(a_log), row(dt_bias), col(a_log), col(dt_bias)


def _gdn_layer_kernel(tm, c, x_ref, mod_ref, modn_ref, g_ref, gn_ref, w_ref, wab_ref, wabt_ref,
                      wout_ref, cw_ref, alog_ref, dtb_ref, alogc_ref, dtbc_ref, og_ref,
                      x1_ref, h1_ref, s_ref, conv_ref, ext_ref, z_s, ab_s, gl_s, gt_s):
    t = pl.program_id(1)
    pad = EXT_LEAD
    halo = CONV_W - 1
    n_chunks = tm // c
    n_col = 512

    @pl.when(t == 0)
    def _():
        s_ref[...] = jnp.zeros_like(s_ref)
        ext_ref[:, 0:2 * pad, :] = jnp.zeros((EXT_SLABS // 2, 2 * pad, D_HEAD), F32)

    per_step = 1
    rows = per_step * c
    n_steps = n_chunks // per_step

    def project(r0):
        hb = _mod_norm(x_ref[pl.ds(r0, rows), :], g_ref[...], mod_ref[...]).astype(BF16)
        for lo in range(0, CONV_DIM + WIDTH, n_col):
            res = jnp.dot(hb, w_ref[:, lo:lo + n_col], preferred_element_type=F32)
            if lo < CONV_DIM:
                for j in range(n_col // D_HEAD):
                    ext_ref[_ext_idx(lo // D_HEAD + j, pad + r0, rows)] = (
                        res[:, j * D_HEAD:(j + 1) * D_HEAD])
            else:
                z_s[pl.ds(r0, rows), lo - CONV_DIM:lo - CONV_DIM + n_col] = res
        ab = jnp.dot(hb, wab_ref[...], preferred_element_type=F32)
        ab_s[pl.ds(r0, rows), :] = ab
        gl_s[pl.ds(r0, rows), :] = _decay_logits(ab, alog_ref[...], dtb_ref[...])
        abt = lax.dot_general(wabt_ref[...], hb, (((1,), (1,)), ((), ())),
                              preferred_element_type=F32)
        gt_s[:, pl.ds(r0, rows)] = _decay_logits(abt, alogc_ref[...], dtbc_ref[...])

    project(0)
    masks = _chunk_masks(c)

    def step(si, carry):
        for half in range(per_step):
            r0 = pl.multiple_of((si * per_step + half) * c, c)
            outs, = _gdn_chunk(c, r0, masks, [(ext_ref, z_s, ab_s, gl_s, gt_s, s_ref)], cw_ref,
                               og_ref)
            o = jnp.concatenate([o.astype(BF16) for o in outs], axis=1)
            y = jnp.dot(o, wout_ref[...], preferred_element_type=F32)
            x1 = x_ref[pl.ds(r0, c), :] + mod_ref[:, 2 * D_MODEL:3 * D_MODEL] * y
            x1_ref[pl.ds(r0, c), :] = x1
            h1_ref[pl.ds(r0, c), :] = _mod_norm(x1, gn_ref[...], modn_ref[...]).astype(BF16)
        project(pl.multiple_of(jnp.minimum(si + 1, n_steps - 1) * rows, rows))
        return carry

    lax.fori_loop(0, n_steps, step, 0)

    for slab in range(EXT_SLABS):
        last = ext_ref[_ext_idx(slab, tm, pad)]
        ext_ref[_ext_idx(slab, 0, pad)] = last
        conv_ref[:, slab * D_HEAD:(slab + 1) * D_HEAD] = last[pad - halo:]


def _gdn_layer_prompt(x, mod, mod_next, norm_g, norm_g_next, w_main, w_ab, w_abt, w_out, conv_w,
                      a_log, dt_bias, out_g, tm, c):
    b, t, _ = x.shape
    const = lambda shape: pl.BlockSpec(shape, lambda i, j: (0,) * len(shape))
    resident = lambda shape: pl.BlockSpec(shape, lambda i, j: (0,) * len(shape),
                                          pipeline_mode=pl.Buffered(1))
    mod_spec = pl.BlockSpec((None, 1, 3 * D_MODEL), lambda i, j: (i, 0, 0))
    row_spec = pl.BlockSpec((None, tm, D_MODEL), lambda i, j: (i, j, 0))
    return pl.pallas_call(
        functools.partial(_gdn_layer_kernel, tm, c),
        grid=(b, t // tm),
        in_specs=[
            row_spec, mod_spec, mod_spec, const((1, D_MODEL)), const((1, D_MODEL)),
            resident((D_MODEL, CONV_DIM + WIDTH)), const((D_MODEL, LANES)), const((16, D_MODEL)),
            resident((WIDTH, D_MODEL)), const((CONV_W, CONV_DIM)),
            const((1, LANES)), const((1, LANES)), const((16, 1)), const((16, 1)),
            const((1, D_HEAD)),
        ],
        out_specs=[
            row_spec, row_spec,
            pl.BlockSpec((None, N_HEADS, D_HEAD, D_HEAD), lambda i, j: (i, 0, 0, 0)),
            pl.BlockSpec((None, CONV_W - 1, CONV_DIM), lambda i, j: (i, 0, 0)),
        ],
        out_shape=[
            jax.ShapeDtypeStruct((b, t, D_MODEL), F32),
            jax.ShapeDtypeStruct((b, t, D_MODEL), BF16),
            jax.ShapeDtypeStruct((b, N_HEADS, D_HEAD, D_HEAD), F32),
            jax.ShapeDtypeStruct((b, CONV_W - 1, CONV_DIM), F32),
        ],
        scratch_shapes=[
            pltpu.VMEM(_ext_shape(tm), F32),
            pltpu.VMEM((tm, WIDTH), F32),
            pltpu.VMEM((tm, LANES), F32),
            pltpu.VMEM((tm, LANES), F32),
            pltpu.VMEM((16, tm), F32),
        ],
        compiler_params=_cparams(("arbitrary", "arbitrary")),
        name="gdn_layer",
    )(x, mod, mod_next, norm_g.reshape(1, D_MODEL), norm_g_next.reshape(1, D_MODEL),
      w_main, w_ab, w_abt, w_out, conv_w, *_head_params(a_log, dt_bias), out_g.reshape(1, D_HEAD))


def _banded_attention(dil, seq, q_ref, k_ref, v_ref, acc_ref, m_ref, l_ref):
    blk = D_HEAD
    per_step = 8
    n_blk = seq // dil // blk
    with_prev = n_blk > 1
    n_keys = 2 * blk if with_prev else blk
    row = lax.broadcasted_iota(jnp.int32, (blk, n_keys), 0)
    col = lax.broadcasted_iota(jnp.int32, (blk, n_keys), 1)
    is_prev = col < (n_keys - blk)
    slack = jnp.where(is_prev, col - row, row + (n_keys - blk) - col)
    cur_ok = slack >= 0
    ones = jnp.ones((n_keys, blk), BF16)

    def rows(start):
        if dil == 1:
            return pl.ds(start, blk)
        return pl.ds(start, blk, stride=dil)

    def body(i, carry):
        starts, masks, qs, kcs, vcs = [], [], [], [], []
        for u in range(per_step):
            idx = i * per_step + u
            r = idx // n_blk
            n = idx % n_blk
            start = r + dil * blk * n
            starts.append(start)
            qs.append(q_ref[rows(start), :].astype(BF16))
            k_cur = k_ref[rows(start), :].astype(BF16)
            v_cur = v_ref[rows(start), :].astype(BF16)
            if with_prev:
                start_p = r + dil * blk * jnp.maximum(n - 1, 0)
                first = jnp.where(n > 0, 0, 2 * blk)
                masks.append(slack - jnp.where(is_prev, first, 0) >= 0)
                k_cur = jnp.concatenate([k_ref[rows(start_p), :].astype(BF16), k_cur], axis=0)
                v_cur = jnp.concatenate([v_ref[rows(start_p), :].astype(BF16), v_cur], axis=0)
            else:
                masks.append(cur_ok)
            kcs.append(k_cur)
            vcs.append(jnp.concatenate([v_cur, ones], axis=1))
        ss = [jnp.where(mask, _dot_nt(q, kc), NEG_BIG) for q, kc, mask in zip(qs, kcs, masks)]
        ms = [jnp.max(s, axis=-1, keepdims=True) for s in ss]
        ps = [jnp.exp2(s - m).astype(BF16) for s, m in zip(ss, ms)]
        accs = [jnp.dot(p, vc, preferred_element_type=F32) for p, vc in zip(ps, vcs)]
        for start, acc, m in zip(starts, accs, ms):
            acc_ref[rows(start), :] = acc[:, 0:blk]
            l_ref[rows(start), :] = acc[:, blk:2 * blk]
            m_ref[rows(start), :] = jnp.broadcast_to(m, (blk, blk))
        return carry

    lax.fori_loop(0, dil * n_blk // per_step, body, 0)


def _attn_layer_kernel(seq, h_ref, *refs):
    n_g = len(DIL_GROUPS)
    w_refs = refs[:3 * n_g + 1]
    (o_ref, kv0_ref, kv1_ref, kv2_ref, q_s, k_s, v_s, z_s, acc_s, m_s, l_s,
     sems) = refs[3 * n_g + 1:]
    b = pl.program_id(0)
    hp = pl.program_id(1)
    kv_refs = (kv0_ref, kv1_ref, kv2_ref)

    def proj(slab):
        return jnp.dot(h_ref[...], w_refs[slab][...], preferred_element_type=F32)

    def kv_copy(g, i, kv):
        keep = kv_refs[g].shape[1]
        src = (k_s, v_s)[kv].at[g, i, pl.ds(seq - keep, keep), :]
        col = pl.multiple_of(kv * WIDTH + (2 * hp + i) * D_HEAD, D_HEAD)
        dst = kv_refs[g].at[b, :, pl.ds(col, D_HEAD)]
        return pltpu.make_async_copy(src, dst, sems.at[(g * 2 + i) * 2 + kv])

    for g in range(n_g):
        k = proj(n_g + g)
        v = proj(2 * n_g + g)
        for i in range(2):
            k_s[g, i] = k[:, i * D_HEAD:(i + 1) * D_HEAD]
            v_s[g, i] = v[:, i * D_HEAD:(i + 1) * D_HEAD]
        for i in range(2):
            kv_copy(g, i, 0).start()
            kv_copy(g, i, 1).start()
        q = proj(g) * (ATTN_SCALE * LOG2_E)
        for i in range(2):
            q_s[g, i] = q[:, i * D_HEAD:(i + 1) * D_HEAD]

    for i in range(2):
        for g, (_, dil) in enumerate(DIL_GROUPS):
            _banded_attention(dil, seq, q_s.at[g, i], k_s.at[g, i], v_s.at[g, i],
                              acc_s.at[g], m_s.at[g], l_s.at[g])
        if i == 0:
            z = proj(3 * n_g)
            for j in range(2):
                z_s[j] = z[:, j * D_HEAD:(j + 1) * D_HEAD]
        m_all = jnp.maximum(jnp.maximum(m_s[0], m_s[1]), m_s[2])
        num = jnp.zeros((seq, D_HEAD), F32)
        den = jnp.zeros((seq, D_HEAD), F32)
        for g in range(n_g):
            w = jnp.exp2(m_s[g] - m_all)
            num = num + w * acc_s[g]
            den = den + w * l_s[g]
        o_ref[:, i * D_HEAD:(i + 1) * D_HEAD] = (num / den * _silu(z_s[i])).astype(BF16)

    for g in range(n_g):
        for i in range(2):
            kv_copy(g, i, 0).wait()
            kv_copy(g, i, 1).wait()


def _attn_layer_prompt(h, w):
    b, seq, _ = h.shape
    n_pair = N_HEADS // 2
    n_g = len(DIL_GROUPS)
    pair = 2 * D_HEAD
    keeps = [min(window, seq) for window, _ in DIL_GROUPS]
    slab = lambda n: pltpu.VMEM((n, seq, D_HEAD), F32)
    w_specs = [pl.BlockSpec((D_MODEL, pair), lambda i, p, s=s: (0, s * n_pair + p))
               for s in range(3 * n_g + 1)]
    return pl.pallas_call(
        functools.partial(_attn_layer_kernel, seq),
        grid=(b, n_pair),
        in_specs=[
            pl.BlockSpec((None, seq, D_MODEL), lambda i, p: (i, 0, 0),
                         pipeline_mode=pl.Buffered(1)),
        ] + w_specs,
        out_specs=[pl.BlockSpec((None, seq, 2 * D_HEAD), lambda i, p: (i, 0, p))]
        + [pl.BlockSpec(memory_space=pl.ANY)] * n_g,
        out_shape=[jax.ShapeDtypeStruct((b, seq, WIDTH), BF16)]
        + [jax.ShapeDtypeStruct((b, keep, 2 * WIDTH), F32) for keep in keeps],
        scratch_shapes=[
            pltpu.VMEM((n_g, 2, seq, D_HEAD), F32),
            pltpu.VMEM((n_g, 2, seq, D_HEAD), F32),
            pltpu.VMEM((n_g, 2, seq, D_HEAD), F32),
            slab(2),
            slab(n_g), slab(n_g), slab(n_g),
            pltpu.SemaphoreType.DMA((n_g * 2 * 2,)),
        ],
        compiler_params=_cparams(("arbitrary", "arbitrary")),
        name="attn_layer",
    )(h, *([w] * (3 * n_g + 1)))


def _attn_sample_kernel(n_new, *refs):
    q_ref, kv0_ref, kv1_ref, kv2_ref, z_ref, c0_ref, c1_ref, c2_ref, o_ref = refs
    n_t = KEYS_PER_QUERY
    ones = jnp.ones((D_HEAD, D_HEAD), BF16)
    dil1 = DIL_GROUPS[1][1]

    def key_tiles(g, l, kv):
        if g == 0:
            return jnp.concatenate([c0_ref[l:, kv], kv0_ref[0:l + 1, kv]], axis=0)
        if g == 1:
            a, r = divmod(l, dil1)
            parts = [c1_ref[a:, r, kv]]
            if a:
                parts.append(kv1_ref[r:r + 1, kv])
            parts.append(kv1_ref[l:l + 1, kv])
            return jnp.concatenate(parts, axis=0)
        return jnp.concatenate([c2_ref[:, l, kv], kv2_ref[l:l + 1, kv]], axis=0)

    for l in range(n_new):
        m_g, l_g, acc_g = [], [], []
        for g in range(len(DIL_GROUPS)):
            q = q_ref[l, g] * (ATTN_SCALE * LOG2_E)
            prod = (key_tiles(g, l, 0) * q[None]).reshape(n_t * N_HEADS, D_HEAD)
            s = jnp.dot(prod.astype(BF16), ones, preferred_element_type=F32)
            s = s.reshape(n_t, N_HEADS, D_HEAD)
            m = jnp.max(s, axis=0)
            p = jnp.exp2(s - m[None])
            m_g.append(m)
            l_g.append(jnp.sum(p, axis=0))
            acc_g.append(jnp.sum(p * key_tiles(g, l, 1), axis=0))
        m_all = jnp.maximum(jnp.maximum(m_g[0], m_g[1]), m_g[2])
        w_g = [jnp.exp2(m - m_all) for m in m_g]
        num = w_g[0] * acc_g[0] + w_g[1] * acc_g[1] + w_g[2] * acc_g[2]
        den = w_g[0] * l_g[0] + w_g[1] * l_g[1] + w_g[2] * l_g[2]
        o_ref[l] = num / den * _silu(z_ref[l])


def _attn_sample(q, kvs, z, caches):
    b, n_new = q.shape[:2]
    assert n_new == SUBLANES
    (w0, d0), (w1, d1), (w2, d2) = DIL_GROUPS
    assert caches[0].shape[1] == w0 and caches[1].shape[1] == w1 and caches[2].shape[1] == w2
    assert d0 == 1 and n_new % d1 == 0 and d2 == 2 * n_new
    assert w0 // d0 + 1 == KEYS_PER_QUERY and w1 // d1 + 1 == KEYS_PER_QUERY
    assert w2 // d2 + 1 == KEYS_PER_QUERY
    c1 = caches[1].reshape(b, w1 // d1, d1, 2, N_HEADS, D_HEAD)
    c2 = caches[2].reshape(b, w2 // d2, d2, 2, N_HEADS, D_HEAD)
    tail = (2, N_HEADS, D_HEAD)
    new_spec = pl.BlockSpec((None, n_new) + tail, lambda i: (i, 0, 0, 0, 0))
    return pl.pallas_call(
        functools.partial(_attn_sample_kernel, n_new),
        grid=(b,),
        in_specs=[
            pl.BlockSpec((None, n_new, 3, N_HEADS, D_HEAD), lambda i: (i, 0, 0, 0, 0)),
            new_spec, new_spec, new_spec,
            pl.BlockSpec((None, n_new, N_HEADS, D_HEAD), lambda i: (i, 0, 0, 0)),
            pl.BlockSpec((None, w0) + tail, lambda i: (i, 0, 0, 0, 0)),
            pl.BlockSpec((None, w1 // d1, d1) + tail, lambda i: (i, 0, 0, 0, 0, 0)),
            pl.BlockSpec((None, w2 // d2, n_new) + tail, lambda i: (i, 0, 0, 0, 0, 0)),
        ],
        out_specs=pl.BlockSpec((None, n_new, N_HEADS, D_HEAD), lambda i: (i, 0, 0, 0)),
        out_shape=jax.ShapeDtypeStruct((b, n_new, N_HEADS, D_HEAD), F32),
        compiler_params=_cparams(("arbitrary",)),
        name="attn_sample",
    )(q, kvs[0], kvs[1], kvs[2], z, caches[0], c1, c2)


def kernel(x_prompt, x_sample, state_delta, state_conv, cache_kv_w128, cache_kv_w512, cache_kv_w2048,
           c_prompt, c_sample, norm_g, ada_w, ada_b, a_w_in, a_conv_w, a_A_log, a_dt_bias,
           a_out_norm_g, a_w_out, b_w_in, b_w_out, final_norm_g):
    bp, seq, _ = x_prompt.shape
    bs, n_new, _ = x_sample.shape
    n_s = bs * n_new

    mod = _ada_mod(jnp.concatenate([c_prompt, c_sample], axis=0), ada_w, ada_b)
    mod_p = [mod[l, :bp].reshape(bp, 1, 3 * D_MODEL) for l in range(2)]
    mod_s = [jnp.repeat(mod[l, bp:], n_new, axis=0).reshape(1, n_s, 3 * D_MODEL) for l in range(2)]

    w_in = a_w_in[0]
    w_main = w_in[:, :CONV_DIM + WIDTH].astype(BF16)
    w_ab = jnp.pad(w_in[:, CONV_DIM + WIDTH:].astype(BF16), ((0, 0), (0, LANES - 2 * N_HEADS)))
    w_abt = w_in[:, CONV_DIM + WIDTH:].T.astype(BF16)
    w_out_a = a_w_out[0].astype(BF16)
    xs_flat = x_sample.reshape(1, n_s, D_MODEL)

    x1_p, h1_p, delta_p, conv_p = _gdn_layer_prompt(
        x_prompt, mod_p[0], mod_p[1], norm_g[0], norm_g[1], w_main, w_ab, w_abt, w_out_a,
        a_conv_w[0], a_A_log[0], a_dt_bias[0], a_out_norm_g[0], 1024, 128)

    qkv_s, z_s, ab_s, abt_s = _project(xs_flat, mod_s[0], norm_g[0], w_main, (3, 1), n_s,
                                       w_ab, w_abt)
    abt_s = abt_s.reshape(16, bs, n_new).transpose(1, 0, 2)
    o_s, delta_s, conv_s = _gdn_core_decode(
        qkv_s.reshape(bs, n_new, CONV_DIM), z_s.reshape(bs, n_new, WIDTH),
        ab_s.reshape(bs, n_new, LANES), abt_s, a_conv_w[0], a_A_log[0], a_dt_bias[0],
        a_out_norm_g[0], state_delta[0], state_conv[0], 4)
    x1_s = _out_project(xs_flat, o_s.reshape(1, n_s, WIDTH), mod_s[0], w_out_a, final_norm_g,
                        n_s, False)

    n_g = len(DIL_GROUPS)
    w_b = b_w_in[0].astype(BF16)
    w_out_b = b_w_out[0].astype(BF16)

    ob_p, kv0_p, kv1_p, kv2_p = _attn_layer_prompt(h1_p, w_b)
    y_p = _out_project(x1_p, ob_p, mod_p[1], w_out_b, final_norm_g, 512, True)

    q_s, k_s, v_s, zb_s = _project(x1_s, mod_s[1], norm_g[1], w_b, (n_g, n_g, n_g, 1), n_s)
    k_s = k_s.reshape(bs, n_new, n_g, N_HEADS, D_HEAD)
    v_s = v_s.reshape(bs, n_new, n_g, N_HEADS, D_HEAD)
    kvn = [jnp.stack([k_s[:, :, g], v_s[:, :, g]], axis=2) for g in range(n_g)]
    ob_s = _attn_sample(q_s.reshape(bs, n_new, n_g, N_HEADS, D_HEAD), kvn,
                        zb_s.reshape(bs, n_new, N_HEADS, D_HEAD),
                        (cache_kv_w128[0], cache_kv_w512[0], cache_kv_w2048[0]))
    y_s = _out_project(x1_s, ob_s.reshape(1, n_s, WIDTH), mod_s[1], w_out_b, final_norm_g,
                       n_s, True)

    def kv_prompt(kv):
        return kv.reshape(1, bp, kv.shape[1], 2, N_HEADS, D_HEAD)

    return (y_p, y_s.reshape(bs, n_new, D_MODEL),
            delta_p[None], delta_s[None], conv_p[None], conv_s[None],
            kv_prompt(kv0_p), kvn[0][None], kv_prompt(kv1_p), kvn[1][None],
            kv_prompt(kv2_p), kvn[2][None])
```

```python
import functools

import jax
import jax.numpy as jnp
from jax import lax
from jax.experimental import pallas as pl
from jax.experimental.pallas import tpu as pltpu

F32 = jnp.float32
BF16 = jnp.bfloat16

D_MODEL = 1024
N_HEADS = 8
D_HEAD = 128
WIDTH = N_HEADS * D_HEAD
CONV_W = 4
CONV_DIM = 3 * WIDTH
DIL_GROUPS = ((128, 1), (512, 4), (2048, 16))
KEYS_PER_QUERY = 129
EPS = 1e-6
ATTN_SCALE = D_HEAD ** -0.5
LOG2_E = 1.4426950408889634
NEG_BIG = -1e30

LANES = 128
SUBLANES = 8
VMEM_LIMIT_BYTES = 56 * 1024 * 1024


def _cparams(semantics):
    return pltpu.CompilerParams(dimension_semantics=semantics,
                                vmem_limit_bytes=VMEM_LIMIT_BYTES)


def _silu(x):
    return x * (1.0 / (1.0 + jnp.exp(-x)))


def _softplus(x):
    return jnp.maximum(x, 0.0) + jnp.log1p(jnp.exp(-jnp.abs(x)))


def _dot(a, b):
    return jnp.dot(a.astype(BF16), b.astype(BF16), preferred_element_type=F32)


def _dot_nt(a, b):
    return lax.dot_general(a.astype(BF16), b.astype(BF16), (((1,), (1,)), ((), ())),
                           preferred_element_type=F32)


def _dot_tn(a, b):
    return lax.dot_general(a.astype(BF16), b.astype(BF16), (((0,), (0,)), ((), ())),
                           preferred_element_type=F32)


def _split3(x):
    x1 = x.astype(BF16)
    r = x - x1.astype(F32)
    x2 = r.astype(BF16)
    x3 = (r - x2.astype(F32)).astype(BF16)
    return x1, x2, x3


def _dot_sel(sel, x):
    s = sel.astype(BF16)
    out = None
    for p in _split3(x):
        t = jnp.dot(s, p, preferred_element_type=F32)
        out = t if out is None else out + t
    return out


def _dot_x_sel(x, sel):
    s = sel.astype(BF16)
    out = None
    for p in _split3(x):
        t = jnp.dot(p, s, preferred_element_type=F32)
        out = t if out is None else out + t
    return out


def _mod_kernel(c_ref, w_ref, b_ref, o_ref):
    s = _silu(c_ref[...])
    o_ref[...] = _dot(s, w_ref[...]) + b_ref[...]


def _ada_mod(c_all, ada_w, ada_b):
    n_layers = ada_w.shape[0]
    rows = c_all.shape[0]
    tn = 768
    return pl.pallas_call(
        _mod_kernel,
        grid=(n_layers, 3 * D_MODEL // tn),
        in_specs=[
            pl.BlockSpec((rows, D_MODEL), lambda l, j: (0, 0)),
            pl.BlockSpec((None, D_MODEL, tn), lambda l, j: (l, 0, j)),
            pl.BlockSpec((None, 1, tn), lambda l, j: (l, 0, j)),
        ],
        out_specs=pl.BlockSpec((None, rows, tn), lambda l, j: (l, 0, j)),
        out_shape=jax.ShapeDtypeStruct((n_layers, rows, 3 * D_MODEL), F32),
        compiler_params=_cparams(("arbitrary", "arbitrary")),
        name="ada_mod",
    )(c_all, ada_w, ada_b.reshape(n_layers, 1, 3 * D_MODEL))


def _proj_kernel(seg_bounds, has_ab, *refs):
    x_ref, mod_ref, g_ref, w_ref = refs[:4]
    pos = 4
    if has_ab:
        wab_ref, wabt_ref = refs[4:6]
        pos = 6
    n_seg = len(seg_bounds)
    seg_refs = refs[pos:pos + n_seg]
    pos += n_seg
    if has_ab:
        ab_ref, abt_ref = refs[pos:pos + 2]
        pos += 2
    h_ref = refs[pos]
    j = pl.program_id(2)

    @pl.when(j == 0)
    def _():
        hb = _mod_norm(x_ref[...], g_ref[...], mod_ref[...]).astype(BF16)
        h_ref[...] = hb
        if has_ab:
            ab_ref[...] = jnp.dot(hb, wab_ref[...], preferred_element_type=F32)
            abt_ref[...] = lax.dot_general(wabt_ref[...], hb, (((1,), (1,)), ((), ())),
                                           preferred_element_type=F32)

    res = jnp.dot(h_ref[...], w_ref[...], preferred_element_type=F32)
    for (lo, hi), o_ref in zip(seg_bounds, seg_refs):
        @pl.when((j >= lo) & (j < hi))
        def _(o_ref=o_ref):
            o_ref[...] = res


def _project(x, mod, norm_g, w, seg_slabs, tm, w_ab=None, w_abt=None):
    n, t, _ = x.shape
    r = mod.shape[1]
    slab = 1024
    n_slab = w.shape[1] // slab
    assert sum(seg_slabs) == n_slab and t % tm == 0
    has_ab = w_ab is not None
    bounds, lo = [], 0
    for s in seg_slabs:
        bounds.append((lo, lo + s))
        lo += s
    mod_rows = 1 if r == 1 else tm
    mod_map = (lambda b, i, j: (b, 0, 0)) if r == 1 else (lambda b, i, j: (b, i, 0))
    in_specs = [
        pl.BlockSpec((None, tm, D_MODEL), lambda b, i, j: (b, i, 0)),
        pl.BlockSpec((None, mod_rows, 3 * D_MODEL), mod_map),
        pl.BlockSpec((1, D_MODEL), lambda b, i, j: (0, 0)),
        pl.BlockSpec((D_MODEL, slab), lambda b, i, j: (0, j)),
    ]
    args = [x, mod, norm_g.reshape(1, D_MODEL), w]
    if has_ab:
        in_specs += [pl.BlockSpec((D_MODEL, LANES), lambda b, i, j: (0, 0)),
                     pl.BlockSpec((16, D_MODEL), lambda b, i, j: (0, 0))]
        args += [w_ab, w_abt]
    out_specs, out_shapes = [], []
    for (lo, hi) in bounds:
        def seg_map(b, i, j, lo=lo, hi=hi):
            return (b, i, jnp.clip(j - lo, 0, hi - lo - 1))
        out_specs.append(pl.BlockSpec((None, tm, slab), seg_map))
        out_shapes.append(jax.ShapeDtypeStruct((n, t, slab * (hi - lo)), F32))
    if has_ab:
        out_specs += [pl.BlockSpec((None, tm, LANES), lambda b, i, j: (b, i, 0)),
                      pl.BlockSpec((None, 16, tm), lambda b, i, j: (b, 0, i))]
        out_shapes += [jax.ShapeDtypeStruct((n, t, LANES), F32),
                       jax.ShapeDtypeStruct((n, 16, t), F32)]
    return pl.pallas_call(
        functools.partial(_proj_kernel, tuple(bounds), has_ab),
        grid=(n, t // tm, n_slab),
        in_specs=in_specs,
        out_specs=out_specs,
        out_shape=out_shapes,
        scratch_shapes=[pltpu.VMEM((tm, D_MODEL), BF16)],
        compiler_params=_cparams(("arbitrary", "arbitrary", "arbitrary")),
        name="norm_proj",
    )(*args)


def _mod_norm(x, g, mod):
    ms = jnp.mean(x * x, axis=-1, keepdims=True)
    y = x * lax.rsqrt(ms + EPS) * g
    return y * (1.0 + mod[:, D_MODEL:2 * D_MODEL]) + mod[:, 0:D_MODEL]


def _out_kernel(final_norm, x_ref, o_ref, mod_ref, w_ref, fg_ref, out_ref):
    y = _dot(o_ref[...], w_ref[...])
    gate = mod_ref[:, 2 * D_MODEL:3 * D_MODEL]
    x = x_ref[...] + gate * y
    if final_norm:
        ms = jnp.mean(x * x, axis=-1, keepdims=True)
        x = x * lax.rsqrt(ms + EPS) * fg_ref[...]
    out_ref[...] = x


def _out_project(x, o, mod, w_out, final_g, tm, final_norm):
    n, t, _ = x.shape
    r = mod.shape[1]
    mod_rows = 1 if r == 1 else tm
    mod_map = (lambda b, i: (b, 0, 0)) if r == 1 else (lambda b, i: (b, i, 0))
    row_spec = pl.BlockSpec((None, tm, D_MODEL), lambda b, i: (b, i, 0))
    return pl.pallas_call(
        functools.partial(_out_kernel, final_norm),
        grid=(n, t // tm),
        in_specs=[
            row_spec,
            pl.BlockSpec((None, tm, WIDTH), lambda b, i: (b, i, 0)),
            pl.BlockSpec((None, mod_rows, 3 * D_MODEL), mod_map),
            pl.BlockSpec((WIDTH, D_MODEL), lambda b, i: (0, 0)),
            pl.BlockSpec((1, D_MODEL), lambda b, i: (0, 0)),
        ],
        out_specs=row_spec,
        out_shape=jax.ShapeDtypeStruct((n, t, D_MODEL), F32),
        compiler_params=_cparams(("arbitrary", "arbitrary")),
        name="out_proj",
    )(x, o, mod, w_out, final_g.reshape(1, D_MODEL))


def _unit_lower_inverses(n_mats, c):
    row = lax.broadcasted_iota(jnp.int32, (c, c), 0)
    col = lax.broadcasted_iota(jnp.int32, (c, c), 1)
    base = min(16, c)
    blk_id = lambda v, size: jnp.right_shift(v, size.bit_length() - 1)
    eye = (row == col).astype(F32)
    same = blk_id(row, base) == blk_id(col, base)
    ps = [jnp.where(same, -n, 0.0) for n in n_mats]
    invs = [eye + p for p in ps]
    assert base >= 4
    ps = [_dot(p, p).astype(BF16) for p in ps]
    power = 2
    while 2 * power < base:
        if c % LANES == 0:
            wide = [_dot(p, jnp.concatenate([p, inv.astype(BF16)], axis=1))
                    for p, inv in zip(ps, invs)]
            invs = [inv + w[:, c:] for inv, w in zip(invs, wide)]
            ps = [w[:, 0:c].astype(BF16) for w in wide]
        else:
            invs = [inv + _dot(p, inv) for p, inv in zip(ps, invs)]
            ps = [_dot(p, p).astype(BF16) for p in ps]
        power *= 2
    invs = [inv + _dot(p, inv) for p, inv in zip(ps, invs)]
    size = base
    while size < c:
        same_next = blk_id(row, 2 * size) == blk_id(col, 2 * size)
        off = same_next & jnp.logical_not(same)
        ls = [jnp.where(off, n, 0.0).astype(BF16) for n in n_mats]
        invs_b = [inv.astype(BF16) for inv in invs]
        ts = [_dot(ib, l) for ib, l in zip(invs_b, ls)]
        invs = [inv - _dot(t, ib) for inv, t, ib in zip(invs, ts, invs_b)]
        same = same_next
        size *= 2
    return invs


def _aligned(v, m):
    return v if isinstance(v, int) else pl.multiple_of(v, m)


EXT_SLABS = CONV_DIM // D_HEAD
EXT_LEAD = SUBLANES


def _ext_idx(slab, start, n):
    return (slab // 2, pl.ds(2 * start + slab % 2, n, stride=2), slice(None))


def _ext_shape(rows):
    return (EXT_SLABS // 2, 2 * (rows + EXT_LEAD), D_HEAD)


def _chunk_masks(c):
    row = lax.broadcasted_iota(jnp.int32, (c, c), 0)
    col = lax.broadcasted_iota(jnp.int32, (c, c), 1)
    tril_sel = jnp.where(row >= col, 1.0, 0.0).astype(BF16)
    triu_sel = jnp.where(row <= col, 1.0, 0.0).astype(BF16)
    return row >= col, row > col, tril_sel, triu_sel


def _problems(n_req):
    return [(i, h) for i in range(n_req) for h in range(N_HEADS)]


def _gdn_chunk_prepare(c, r0, masks, requests, cw_ref):
    tril, strict, tril_sel, triu_sel = masks
    pad = EXT_LEAD
    halo = CONV_W - 1
    heads = _problems(len(requests))
    gc_alls, gct_alls, beta_alls = [], [], []
    for _, _, ab_ref, g_ref, gt_ref, _ in requests:
        gc_alls.append(_dot_sel(tril_sel, g_ref[pl.ds(r0, c), :]))
        gct_alls.append(_dot_x_sel(gt_ref[0:N_HEADS, pl.ds(r0, c)], triu_sel))
        beta_alls.append(1.0 / (1.0 + jnp.exp(-ab_ref[pl.ds(r0, c), :])))

    def conv(off, i, h):
        slab = off // D_HEAD + h
        cols = slice(slab * D_HEAD, (slab + 1) * D_HEAD)
        acc = None
        for tap in range(CONV_W):
            rows = _ext_idx(slab, r0 + pad - halo + tap, c)
            term = requests[i][0][rows] * cw_ref[tap:tap + 1, cols]
            acc = term if acc is None else acc + term
        return _silu(acc)

    qs, ks, vs = [], [], []
    for i, h in heads:
        q = conv(0, i, h)
        k = conv(WIDTH, i, h)
        qs.append(q * lax.rsqrt(jnp.sum(q * q, axis=-1, keepdims=True) + EPS)
                  * (D_HEAD ** -0.5))
        ks.append(k * lax.rsqrt(jnp.sum(k * k, axis=-1, keepdims=True) + EPS))
        vs.append(conv(2 * WIDTH, i, h))
    gcs = [jnp.broadcast_to(gc_alls[i][:, h:h + 1], (c, D_HEAD)) for i, h in heads]
    betas = [jnp.broadcast_to(beta_alls[i][:, N_HEADS + h:N_HEADS + h + 1], (c, D_HEAD))
             for i, h in heads]
    g_lasts = [gc[c - 1:c, :] for gc in gcs]
    decays = [jnp.where(tril, jnp.exp2(gc[:, 0:c] - gct_alls[i][h:h + 1, :]), 0.0)
              for gc, (i, h) in zip(gcs, heads)]
    kbs = [k * beta for k, beta in zip(ks, betas)]
    ks_b = [k.astype(BF16) for k in ks]
    kks = [_dot_nt(kb, k) for kb, k in zip(kbs, ks_b)]
    qks = [_dot_nt(q, k) for q, k in zip(qs, ks_b)]
    n_mats = [jnp.where(strict, kk * decay, 0.0) for kk, decay in zip(kks, decays)]
    qks = [(qk * decay).astype(BF16) for qk, decay in zip(qks, decays)]
    invs = _unit_lower_inverses(n_mats, c)
    e_gcs = [jnp.exp2(gc) for gc in gcs]
    rhss = [jnp.concatenate([v * beta, kb * e_gc], axis=1).astype(BF16)
            for v, beta, kb, e_gc in zip(vs, betas, kbs, e_gcs)]
    sols = [_dot(inv, rhs) for inv, rhs in zip(invs, rhss)]
    q_decs = [(q * e_gc).astype(BF16) for q, e_gc in zip(qs, e_gcs)]
    k_decs = [(k * jnp.exp2(g_last - gc)).astype(BF16)
              for k, g_last, gc in zip(ks, g_lasts, gcs)]
    return [(sol[:, 0:D_HEAD], sol[:, D_HEAD:2 * D_HEAD].astype(BF16), qk, q_dec, k_dec,
             jnp.exp2(g_last))
            for sol, qk, q_dec, k_dec, g_last in zip(sols, qks, q_decs, k_decs, g_lasts)]


def _gdn_chunk_apply(c, r0, prepared, requests, og_ref):
    n_req = len(requests)
    heads = _problems(n_req)
    s_olds = [requests[i][5][h] for i, h in heads]
    s_bs = [s.astype(BF16) for s in s_olds]
    on_s = [_dot(jnp.concatenate([k_cum, q_dec], axis=0), s_b)
            for (_, k_cum, _, q_dec, _, _), s_b in zip(prepared, s_bs)]
    us = [w - ks[0:c] for (w, _, _, _, _, _), ks in zip(prepared, on_s)]
    us_b = [u.astype(BF16) for u in us]
    os_ = [ks[c:2 * c] + _dot(qk, u_b)
           for (_, _, qk, _, _, _), ks, u_b in zip(prepared, on_s, us_b)]
    for n, (i, h) in enumerate(heads):
        k_dec, s_decay = prepared[n][4], prepared[n][5]
        requests[i][5][h] = s_olds[n] * s_decay + _dot_tn(k_dec, us_b[n])
    outs = [[] for _ in range(n_req)]
    for n, (i, h) in enumerate(heads):
        o = os_[n]
        o = o * lax.rsqrt(jnp.mean(o * o, axis=-1, keepdims=True) + EPS) * og_ref[...]
        z = requests[i][1][pl.ds(r0, c), h * D_HEAD:(h + 1) * D_HEAD]
        outs[i].append(o * _silu(z))
    return outs


def _decay_logits(ab, alog, dtb):
    return (-LOG2_E * jnp.exp(alog)) * _softplus(ab + dtb)


def _gdn_decode_kernel(c, n_req, qkv_ref, z_ref, ab_ref, abt_ref, cw_ref, alog_ref, dtb_ref,
                       alogc_ref, dtbc_ref, og_ref, s0_ref, conv0_ref,
                       o_ref, s_ref, conv_ref, ext_ref, g_ref, gt_ref):
    halo = CONV_W - 1
    s_ref[...] = s0_ref[...]
    lead_fill = jnp.zeros((EXT_LEAD - halo, D_HEAD), F32)
    for i in range(n_req):
        for slab in range(EXT_SLABS):
            cols = slice(slab * D_HEAD, (slab + 1) * D_HEAD)
            ext_ref[(i,) + _ext_idx(slab, 0, EXT_LEAD)] = jnp.concatenate(
                [lead_fill, conv0_ref[i, :, cols]], axis=0)
            ext_ref[(i,) + _ext_idx(slab, EXT_LEAD, c)] = qkv_ref[i, :, cols]
    g_ref[...] = _decay_logits(ab_ref[...], alog_ref[...], dtb_ref[...])
    gt_ref[...] = _decay_logits(abt_ref[...], alogc_ref[...], dtbc_ref[...])
    requests = [(ext_ref.at[i], z_ref.at[i], ab_ref.at[i], g_ref.at[i], gt_ref.at[i], s_ref.at[i])
                for i in range(n_req)]
    prepared = _gdn_chunk_prepare(c, 0, _chunk_masks(c), requests, cw_ref)
    outs = _gdn_chunk_apply(c, 0, prepared, requests, og_ref)
    for i in range(n_req):
        for h, o in enumerate(outs[i]):
            o_ref[i, :, h * D_HEAD:(h + 1) * D_HEAD] = o
        for slab in range(EXT_SLABS):
            last = ext_ref[(i,) + _ext_idx(slab, c, EXT_LEAD)]
            conv_ref[i, :, slab * D_HEAD:(slab + 1) * D_HEAD] = last[EXT_LEAD - halo:]


def _gdn_core_decode(qkv, z, ab, abt, conv_w, a_log, dt_bias, out_g, s0, conv0, n_req):
    b, c, _ = qkv.shape
    assert b % n_req == 0
    const = lambda shape: pl.BlockSpec(shape, lambda i: (0,) * len(shape))
    per_req = lambda *tail: pl.BlockSpec((n_req,) + tail, lambda i: (i,) + (0,) * len(tail))
    return pl.pallas_call(
        functools.partial(_gdn_decode_kernel, c, n_req),
        grid=(b // n_req,),
        in_specs=[
            per_req(c, CONV_DIM), per_req(c, WIDTH), per_req(c, LANES), per_req(16, c),
            const((CONV_W, CONV_DIM)), const((1, LANES)), const((1, LANES)),
            const((16, 1)), const((16, 1)), const((1, D_HEAD)),
            per_req(N_HEADS, D_HEAD, D_HEAD), per_req(CONV_W - 1, CONV_DIM),
        ],
        out_specs=[per_req(c, WIDTH), per_req(N_HEADS, D_HEAD, D_HEAD),
                   per_req(CONV_W - 1, CONV_DIM)],
        out_shape=[
            jax.ShapeDtypeStruct((b, c, WIDTH), F32),
            jax.ShapeDtypeStruct((b, N_HEADS, D_HEAD, D_HEAD), F32),
            jax.ShapeDtypeStruct((b, CONV_W - 1, CONV_DIM), F32),
        ],
        scratch_shapes=[
            pltpu.VMEM((n_req,) + _ext_shape(c), F32),
            pltpu.VMEM((n_req, c, LANES), F32),
            pltpu.VMEM((n_req, 16, c), F32),
        ],
        compiler_params=_cparams(("arbitrary",)),
        name="gdn_core",
    )(qkv, z, ab, abt, conv_w, *_head_params(a_log, dt_bias), out_g.reshape(1, D_HEAD),
      s0, conv0)


def _head_params(a_log, dt_bias):
    row = lambda v: jnp.pad(v.reshape(1, N_HEADS), ((0, 0), (0, LANES - N_HEADS)))
    col = lambda v: jnp.pad(v.reshape(N_HEADS, 1), ((0, 16 - N_HEADS), (0, 0)))
    return row(a_log), row(dt_bias), col(a_log), col(dt_bias)


def _gdn_layer_kernel(tm, c, x_ref, mod_ref, modn_ref, g_ref, gn_ref, w_ref, wab_ref, wabt_ref,
                      wout_ref, cw_ref, alog_ref, dtb_ref, alogc_ref, dtbc_ref, og_ref,
                      x1_ref, h1_ref, s_ref, conv_ref, ext_ref, z_s, ab_s, gl_s, gt_s):
    t = pl.program_id(1)
    pad = EXT_LEAD
    halo = CONV_W - 1
    n_chunks = tm // c
    n_col = 512
    request = (ext_ref, z_s, ab_s, gl_s, gt_s, s_ref)

    @pl.when(t == 0)
    def _():
        s_ref[...] = jnp.zeros_like(s_ref)
        ext_ref[:, 0:2 * pad, :] = jnp.zeros((EXT_SLABS // 2, 2 * pad, D_HEAD), F32)

    def project(r0):
        hb = _mod_norm(x_ref[pl.ds(r0, c), :], g_ref[...], mod_ref[...]).astype(BF16)
        for lo in range(0, CONV_DIM + WIDTH, n_col):
            res = jnp.dot(hb, w_ref[:, lo:lo + n_col], preferred_element_type=F32)
            if lo < CONV_DIM:
                for j in range(n_col // D_HEAD):
                    ext_ref[_ext_idx(lo // D_HEAD + j, pad + r0, c)] = (
                        res[:, j * D_HEAD:(j + 1) * D_HEAD])
            else:
                z_s[pl.ds(r0, c), lo - CONV_DIM:lo - CONV_DIM + n_col] = res
        ab = jnp.dot(hb, wab_ref[...], preferred_element_type=F32)
        ab_s[pl.ds(r0, c), :] = ab
        gl_s[pl.ds(r0, c), :] = _decay_logits(ab, alog_ref[...], dtb_ref[...])
        abt = lax.dot_general(wabt_ref[...], hb, (((1,), (1,)), ((), ())),
                              preferred_element_type=F32)
        gt_s[:, pl.ds(r0, c)] = _decay_logits(abt, alogc_ref[...], dtbc_ref[...])

    masks = _chunk_masks(c)
    project(0)

    def step(i, carry):
        r0 = pl.multiple_of(i * c, c)
        prepared = _gdn_chunk_prepare(c, r0, masks, [request], cw_ref)
        outs, = _gdn_chunk_apply(c, r0, prepared, [request], og_ref)
        o = jnp.concatenate([o.astype(BF16) for o in outs], axis=1)
        y = jnp.dot(o, wout_ref[...], preferred_element_type=F32)
        x1 = x_ref[pl.ds(r0, c), :] + mod_ref[:, 2 * D_MODEL:3 * D_MODEL] * y
        x1_ref[pl.ds(r0, c), :] = x1
        h1_ref[pl.ds(r0, c), :] = _mod_norm(x1, gn_ref[...], modn_ref[...]).astype(BF16)
        project(pl.multiple_of(jnp.minimum(i + 1, n_chunks - 1) * c, c))
        return carry

    lax.fori_loop(0, n_chunks, step, 0)

    for slab in range(EXT_SLABS):
        last = ext_ref[_ext_idx(slab, tm, pad)]
        ext_ref[_ext_idx(slab, 0, pad)] = last
        conv_ref[:, slab * D_HEAD:(slab + 1) * D_HEAD] = last[pad - halo:]


def _gdn_layer_prompt(x, mod, mod_next, norm_g, norm_g_next, w_main, w_ab, w_abt, w_out, conv_w,
                      a_log, dt_bias, out_g, tm, c):
    b, t, _ = x.shape
    const = lambda shape: pl.BlockSpec(shape, lambda i, j: (0,) * len(shape))
    resident = lambda shape: pl.BlockSpec(shape, lambda i, j: (0,) * len(shape),
                                          pipeline_mode=pl.Buffered(1))
    mod_spec = pl.BlockSpec((None, 1, 3 * D_MODEL), lambda i, j: (i, 0, 0))
    row_spec = pl.BlockSpec((None, tm, D_MODEL), lambda i, j: (i, j, 0))
    return pl.pallas_call(
        functools.partial(_gdn_layer_kernel, tm, c),
        grid=(b, t // tm),
        in_specs=[
            row_spec, mod_spec, mod_spec, const((1, D_MODEL)), const((1, D_MODEL)),
            resident((D_MODEL, CONV_DIM + WIDTH)), const((D_MODEL, LANES)), const((16, D_MODEL)),
            resident((WIDTH, D_MODEL)), const((CONV_W, CONV_DIM)),
            const((1, LANES)), const((1, LANES)), const((16, 1)), const((16, 1)),
            const((1, D_HEAD)),
        ],
        out_specs=[
            row_spec, row_spec,
            pl.BlockSpec((None, N_HEADS, D_HEAD, D_HEAD), lambda i, j: (i, 0, 0, 0)),
            pl.BlockSpec((None, CONV_W - 1, CONV_DIM), lambda i, j: (i, 0, 0)),
        ],
        out_shape=[
            jax.ShapeDtypeStruct((b, t, D_MODEL), F32),
            jax.ShapeDtypeStruct((b, t, D_MODEL), BF16),
            jax.ShapeDtypeStruct((b, N_HEADS, D_HEAD, D_HEAD), F32),
            jax.ShapeDtypeStruct((b, CONV_W - 1, CONV_DIM), F32),
        ],
        scratch_shapes=[
            pltpu.VMEM(_ext_shape(tm), F32),
            pltpu.VMEM((tm, WIDTH), F32),
            pltpu.VMEM((tm, LANES), F32),
            pltpu.VMEM((tm, LANES), F32),
            pltpu.VMEM((16, tm), F32),
        ],
        compiler_params=_cparams(("arbitrary", "arbitrary")),
        name="gdn_layer",
    )(x, mod, mod_next, norm_g.reshape(1, D_MODEL), norm_g_next.reshape(1, D_MODEL),
      w_main, w_ab, w_abt, w_out, conv_w, *_head_params(a_log, dt_bias), out_g.reshape(1, D_HEAD))


def _banded_attention(dil, seq, q_ref, k_ref, v_ref, acc_ref, m_ref, l_ref):
    blk = D_HEAD
    per_step = 8
    n_blk = seq // dil // blk
    with_prev = n_blk > 1
    n_keys = 2 * blk if with_prev else blk
    row = lax.broadcasted_iota(jnp.int32, (blk, n_keys), 0)
    col = lax.broadcasted_iota(jnp.int32, (blk, n_keys), 1)
    is_prev = col < (n_keys - blk)
    slack = jnp.where(is_prev, col - row, row + (n_keys - blk) - col)
    cur_ok = slack >= 0
    ones = jnp.ones((n_keys, blk), BF16)

    def rows(start):
        if dil == 1:
            return pl.ds(start, blk)
        return pl.ds(start, blk, stride=dil)

    def body(i, carry):
        starts, masks, qs, kcs, vcs = [], [], [], [], []
        for u in range(per_step):
            idx = i * per_step + u
            r = idx // n_blk
            n = idx % n_blk
            start = r + dil * blk * n
            starts.append(start)
            qs.append(q_ref[rows(start), :].astype(BF16))
            k_cur = k_ref[rows(start), :].astype(BF16)
            v_cur = v_ref[rows(start), :].astype(BF16)
            if with_prev:
                start_p = r + dil * blk * jnp.maximum(n - 1, 0)
                first = jnp.where(n > 0, 0, 2 * blk)
                masks.append(slack - jnp.where(is_prev, first, 0) >= 0)
                k_cur = jnp.concatenate([k_ref[rows(start_p), :].astype(BF16), k_cur], axis=0)
                v_cur = jnp.concatenate([v_ref[rows(start_p), :].astype(BF16), v_cur], axis=0)
            else:
                masks.append(cur_ok)
            kcs.append(k_cur)
            vcs.append(jnp.concatenate([v_cur, ones], axis=1))
        ss = [jnp.where(mask, _dot_nt(q, kc), NEG_BIG) for q, kc, mask in zip(qs, kcs, masks)]
        ms = [jnp.max(s, axis=-1, keepdims=True) for s in ss]
        ps = [jnp.exp2(s - m).astype(BF16) for s, m in zip(ss, ms)]
        accs = [jnp.dot(p, vc, preferred_element_type=F32) for p, vc in zip(ps, vcs)]
        for start, acc, m in zip(starts, accs, ms):
            acc_ref[rows(start), :] = acc[:, 0:blk]
            l_ref[rows(start), :] = acc[:, blk:2 * blk]
            m_ref[rows(start), :] = jnp.broadcast_to(m, (blk, blk))
        return carry

    lax.fori_loop(0, dil * n_blk // per_step, body, 0)


def _attn_layer_kernel(seq, h_ref, *refs):
    n_g = len(DIL_GROUPS)
    w_refs = refs[:3 * n_g + 1]
    (o_ref, kv0_ref, kv1_ref, kv2_ref, q_s, k_s, v_s, z_s, acc_s, m_s, l_s,
     sems) = refs[3 * n_g + 1:]
    b = pl.program_id(0)
    hp = pl.program_id(1)
    kv_refs = (kv0_ref, kv1_ref, kv2_ref)

    def proj(slab):
        return jnp.dot(h_ref[...], w_refs[slab][...], preferred_element_type=F32)

    def kv_copy(g, i, kv):
        keep = kv_refs[g].shape[1]
        src = (k_s, v_s)[kv].at[g, i, pl.ds(seq - keep, keep), :]
        col = pl.multiple_of(kv * WIDTH + (2 * hp + i) * D_HEAD, D_HEAD)
        dst = kv_refs[g].at[b, :, pl.ds(col, D_HEAD)]
        return pltpu.make_async_copy(src, dst, sems.at[(g * 2 + i) * 2 + kv])

    for g in range(n_g):
        k = proj(n_g + g)
        v = proj(2 * n_g + g)
        for i in range(2):
            k_s[g, i] = k[:, i * D_HEAD:(i + 1) * D_HEAD]
            v_s[g, i] = v[:, i * D_HEAD:(i + 1) * D_HEAD]
        for i in range(2):
            kv_copy(g, i, 0).start()
            kv_copy(g, i, 1).start()
        q = proj(g) * (ATTN_SCALE * LOG2_E)
        for i in range(2):
            q_s[g, i] = q[:, i * D_HEAD:(i + 1) * D_HEAD]

    for i in range(2):
        for g, (_, dil) in enumerate(DIL_GROUPS):
            _banded_attention(dil, seq, q_s.at[g, i], k_s.at[g, i], v_s.at[g, i],
                              acc_s.at[g], m_s.at[g], l_s.at[g])
        if i == 0:
            z = proj(3 * n_g)
            for j in range(2):
                z_s[j] = z[:, j * D_HEAD:(j + 1) * D_HEAD]
        m_all = jnp.maximum(jnp.maximum(m_s[0], m_s[1]), m_s[2])
        num = jnp.zeros((seq, D_HEAD), F32)
        den = jnp.zeros((seq, D_HEAD), F32)
        for g in range(n_g):
            w = jnp.exp2(m_s[g] - m_all)
            num = num + w * acc_s[g]
            den = den + w * l_s[g]
        o_ref[:, i * D_HEAD:(i + 1) * D_HEAD] = (num / den * _silu(z_s[i])).astype(BF16)

    for g in range(n_g):
        for i in range(2):
            kv_copy(g, i, 0).wait()
            kv_copy(g, i, 1).wait()


def _attn_layer_prompt(h, w):
    b, seq, _ = h.shape
    n_pair = N_HEADS // 2
    n_g = len(DIL_GROUPS)
    pair = 2 * D_HEAD
    keeps = [min(window, seq) for window, _ in DIL_GROUPS]
    slab = lambda n: pltpu.VMEM((n, seq, D_HEAD), F32)
    w_specs = [pl.BlockSpec((D_MODEL, pair), lambda i, p, s=s: (0, s * n_pair + p))
               for s in range(3 * n_g + 1)]
    return pl.pallas_call(
        functools.partial(_attn_layer_kernel, seq),
        grid=(b, n_pair),
        in_specs=[
            pl.BlockSpec((None, seq, D_MODEL), lambda i, p: (i, 0, 0),
                         pipeline_mode=pl.Buffered(1)),
        ] + w_specs,
        out_specs=[pl.BlockSpec((None, seq, 2 * D_HEAD), lambda i, p: (i, 0, p))]
        + [pl.BlockSpec(memory_space=pl.ANY)] * n_g,
        out_shape=[jax.ShapeDtypeStruct((b, seq, WIDTH), BF16)]
        + [jax.ShapeDtypeStruct((b, keep, 2 * WIDTH), F32) for keep in keeps],
        scratch_shapes=[
            pltpu.VMEM((n_g, 2, seq, D_HEAD), F32),
            pltpu.VMEM((n_g, 2, seq, D_HEAD), F32),
            pltpu.VMEM((n_g, 2, seq, D_HEAD), F32),
            slab(2),
            slab(n_g), slab(n_g), slab(n_g),
            pltpu.SemaphoreType.DMA((n_g * 2 * 2,)),
        ],
        compiler_params=_cparams(("arbitrary", "arbitrary")),
        name="attn_layer",
    )(h, *([w] * (3 * n_g + 1)))


def _attn_sample_kernel(n_new, *refs):
    q_ref, kv0_ref, kv1_ref, kv2_ref, z_ref, c0_ref, c1_ref, c2_ref, o_ref = refs
    n_t = KEYS_PER_QUERY
    ones = jnp.ones((D_HEAD, D_HEAD), BF16)
    dil1 = DIL_GROUPS[1][1]

    def key_tiles(g, l, kv):
        if g == 0:
            return jnp.concatenate([c0_ref[l:, kv], kv0_ref[0:l + 1, kv]], axis=0)
        if g == 1:
            a, r = divmod(l, dil1)
            parts = [c1_ref[a:, r, kv]]
            if a:
                parts.append(kv1_ref[r:r + 1, kv])
            parts.append(kv1_ref[l:l + 1, kv])
            return jnp.concatenate(parts, axis=0)
        return jnp.concatenate([c2_ref[:, l, kv], kv2_ref[l:l + 1, kv]], axis=0)

    for l in range(n_new):
        m_g, l_g, acc_g = [], [], []
        for g in range(len(DIL_GROUPS)):
            q = q_ref[l, g] * (ATTN_SCALE * LOG2_E)
            prod = (key_tiles(g, l, 0) * q[None]).reshape(n_t * N_HEADS, D_HEAD)
            s = jnp.dot(prod.astype(BF16), ones, preferred_element_type=F32)
            s = s.reshape(n_t, N_HEADS, D_HEAD)
            m = jnp.max(s, axis=0)
            p = jnp.exp2(s - m[None])
            m_g.append(m)
            l_g.append(jnp.sum(p, axis=0))
            acc_g.append(jnp.sum(p * key_tiles(g, l, 1), axis=0))
        m_all = jnp.maximum(jnp.maximum(m_g[0], m_g[1]), m_g[2])
        w_g = [jnp.exp2(m - m_all) for m in m_g]
        num = w_g[0] * acc_g[0] + w_g[1] * acc_g[1] + w_g[2] * acc_g[2]
        den = w_g[0] * l_g[0] + w_g[1] * l_g[1] + w_g[2] * l_g[2]
        o_ref[l] = num / den * _silu(z_ref[l])


def _attn_sample(q, kvs, z, caches):
    b, n_new = q.shape[:2]
    assert n_new == SUBLANES
    (w0, d0), (w1, d1), (w2, d2) = DIL_GROUPS
    assert caches[0].shape[1] == w0 and caches[1].shape[1] == w1 and caches[2].shape[1] == w2
    assert d0 == 1 and n_new % d1 == 0 and d2 == 2 * n_new
    assert w0 // d0 + 1 == KEYS_PER_QUERY and w1 // d1 + 1 == KEYS_PER_QUERY
    assert w2 // d2 + 1 == KEYS_PER_QUERY
    c1 = caches[1].reshape(b, w1 // d1, d1, 2, N_HEADS, D_HEAD)
    c2 = caches[2].reshape(b, w2 // d2, d2, 2, N_HEADS, D_HEAD)
    tail = (2, N_HEADS, D_HEAD)
    new_spec = pl.BlockSpec((None, n_new) + tail, lambda i: (i, 0, 0, 0, 0))
    return pl.pallas_call(
        functools.partial(_attn_sample_kernel, n_new),
        grid=(b,),
        in_specs=[
            pl.BlockSpec((None, n_new, 3, N_HEADS, D_HEAD), lambda i: (i, 0, 0, 0, 0)),
            new_spec, new_spec, new_spec,
            pl.BlockSpec((None, n_new, N_HEADS, D_HEAD), lambda i: (i, 0, 0, 0)),
            pl.BlockSpec((None, w0) + tail, lambda i: (i, 0, 0, 0, 0)),
            pl.BlockSpec((None, w1 // d1, d1) + tail, lambda i: (i, 0, 0, 0, 0, 0)),
            pl.BlockSpec((None, w2 // d2, n_new) + tail, lambda i: (i, 0, 0, 0, 0, 0)),
        ],
        out_specs=pl.BlockSpec((None, n_new, N_HEADS, D_HEAD), lambda i: (i, 0, 0, 0)),
        out_shape=jax.ShapeDtypeStruct((b, n_new, N_HEADS, D_HEAD), F32),
        compiler_params=_cparams(("arbitrary",)),
        name="attn_sample",
    )(q, kvs[0], kvs[1], kvs[2], z, caches[0], c1, c2)


def kernel(x_prompt, x_sample, state_delta, state_conv, cache_kv_w128, cache_kv_w512, cache_kv_w2048,
           c_prompt, c_sample, norm_g, ada_w, ada_b, a_w_in, a_conv_w, a_A_log, a_dt_bias,
           a_out_norm_g, a_w_out, b_w_in, b_w_out, final_norm_g):
    bp, seq, _ = x_prompt.shape
    bs, n_new, _ = x_sample.shape
    n_s = bs * n_new

    mod = _ada_mod(jnp.concatenate([c_prompt, c_sample], axis=0), ada_w, ada_b)
    mod_p = [mod[l, :bp].reshape(bp, 1, 3 * D_MODEL) for l in range(2)]
    mod_s = [jnp.repeat(mod[l, bp:], n_new, axis=0).reshape(1, n_s, 3 * D_MODEL) for l in range(2)]

    w_in = a_w_in[0]
    w_main = w_in[:, :CONV_DIM + WIDTH].astype(BF16)
    w_ab = jnp.pad(w_in[:, CONV_DIM + WIDTH:].astype(BF16), ((0, 0), (0, LANES - 2 * N_HEADS)))
    w_abt = w_in[:, CONV_DIM + WIDTH:].T.astype(BF16)
    w_out_a = a_w_out[0].astype(BF16)
    xs_flat = x_sample.reshape(1, n_s, D_MODEL)

    x1_p, h1_p, delta_p, conv_p = _gdn_layer_prompt(
        x_prompt, mod_p[0], mod_p[1], norm_g[0], norm_g[1], w_main, w_ab, w_abt, w_out_a,
        a_conv_w[0], a_A_log[0], a_dt_bias[0], a_out_norm_g[0], 1024, 128)

    qkv_s, z_s, ab_s, abt_s = _project(xs_flat, mod_s[0], norm_g[0], w_main, (3, 1), n_s,
                                       w_ab, w_abt)
    abt_s = abt_s.reshape(16, bs, n_new).transpose(1, 0, 2)
    o_s, delta_s, conv_s = _gdn_core_decode(
        qkv_s.reshape(bs, n_new, CONV_DIM), z_s.reshape(bs, n_new, WIDTH),
        ab_s.reshape(bs, n_new, LANES), abt_s, a_conv_w[0], a_A_log[0], a_dt_bias[0],
        a_out_norm_g[0], state_delta[0], state_conv[0], 4)
    x1_s = _out_project(xs_flat, o_s.reshape(1, n_s, WIDTH), mod_s[0], w_out_a, final_norm_g,
                        n_s, False)

    n_g = len(DIL_GROUPS)
    w_b = b_w_in[0].astype(BF16)
    w_out_b = b_w_out[0].astype(BF16)

    ob_p, kv0_p, kv1_p, kv2_p = _attn_layer_prompt(h1_p, w_b)
    y_p = _out_project(x1_p, ob_p, mod_p[1], w_out_b, final_norm_g, 512, True)

    q_s, k_s, v_s, zb_s = _project(x1_s, mod_s[1], norm_g[1], w_b, (n_g, n_g, n_g, 1), n_s)
    k_s = k_s.reshape(bs, n_new, n_g, N_HEADS, D_HEAD)
    v_s = v_s.reshape(bs, n_new, n_g, N_HEADS, D_HEAD)
    kvn = [jnp.stack([k_s[:, :, g], v_s[:, :, g]], axis=2) for g in range(n_g)]
    ob_s = _attn_sample(q_s.reshape(bs, n_new, n_g, N_HEADS, D_HEAD), kvn,
                        zb_s.reshape(bs, n_new, N_HEADS, D_HEAD),
                        (cache_kv_w128[0], cache_kv_w512[0], cache_kv_w2048[0]))
    y_s = _out_project(x1_s, ob_s.reshape(1, n_s, WIDTH), mod_s[1], w_out_b, final_norm_g,
                       n_s, True)

    def kv_prompt(kv):
        return kv.reshape(1, bp, kv.shape[1], 2, N_HEADS, D_HEAD)

    return (y_p, y_s.reshape(bs, n_new, D_MODEL),
            delta_p[None], delta_s[None], conv_p[None], conv_s[None],
            kv_prompt(kv0_p), kvn[0][None], kv_prompt(kv1_p), kvn[1][None],
            kv_prompt(kv2_p), kvn[2][None])
```

```python
import functools

import jax
import jax.numpy as jnp
from jax import lax
from jax.experimental import pallas as pl
from jax.experimental.pallas import tpu as pltpu

F32 = jnp.float32
BF16 = jnp.bfloat16

D_MODEL = 1024
N_HEADS = 8
D_HEAD = 128
WIDTH = N_HEADS * D_HEAD
CONV_W = 4
CONV_DIM = 3 * WIDTH
DIL_GROUPS = ((128, 1), (512, 4), (2048, 16))
KEYS_PER_QUERY = 129
EPS = 1e-6
ATTN_SCALE = D_HEAD ** -0.5
LOG2_E = 1.4426950408889634
NEG_BIG = -1e30

LANES = 128
SUBLANES = 8
VMEM_LIMIT_BYTES = 56 * 1024 * 1024


def _cparams(semantics):
    return pltpu.CompilerParams(dimension_semantics=semantics,
                                vmem_limit_bytes=VMEM_LIMIT_BYTES)


def _silu(x):
    return x * (1.0 / (1.0 + jnp.exp(-x)))


def _softplus(x):
    return jnp.maximum(x, 0.0) + jnp.log1p(jnp.exp(-jnp.abs(x)))


def _dot(a, b):
    return jnp.dot(a.astype(BF16), b.astype(BF16), preferred_element_type=F32)


def _dot_nt(a, b):
    return lax.dot_general(a.astype(BF16), b.astype(BF16), (((1,), (1,)), ((), ())),
                           preferred_element_type=F32)


def _dot_tn(a, b):
    return lax.dot_general(a.astype(BF16), b.astype(BF16), (((0,), (0,)), ((), ())),
                           preferred_element_type=F32)


def _split3(x):
    x1 = x.astype(BF16)
    r = x - x1.astype(F32)
    x2 = r.astype(BF16)
    x3 = (r - x2.astype(F32)).astype(BF16)
    return x1, x2, x3


def _dot_sel(sel, x):
    s = sel.astype(BF16)
    out = None
    for p in _split3(x):
        t = jnp.dot(s, p, preferred_element_type=F32)
        out = t if out is None else out + t
    return out


def _dot_x_sel(x, sel):
    s = sel.astype(BF16)
    out = None
    for p in _split3(x):
        t = jnp.dot(p, s, preferred_element_type=F32)
        out = t if out is None else out + t
    return out


def _mod_kernel(c_ref, w_ref, b_ref, o_ref):
    s = _silu(c_ref[...])
    o_ref[...] = _dot(s, w_ref[...]) + b_ref[...]


def _ada_mod(c_all, ada_w, ada_b):
    n_layers = ada_w.shape[0]
    rows = c_all.shape[0]
    tn = 768
    return pl.pallas_call(
        _mod_kernel,
        grid=(n_layers, 3 * D_MODEL // tn),
        in_specs=[
            pl.BlockSpec((rows, D_MODEL), lambda l, j: (0, 0)),
            pl.BlockSpec((None, D_MODEL, tn), lambda l, j: (l, 0, j)),
            pl.BlockSpec((None, 1, tn), lambda l, j: (l, 0, j)),
        ],
        out_specs=pl.BlockSpec((None, rows, tn), lambda l, j: (l, 0, j)),
        out_shape=jax.ShapeDtypeStruct((n_layers, rows, 3 * D_MODEL), F32),
        compiler_params=_cparams(("arbitrary", "arbitrary")),
        name="ada_mod",
    )(c_all, ada_w, ada_b.reshape(n_layers, 1, 3 * D_MODEL))


def _proj_kernel(seg_bounds, has_ab, *refs):
    x_ref, mod_ref, g_ref, w_ref = refs[:4]
    pos = 4
    if has_ab:
        wab_ref = refs[4]
        pos = 5
    n_seg = len(seg_bounds)
    seg_refs = refs[pos:pos + n_seg]
    pos += n_seg
    if has_ab:
        ab_ref, abt_ref = refs[pos:pos + 2]
        pos += 2
    h_ref = refs[pos]
    j = pl.program_id(2)

    @pl.when(j == 0)
    def _():
        hb = _mod_norm(x_ref[...], g_ref[...], mod_ref[...]).astype(BF16)
        h_ref[...] = hb
        if has_ab:
            ab = jnp.dot(hb, wab_ref[...], preferred_element_type=F32)
            ab_ref[...] = ab
            abt_ref[...] = ab.T[0:16, :]

    res = jnp.dot(h_ref[...], w_ref[...], preferred_element_type=F32)
    for (lo, hi), o_ref in zip(seg_bounds, seg_refs):
        @pl.when((j >= lo) & (j < hi))
        def _(o_ref=o_ref):
            o_ref[...] = res


def _project(x, mod, norm_g, w, seg_slabs, tm, w_ab=None):
    n, t, _ = x.shape
    r = mod.shape[1]
    slab = 1024
    n_slab = w.shape[1] // slab
    assert sum(seg_slabs) == n_slab and t % tm == 0
    has_ab = w_ab is not None
    bounds, lo = [], 0
    for s in seg_slabs:
        bounds.append((lo, lo + s))
        lo += s
    mod_rows = 1 if r == 1 else tm
    mod_map = (lambda b, i, j: (b, 0, 0)) if r == 1 else (lambda b, i, j: (b, i, 0))
    in_specs = [
        pl.BlockSpec((None, tm, D_MODEL), lambda b, i, j: (b, i, 0)),
        pl.BlockSpec((None, mod_rows, 3 * D_MODEL), mod_map),
        pl.BlockSpec((1, D_MODEL), lambda b, i, j: (0, 0)),
        pl.BlockSpec((D_MODEL, slab), lambda b, i, j: (0, j)),
    ]
    args = [x, mod, norm_g.reshape(1, D_MODEL), w]
    if has_ab:
        in_specs.append(pl.BlockSpec((D_MODEL, LANES), lambda b, i, j: (0, 0)))
        args.append(w_ab)
    out_specs, out_shapes = [], []
    for (lo, hi) in bounds:
        def seg_map(b, i, j, lo=lo, hi=hi):
            return (b, i, jnp.clip(j - lo, 0, hi - lo - 1))
        out_specs.append(pl.BlockSpec((None, tm, slab), seg_map))
        out_shapes.append(jax.ShapeDtypeStruct((n, t, slab * (hi - lo)), F32))
    if has_ab:
        out_specs += [pl.BlockSpec((None, tm, LANES), lambda b, i, j: (b, i, 0)),
                      pl.BlockSpec((None, 16, tm), lambda b, i, j: (b, 0, i))]
        out_shapes += [jax.ShapeDtypeStruct((n, t, LANES), F32),
                       jax.ShapeDtypeStruct((n, 16, t), F32)]
    return pl.pallas_call(
        functools.partial(_proj_kernel, tuple(bounds), has_ab),
        grid=(n, t // tm, n_slab),
        in_specs=in_specs,
        out_specs=out_specs,
        out_shape=out_shapes,
        scratch_shapes=[pltpu.VMEM((tm, D_MODEL), BF16)],
        compiler_params=_cparams(("arbitrary", "arbitrary", "arbitrary")),
        name="norm_proj",
    )(*args)


def _mod_norm(x, g, mod):
    ms = jnp.mean(x * x, axis=-1, keepdims=True)
    y = x * lax.rsqrt(ms + EPS) * g
    return y * (1.0 + mod[:, D_MODEL:2 * D_MODEL]) + mod[:, 0:D_MODEL]


def _out_kernel(final_norm, x_ref, o_ref, mod_ref, w_ref, fg_ref, out_ref):
    y = _dot(o_ref[...], w_ref[...])
    gate = mod_ref[:, 2 * D_MODEL:3 * D_MODEL]
    x = x_ref[...] + gate * y
    if final_norm:
        ms = jnp.mean(x * x, axis=-1, keepdims=True)
        x = x * lax.rsqrt(ms + EPS) * fg_ref[...]
    out_ref[...] = x


def _out_project(x, o, mod, w_out, final_g, tm, final_norm):
    n, t, _ = x.shape
    r = mod.shape[1]
    mod_rows = 1 if r == 1 else tm
    mod_map = (lambda b, i: (b, 0, 0)) if r == 1 else (lambda b, i: (b, i, 0))
    row_spec = pl.BlockSpec((None, tm, D_MODEL), lambda b, i: (b, i, 0))
    return pl.pallas_call(
        functools.partial(_out_kernel, final_norm),
        grid=(n, t // tm),
        in_specs=[
            row_spec,
            pl.BlockSpec((None, tm, WIDTH), lambda b, i: (b, i, 0)),
            pl.BlockSpec((None, mod_rows, 3 * D_MODEL), mod_map),
            pl.BlockSpec((WIDTH, D_MODEL), lambda b, i: (0, 0)),
            pl.BlockSpec((1, D_MODEL), lambda b, i: (0, 0)),
        ],
        out_specs=row_spec,
        out_shape=jax.ShapeDtypeStruct((n, t, D_MODEL), F32),
        compiler_params=_cparams(("arbitrary", "arbitrary")),
        name="out_proj",
    )(x, o, mod, w_out, final_g.reshape(1, D_MODEL))


def _unit_lower_inverses(n_mats, c):
    row = lax.broadcasted_iota(jnp.int32, (c, c), 0)
    col = lax.broadcasted_iota(jnp.int32, (c, c), 1)
    base = min(16, c)
    blk_id = lambda v, size: jnp.right_shift(v, size.bit_length() - 1)
    eye = (row == col).astype(F32)
    same = blk_id(row, base) == blk_id(col, base)
    ps = [jnp.where(same, -n, 0.0) for n in n_mats]
    invs = [eye + p for p in ps]
    assert base >= 4
    ps = [_dot(p, p).astype(BF16) for p in ps]
    power = 2
    while 2 * power < base:
        if c % LANES == 0:
            wide = [_dot(p, jnp.concatenate([p, inv.astype(BF16)], axis=1))
                    for p, inv in zip(ps, invs)]
            invs = [inv + w[:, c:] for inv, w in zip(invs, wide)]
            ps = [w[:, 0:c].astype(BF16) for w in wide]
        else:
            invs = [inv + _dot(p, inv) for p, inv in zip(ps, invs)]
            ps = [_dot(p, p).astype(BF16) for p in ps]
        power *= 2
    invs = [inv + _dot(p, inv) for p, inv in zip(ps, invs)]
    size = base
    while size < c:
        same_next = blk_id(row, 2 * size) == blk_id(col, 2 * size)
        off = same_next & jnp.logical_not(same)
        ls = [jnp.where(off, n, 0.0).astype(BF16) for n in n_mats]
        invs_b = [inv.astype(BF16) for inv in invs]
        ts = [_dot(ib, l) for ib, l in zip(invs_b, ls)]
        invs = [inv - _dot(t, ib) for inv, t, ib in zip(invs, ts, invs_b)]
        same = same_next
        size *= 2
    return invs


def _aligned(v, m):
    return v if isinstance(v, int) else pl.multiple_of(v, m)


EXT_SLABS = CONV_DIM // D_HEAD
EXT_LEAD = SUBLANES


def _ext_idx(slab, start, n):
    return (slab // 2, pl.ds(2 * start + slab % 2, n, stride=2), slice(None))


def _ext_shape(rows):
    return (EXT_SLABS // 2, 2 * (rows + EXT_LEAD), D_HEAD)


def _chunk_masks(c):
    row = lax.broadcasted_iota(jnp.int32, (c, c), 0)
    col = lax.broadcasted_iota(jnp.int32, (c, c), 1)
    tril_sel = jnp.where(row >= col, 1.0, 0.0).astype(BF16)
    triu_sel = jnp.where(row <= col, 1.0, 0.0).astype(BF16)
    return row >= col, row > col, tril_sel, triu_sel


def _problems(n_req):
    return [(i, h) for i in range(n_req) for h in range(N_HEADS)]


def _gdn_chunk_prepare(c, r0, masks, requests, cw_ref):
    tril, strict, tril_sel, triu_sel = masks
    pad = EXT_LEAD
    halo = CONV_W - 1
    heads = _problems(len(requests))
    gc_alls, gct_alls, beta_alls = [], [], []
    for _, _, ab_ref, g_ref, gt_ref, _ in requests:
        gc_alls.append(_dot_sel(tril_sel, g_ref[pl.ds(r0, c), :]))
        gct_alls.append(_dot_x_sel(gt_ref[0:N_HEADS, pl.ds(r0, c)], triu_sel))
        beta_alls.append(1.0 / (1.0 + jnp.exp(-ab_ref[pl.ds(r0, c), :])))

    def conv(off, i, h):
        slab = off // D_HEAD + h
        cols = slice(slab * D_HEAD, (slab + 1) * D_HEAD)
        acc = None
        for tap in range(CONV_W):
            rows = _ext_idx(slab, r0 + pad - halo + tap, c)
            term = requests[i][0][rows] * cw_ref[tap:tap + 1, cols]
            acc = term if acc is None else acc + term
        return _silu(acc)

    qs, ks, vs = [], [], []
    for i, h in heads:
        q = conv(0, i, h)
        k = conv(WIDTH, i, h)
        qs.append(q * lax.rsqrt(jnp.sum(q * q, axis=-1, keepdims=True) + EPS)
                  * (D_HEAD ** -0.5))
        ks.append(k * lax.rsqrt(jnp.sum(k * k, axis=-1, keepdims=True) + EPS))
        vs.append(conv(2 * WIDTH, i, h))
    gcs = [jnp.broadcast_to(gc_alls[i][:, h:h + 1], (c, D_HEAD)) for i, h in heads]
    betas = [jnp.broadcast_to(beta_alls[i][:, N_HEADS + h:N_HEADS + h + 1], (c, D_HEAD))
             for i, h in heads]
    g_lasts = [gc[c - 1:c, :] for gc in gcs]
    decays = [jnp.where(tril, jnp.exp2(gc[:, 0:c] - gct_alls[i][h:h + 1, :]), 0.0)
              for gc, (i, h) in zip(gcs, heads)]
    kbs = [k * beta for k, beta in zip(ks, betas)]
    ks_b = [k.astype(BF16) for k in ks]
    kks = [_dot_nt(kb, k) for kb, k in zip(kbs, ks_b)]
    qks = [_dot_nt(q, k) for q, k in zip(qs, ks_b)]
    n_mats = [jnp.where(strict, kk * decay, 0.0) for kk, decay in zip(kks, decays)]
    qks = [(qk * decay).astype(BF16) for qk, decay in zip(qks, decays)]
    invs = _unit_lower_inverses(n_mats, c)
    e_gcs = [jnp.exp2(gc) for gc in gcs]
    rhss = [jnp.concatenate([v * beta, kb * e_gc], axis=1).astype(BF16)
            for v, beta, kb, e_gc in zip(vs, betas, kbs, e_gcs)]
    sols = [_dot(inv, rhs) for inv, rhs in zip(invs, rhss)]
    q_decs = [(q * e_gc).astype(BF16) for q, e_gc in zip(qs, e_gcs)]
    k_decs = [(k * jnp.exp2(g_last - gc)).astype(BF16)
              for k, g_last, gc in zip(ks, g_lasts, gcs)]
    return [(sol[:, 0:D_HEAD], sol[:, D_HEAD:2 * D_HEAD].astype(BF16), qk, q_dec, k_dec,
             jnp.exp2(g_last))
            for sol, qk, q_dec, k_dec, g_last in zip(sols, qks, q_decs, k_decs, g_lasts)]


def _gdn_chunk_apply(c, r0, prepared, requests, og_ref):
    n_req = len(requests)
    heads = _problems(n_req)
    s_olds = [requests[i][5][h] for i, h in heads]
    s_bs = [s.astype(BF16) for s in s_olds]
    on_s = [_dot(jnp.concatenate([k_cum, q_dec], axis=0), s_b)
            for (_, k_cum, _, q_dec, _, _), s_b in zip(prepared, s_bs)]
    us = [w - ks[0:c] for (w, _, _, _, _, _), ks in zip(prepared, on_s)]
    us_b = [u.astype(BF16) for u in us]
    os_ = [ks[c:2 * c] + _dot(qk, u_b)
           for (_, _, qk, _, _, _), ks, u_b in zip(prepared, on_s, us_b)]
    for n, (i, h) in enumerate(heads):
        k_dec, s_decay = prepared[n][4], prepared[n][5]
        requests[i][5][h] = s_olds[n] * s_decay + _dot_tn(k_dec, us_b[n])
    outs = [[] for _ in range(n_req)]
    for n, (i, h) in enumerate(heads):
        o = os_[n]
        o = o * lax.rsqrt(jnp.mean(o * o, axis=-1, keepdims=True) + EPS) * og_ref[...]
        z = requests[i][1][pl.ds(r0, c), h * D_HEAD:(h + 1) * D_HEAD]
        outs[i].append(o * _silu(z))
    return outs


def _decay_logits(ab, alog, dtb):
    return (-LOG2_E * jnp.exp(alog)) * _softplus(ab + dtb)


def _gdn_decode_kernel(c, n_req, qkv_ref, z_ref, ab_ref, abt_ref, cw_ref, alog_ref, dtb_ref,
                       alogc_ref, dtbc_ref, og_ref, s0_ref, conv0_ref,
                       o_ref, s_ref, conv_ref, ext_ref, g_ref, gt_ref):
    halo = CONV_W - 1
    s_ref[...] = s0_ref[...]
    lead_fill = jnp.zeros((EXT_LEAD - halo, D_HEAD), F32)
    for i in range(n_req):
        for slab in range(EXT_SLABS):
            cols = slice(slab * D_HEAD, (slab + 1) * D_HEAD)
            ext_ref[(i,) + _ext_idx(slab, 0, EXT_LEAD)] = jnp.concatenate(
                [lead_fill, conv0_ref[i, :, cols]], axis=0)
            ext_ref[(i,) + _ext_idx(slab, EXT_LEAD, c)] = qkv_ref[i, :, cols]
    g_ref[...] = _decay_logits(ab_ref[...], alog_ref[...], dtb_ref[...])
    gt_ref[...] = _decay_logits(abt_ref[...], alogc_ref[...], dtbc_ref[...])
    requests = [(ext_ref.at[i], z_ref.at[i], ab_ref.at[i], g_ref.at[i], gt_ref.at[i], s_ref.at[i])
                for i in range(n_req)]
    prepared = _gdn_chunk_prepare(c, 0, _chunk_masks(c), requests, cw_ref)
    outs = _gdn_chunk_apply(c, 0, prepared, requests, og_ref)
    for i in range(n_req):
        for h, o in enumerate(outs[i]):
            o_ref[i, :, h * D_HEAD:(h + 1) * D_HEAD] = o
        for slab in range(EXT_SLABS):
            last = ext_ref[(i,) + _ext_idx(slab, c, EXT_LEAD)]
            conv_ref[i, :, slab * D_HEAD:(slab + 1) * D_HEAD] = last[EXT_LEAD - halo:]


def _gdn_core_decode(qkv, z, ab, abt, conv_w, a_log, dt_bias, out_g, s0, conv0, n_req):
    b, c, _ = qkv.shape
    assert b % n_req == 0
    const = lambda shape: pl.BlockSpec(shape, lambda i: (0,) * len(shape))
    per_req = lambda *tail: pl.BlockSpec((n_req,) + tail, lambda i: (i,) + (0,) * len(tail))
    return pl.pallas_call(
        functools.partial(_gdn_decode_kernel, c, n_req),
        grid=(b // n_req,),
        in_specs=[
            per_req(c, CONV_DIM), per_req(c, WIDTH), per_req(c, LANES), per_req(16, c),
            const((CONV_W, CONV_DIM)), const((1, LANES)), const((1, LANES)),
            const((16, 1)), const((16, 1)), const((1, D_HEAD)),
            per_req(N_HEADS, D_HEAD, D_HEAD), per_req(CONV_W - 1, CONV_DIM),
        ],
        out_specs=[per_req(c, WIDTH), per_req(N_HEADS, D_HEAD, D_HEAD),
                   per_req(CONV_W - 1, CONV_DIM)],
        out_shape=[
            jax.ShapeDtypeStruct((b, c, WIDTH), F32),
            jax.ShapeDtypeStruct((b, N_HEADS, D_HEAD, D_HEAD), F32),
            jax.ShapeDtypeStruct((b, CONV_W - 1, CONV_DIM), F32),
        ],
        scratch_shapes=[
            pltpu.VMEM((n_req,) + _ext_shape(c), F32),
            pltpu.VMEM((n_req, c, LANES), F32),
            pltpu.VMEM((n_req, 16, c), F32),
        ],
        compiler_params=_cparams(("arbitrary",)),
        name="gdn_core",
    )(qkv, z, ab, abt, conv_w, *_head_params(a_log, dt_bias), out_g.reshape(1, D_HEAD),
      s0, conv0)


def _head_params(a_log, dt_bias):
    row = lambda v: jnp.pad(v.reshape(1, N_HEADS), ((0, 0), (0, LANES - N_HEADS)))
    col = lambda v: jnp.pad(v.reshape(N_HEADS, 1), ((0, 16 - N_HEADS), (0, 0)))
    return row(a_log), row(dt_bias), col(a_log), col(dt_bias)


def _gdn_layer_kernel(tm, c, x_ref, mod_ref, modn_ref, g_ref, gn_ref, w_ref, wab_ref,
                      wout_ref, cw_ref, alog_ref, dtb_ref, alogc_ref, dtbc_ref, og_ref,
                      x1_ref, h1_ref, s_ref, conv_ref, ext_ref, z_s, ab_s, gl_s, gt_s):
    t = pl.program_id(1)
    pad = EXT_LEAD
    halo = CONV_W - 1
    n_chunks = tm // c
    n_col = 512
    request = (ext_ref, z_s, ab_s, gl_s, gt_s, s_ref)

    @pl.when(t == 0)
    def _():
        s_ref[...] = jnp.zeros_like(s_ref)
        ext_ref[:, 0:2 * pad, :] = jnp.zeros((EXT_SLABS // 2, 2 * pad, D_HEAD), F32)

    def project(r0):
        hb = _mod_norm(x_ref[pl.ds(r0, c), :], g_ref[...], mod_ref[...]).astype(BF16)
        for lo in range(0, CONV_DIM + WIDTH, n_col):
            res = jnp.dot(hb, w_ref[:, lo:lo + n_col], preferred_element_type=F32)
            if lo < CONV_DIM:
                for j in range(n_col // D_HEAD):
                    ext_ref[_ext_idx(lo // D_HEAD + j, pad + r0, c)] = (
                        res[:, j * D_HEAD:(j + 1) * D_HEAD])
            else:
                z_s[pl.ds(r0, c), lo - CONV_DIM:lo - CONV_DIM + n_col] = res
        ab = jnp.dot(hb, wab_ref[...], preferred_element_type=F32)
        ab_s[pl.ds(r0, c), :] = ab
        gl_s[pl.ds(r0, c), :] = _decay_logits(ab, alog_ref[...], dtb_ref[...])
        gt_s[:, pl.ds(r0, c)] = _decay_logits(ab.T[0:16, :], alogc_ref[...], dtbc_ref[...])

    masks = _chunk_masks(c)
    project(0)

    def step(i, carry):
        r0 = pl.multiple_of(i * c, c)
        prepared = _gdn_chunk_prepare(c, r0, masks, [request], cw_ref)
        outs, = _gdn_chunk_apply(c, r0, prepared, [request], og_ref)
        o = jnp.concatenate([o.astype(BF16) for o in outs], axis=1)
        y = jnp.dot(o, wout_ref[...], preferred_element_type=F32)
        x1 = x_ref[pl.ds(r0, c), :] + mod_ref[:, 2 * D_MODEL:3 * D_MODEL] * y
        x1_ref[pl.ds(r0, c), :] = x1
        h1_ref[pl.ds(r0, c), :] = _mod_norm(x1, gn_ref[...], modn_ref[...]).astype(BF16)
        project(pl.multiple_of(jnp.minimum(i + 1, n_chunks - 1) * c, c))
        return carry

    lax.fori_loop(0, n_chunks, step, 0)

    for slab in range(EXT_SLABS):
        last = ext_ref[_ext_idx(slab, tm, pad)]
        ext_ref[_ext_idx(slab, 0, pad)] = last
        conv_ref[:, slab * D_HEAD:(slab + 1) * D_HEAD] = last[pad - halo:]


def _gdn_layer_prompt(x, mod, mod_next, norm_g, norm_g_next, w_main, w_ab, w_out, conv_w,
                      a_log, dt_bias, out_g, tm, c):
    b, t, _ = x.shape
    const = lambda shape: pl.BlockSpec(shape, lambda i, j: (0,) * len(shape))
    resident = lambda shape: pl.BlockSpec(shape, lambda i, j: (0,) * len(shape),
                                          pipeline_mode=pl.Buffered(1))
    mod_spec = pl.BlockSpec((None, 1, 3 * D_MODEL), lambda i, j: (i, 0, 0))
    row_spec = pl.BlockSpec((None, tm, D_MODEL), lambda i, j: (i, j, 0))
    return pl.pallas_call(
        functools.partial(_gdn_layer_kernel, tm, c),
        grid=(b, t // tm),
        in_specs=[
            row_spec, mod_spec, mod_spec, const((1, D_MODEL)), const((1, D_MODEL)),
            resident((D_MODEL, CONV_DIM + WIDTH)), const((D_MODEL, LANES)),
            resident((WIDTH, D_MODEL)), const((CONV_W, CONV_DIM)),
            const((1, LANES)), const((1, LANES)), const((16, 1)), const((16, 1)),
            const((1, D_HEAD)),
        ],
        out_specs=[
            row_spec, row_spec,
            pl.BlockSpec((None, N_HEADS, D_HEAD, D_HEAD), lambda i, j: (i, 0, 0, 0)),
            pl.BlockSpec((None, CONV_W - 1, CONV_DIM), lambda i, j: (i, 0, 0)),
        ],
        out_shape=[
            jax.ShapeDtypeStruct((b, t, D_MODEL), F32),
            jax.ShapeDtypeStruct((b, t, D_MODEL), BF16),
            jax.ShapeDtypeStruct((b, N_HEADS, D_HEAD, D_HEAD), F32),
            jax.ShapeDtypeStruct((b, CONV_W - 1, CONV_DIM), F32),
        ],
        scratch_shapes=[
            pltpu.VMEM(_ext_shape(tm), F32),
            pltpu.VMEM((tm, WIDTH), F32),
            pltpu.VMEM((tm, LANES), F32),
            pltpu.VMEM((tm, LANES), F32),
            pltpu.VMEM((16, tm), F32),
        ],
        compiler_params=_cparams(("arbitrary", "arbitrary")),
        name="gdn_layer",
    )(x, mod, mod_next, norm_g.reshape(1, D_MODEL), norm_g_next.reshape(1, D_MODEL),
      w_main, w_ab, w_out, conv_w, *_head_params(a_log, dt_bias), out_g.reshape(1, D_HEAD))


def _banded_attention(dil, seq, q_ref, k_ref, v_ref, acc_ref, m_ref, l_ref):
    blk = D_HEAD
    per_step = 8
    n_blk = seq // dil // blk
    with_prev = n_blk > 1
    n_keys = 2 * blk if with_prev else blk
    row = lax.broadcasted_iota(jnp.int32, (blk, n_keys), 0)
    col = lax.broadcasted_iota(jnp.int32, (blk, n_keys), 1)
    is_prev = col < (n_keys - blk)
    slack = jnp.where(is_prev, col - row, row + (n_keys - blk) - col)
    cur_ok = slack >= 0
    ones = jnp.ones((n_keys, blk), BF16)

    def rows(start):
        if dil == 1:
            return pl.ds(start, blk)
        return pl.ds(start, blk, stride=dil)

    def body(i, carry):
        starts, masks, qs, kcs, vcs = [], [], [], [], []
        for u in range(per_step):
            idx = i * per_step + u
            r = idx // n_blk
            n = idx % n_blk
            start = r + dil * blk * n
            starts.append(start)
            qs.append(q_ref[rows(start), :].astype(BF16))
            k_cur = k_ref[rows(start), :].astype(BF16)
            v_cur = v_ref[rows(start), :].astype(BF16)
            if with_prev:
                start_p = r + dil * blk * jnp.maximum(n - 1, 0)
                first = jnp.where(n > 0, 0, 2 * blk)
                masks.append(slack - jnp.where(is_prev, first, 0) >= 0)
                k_cur = jnp.concatenate([k_ref[rows(start_p), :].astype(BF16), k_cur], axis=0)
                v_cur = jnp.concatenate([v_ref[rows(start_p), :].astype(BF16), v_cur], axis=0)
            else:
                masks.append(cur_ok)
            kcs.append(k_cur)
            vcs.append(jnp.concatenate([v_cur, ones], axis=1))
        ss = [jnp.where(mask, _dot_nt(q, kc), NEG_BIG) for q, kc, mask in zip(qs, kcs, masks)]
        ms = [jnp.max(s, axis=-1, keepdims=True) for s in ss]
        ps = [jnp.exp2(s - m).astype(BF16) for s, m in zip(ss, ms)]
        accs = [jnp.dot(p, vc, preferred_element_type=F32) for p, vc in zip(ps, vcs)]
        for start, acc, m in zip(starts, accs, ms):
            acc_ref[rows(start), :] = acc[:, 0:blk]
            l_ref[rows(start), :] = acc[:, blk:2 * blk]
            m_ref[rows(start), :] = jnp.broadcast_to(m, (blk, blk))
        return carry

    lax.fori_loop(0, dil * n_blk // per_step, body, 0)


def _attn_layer_kernel(seq, h_ref, *refs):
    n_g = len(DIL_GROUPS)
    w_refs = refs[:3 * n_g + 1]
    (o_ref, kv0_ref, kv1_ref, kv2_ref, q_s, k_s, v_s, z_s, acc_s, m_s, l_s,
     sems) = refs[3 * n_g + 1:]
    b = pl.program_id(0)
    hp = pl.program_id(1)
    kv_refs = (kv0_ref, kv1_ref, kv2_ref)

    def proj(slab):
        return jnp.dot(h_ref[...], w_refs[slab][...], preferred_element_type=F32)

    def kv_copy(g, i, kv):
        keep = kv_refs[g].shape[1]
        src = (k_s, v_s)[kv].at[g, i, pl.ds(seq - keep, keep), :]
        col = pl.multiple_of(kv * WIDTH + (2 * hp + i) * D_HEAD, D_HEAD)
        dst = kv_refs[g].at[b, :, pl.ds(col, D_HEAD)]
        return pltpu.make_async_copy(src, dst, sems.at[(g * 2 + i) * 2 + kv])

    for g in range(n_g):
        k = proj(n_g + g)
        v = proj(2 * n_g + g)
        for i in range(2):
            k_s[g, i] = k[:, i * D_HEAD:(i + 1) * D_HEAD]
            v_s[g, i] = v[:, i * D_HEAD:(i + 1) * D_HEAD]
        for i in range(2):
            kv_copy(g, i, 0).start()
            kv_copy(g, i, 1).start()
        q = proj(g) * (ATTN_SCALE * LOG2_E)
        for i in range(2):
            q_s[g, i] = q[:, i * D_HEAD:(i + 1) * D_HEAD]

    for i in range(2):
        for g, (_, dil) in enumerate(DIL_GROUPS):
            _banded_attention(dil, seq, q_s.at[g, i], k_s.at[g, i], v_s.at[g, i],
                              acc_s.at[g], m_s.at[g], l_s.at[g])
        if i == 0:
            z = proj(3 * n_g)
            for j in range(2):
                z_s[j] = z[:, j * D_HEAD:(j + 1) * D_HEAD]
        m_all = jnp.maximum(jnp.maximum(m_s[0], m_s[1]), m_s[2])
        num = jnp.zeros((seq, D_HEAD), F32)
        den = jnp.zeros((seq, D_HEAD), F32)
        for g in range(n_g):
            w = jnp.exp2(m_s[g] - m_all)
            num = num + w * acc_s[g]
            den = den + w * l_s[g]
        o_ref[:, i * D_HEAD:(i + 1) * D_HEAD] = (num / den * _silu(z_s[i])).astype(BF16)

    for g in range(n_g):
        for i in range(2):
            kv_copy(g, i, 0).wait()
            kv_copy(g, i, 1).wait()


def _attn_layer_prompt(h, w):
    b, seq, _ = h.shape
    n_pair = N_HEADS // 2
    n_g = len(DIL_GROUPS)
    pair = 2 * D_HEAD
    keeps = [min(window, seq) for window, _ in DIL_GROUPS]
    slab = lambda n: pltpu.VMEM((n, seq, D_HEAD), F32)
    w_specs = [pl.BlockSpec((D_MODEL, pair), lambda i, p, s=s: (0, s * n_pair + p))
               for s in range(3 * n_g + 1)]
    return pl.pallas_call(
        functools.partial(_attn_layer_kernel, seq),
        grid=(b, n_pair),
        in_specs=[
            pl.BlockSpec((None, seq, D_MODEL), lambda i, p: (i, 0, 0),
                         pipeline_mode=pl.Buffered(1)),
        ] + w_specs,
        out_specs=[pl.BlockSpec((None, seq, 2 * D_HEAD), lambda i, p: (i, 0, p))]
        + [pl.BlockSpec(memory_space=pl.ANY)] * n_g,
        out_shape=[jax.ShapeDtypeStruct((b, seq, WIDTH), BF16)]
        + [jax.ShapeDtypeStruct((b, keep, 2 * WIDTH), F32) for keep in keeps],
        scratch_shapes=[
            pltpu.VMEM((n_g, 2, seq, D_HEAD), F32),
            pltpu.VMEM((n_g, 2, seq, D_HEAD), F32),
            pltpu.VMEM((n_g, 2, seq, D_HEAD), F32),
            slab(2),
            slab(n_g), slab(n_g), slab(n_g),
            pltpu.SemaphoreType.DMA((n_g * 2 * 2,)),
        ],
        compiler_params=_cparams(("arbitrary", "arbitrary")),
        name="attn_layer",
    )(h, *([w] * (3 * n_g + 1)))


def _attn_sample_kernel(n_new, *refs):
    q_ref, kv0_ref, kv1_ref, kv2_ref, z_ref, c0_ref, c1_ref, c2_ref, o_ref = refs
    n_t = KEYS_PER_QUERY
    ones = jnp.ones((D_HEAD, D_HEAD), BF16)
    dil1 = DIL_GROUPS[1][1]

    def key_tiles(g, l, kv):
        if g == 0:
            return jnp.concatenate([c0_ref[l:, kv], kv0_ref[0:l + 1, kv]], axis=0)
        if g == 1:
            a, r = divmod(l, dil1)
            parts = [c1_ref[a:, r, kv]]
            if a:
                parts.append(kv1_ref[r:r + 1, kv])
            parts.append(kv1_ref[l:l + 1, kv])
            return jnp.concatenate(parts, axis=0)
        return jnp.concatenate([c2_ref[:, l, kv], kv2_ref[l:l + 1, kv]], axis=0)

    for l in range(n_new):
        m_g, l_g, acc_g = [], [], []
        for g in range(len(DIL_GROUPS)):
            q = q_ref[l, g] * (ATTN_SCALE * LOG2_E)
            prod = (key_tiles(g, l, 0) * q[None]).reshape(n_t * N_HEADS, D_HEAD)
            s = jnp.dot(prod.astype(BF16), ones, preferred_element_type=F32)
            s = s.reshape(n_t, N_HEADS, D_HEAD)
            m = jnp.max(s, axis=0)
            p = jnp.exp2(s - m[None])
            m_g.append(m)
            l_g.append(jnp.sum(p, axis=0))
            acc_g.append(jnp.sum(p * key_tiles(g, l, 1), axis=0))
        m_all = jnp.maximum(jnp.maximum(m_g[0], m_g[1]), m_g[2])
        w_g = [jnp.exp2(m - m_all) for m in m_g]
        num = w_g[0] * acc_g[0] + w_g[1] * acc_g[1] + w_g[2] * acc_g[2]
        den = w_g[0] * l_g[0] + w_g[1] * l_g[1] + w_g[2] * l_g[2]
        o_ref[l] = num / den * _silu(z_ref[l])


def _attn_sample(q, kvs, z, caches):
    b, n_new = q.shape[:2]
    assert n_new == SUBLANES
    (w0, d0), (w1, d1), (w2, d2) = DIL_GROUPS
    assert caches[0].shape[1] == w0 and caches[1].shape[1] == w1 and caches[2].shape[1] == w2
    assert d0 == 1 and n_new % d1 == 0 and d2 == 2 * n_new
    assert w0 // d0 + 1 == KEYS_PER_QUERY and w1 // d1 + 1 == KEYS_PER_QUERY
    assert w2 // d2 + 1 == KEYS_PER_QUERY
    c1 = caches[1].reshape(b, w1 // d1, d1, 2, N_HEADS, D_HEAD)
    c2 = caches[2].reshape(b, w2 // d2, d2, 2, N_HEADS, D_HEAD)
    tail = (2, N_HEADS, D_HEAD)
    new_spec = pl.BlockSpec((None, n_new) + tail, lambda i: (i, 0, 0, 0, 0))
    return pl.pallas_call(
        functools.partial(_attn_sample_kernel, n_new),
        grid=(b,),
        in_specs=[
            pl.BlockSpec((None, n_new, 3, N_HEADS, D_HEAD), lambda i: (i, 0, 0, 0, 0)),
            new_spec, new_spec, new_spec,
            pl.BlockSpec((None, n_new, N_HEADS, D_HEAD), lambda i: (i, 0, 0, 0)),
            pl.BlockSpec((None, w0) + tail, lambda i: (i, 0, 0, 0, 0)),
            pl.BlockSpec((None, w1 // d1, d1) + tail, lambda i: (i, 0, 0, 0, 0, 0)),
            pl.BlockSpec((None, w2 // d2, n_new) + tail, lambda i: (i, 0, 0, 0, 0, 0)),
        ],
        out_specs=pl.BlockSpec((None, n_new, N_HEADS, D_HEAD), lambda i: (i, 0, 0, 0)),
        out_shape=jax.ShapeDtypeStruct((b, n_new, N_HEADS, D_HEAD), F32),
        compiler_params=_cparams(("arbitrary",)),
        name="attn_sample",
    )(q, kvs[0], kvs[1], kvs[2], z, caches[0], c1, c2)


def kernel(x_prompt, x_sample, state_delta, state_conv, cache_kv_w128, cache_kv_w512, cache_kv_w2048,
           c_prompt, c_sample, norm_g, ada_w, ada_b, a_w_in, a_conv_w, a_A_log, a_dt_bias,
           a_out_norm_g, a_w_out, b_w_in, b_w_out, final_norm_g):
    bp, seq, _ = x_prompt.shape
    bs, n_new, _ = x_sample.shape
    n_s = bs * n_new

    mod = _ada_mod(jnp.concatenate([c_prompt, c_sample], axis=0), ada_w, ada_b)
    mod_p = [mod[l, :bp].reshape(bp, 1, 3 * D_MODEL) for l in range(2)]
    mod_s = [jnp.repeat(mod[l, bp:], n_new, axis=0).reshape(1, n_s, 3 * D_MODEL) for l in range(2)]

    w_main = a_w_in[0].astype(BF16)
    w_ab = jnp.pad(w_main[:, CONV_DIM + WIDTH:], ((0, 0), (0, LANES - 2 * N_HEADS)))
    w_out_a = a_w_out[0].astype(BF16)
    xs_flat = x_sample.reshape(1, n_s, D_MODEL)

    x1_p, h1_p, delta_p, conv_p = _gdn_layer_prompt(
        x_prompt, mod_p[0], mod_p[1], norm_g[0], norm_g[1], w_main, w_ab, w_out_a,
        a_conv_w[0], a_A_log[0], a_dt_bias[0], a_out_norm_g[0], 1024, 128)

    qkv_s, z_s, ab_s, abt_s = _project(xs_flat, mod_s[0], norm_g[0], w_main, (3, 1), n_s, w_ab)
    abt_s = abt_s.reshape(16, bs, n_new).transpose(1, 0, 2)
    o_s, delta_s, conv_s = _gdn_core_decode(
        qkv_s.reshape(bs, n_new, CONV_DIM), z_s.reshape(bs, n_new, WIDTH),
        ab_s.reshape(bs, n_new, LANES), abt_s, a_conv_w[0], a_A_log[0], a_dt_bias[0],
        a_out_norm_g[0], state_delta[0], state_conv[0], 4)
    x1_s = _out_project(xs_flat, o_s.reshape(1, n_s, WIDTH), mod_s[0], w_out_a, final_norm_g,
                        n_s, False)

    n_g = len(DIL_GROUPS)
    w_b = b_w_in[0].astype(BF16)
    w_out_b = b_w_out[0].astype(BF16)

    ob_p, kv0_p, kv1_p, kv2_p = _attn_layer_prompt(h1_p, w_b)
    y_p = _out_project(x1_p, ob_p, mod_p[1], w_out_b, final_norm_g, 512, True)

    q_s, k_s, v_s, zb_s = _project(x1_s, mod_s[1], norm_g[1], w_b, (n_g, n_g, n_g, 1), n_s)
    k_s = k_s.reshape(bs, n_new, n_g, N_HEADS, D_HEAD)
    v_s = v_s.reshape(bs, n_new, n_g, N_HEADS, D_HEAD)
    kvn = [jnp.stack([k_s[:, :, g], v_s[:, :, g]], axis=2) for g in range(n_g)]
    ob_s = _attn_sample(q_s.reshape(bs, n_new, n_g, N_HEADS, D_HEAD), kvn,
                        zb_s.reshape(bs, n_new, N_HEADS, D_HEAD),
                        (cache_kv_w128[0], cache_kv_w512[0], cache_kv_w2048[0]))
    y_s = _out_project(x1_s, ob_s.reshape(1, n_s, WIDTH), mod_s[1], w_out_b, final_norm_g,
                       n_s, True)

    def kv_prompt(kv):
        return kv.reshape(1, bp, kv.shape[1], 2, N_HEADS, D_HEAD)

    return (y_p, y_s.reshape(bs, n_new, D_MODEL),
            delta_p[None], delta_s[None], conv_p[None], conv_s[None],
            kv_prompt(kv0_p), kvn[0][None], kv_prompt(kv1_p), kvn[1][None],
            kv_prompt(kv2_p), kvn[2][None])
```

```python
import functools

import jax
import jax.numpy as jnp
from jax import lax
from jax.experimental import pallas as pl
from jax.experimental.pallas import tpu as pltpu

F32 = jnp.float32
BF16 = jnp.bfloat16

D_MODEL = 1024
N_HEADS = 8
D_HEAD = 128
WIDTH = N_HEADS * D_HEAD
CONV_W = 4
CONV_DIM = 3 * WIDTH
DIL_GROUPS = ((128, 1), (512, 4), (2048, 16))
KEYS_PER_QUERY = 129
EPS = 1e-6
ATTN_SCALE = D_HEAD ** -0.5
LOG2_E = 1.4426950408889634
NEG_BIG = -1e30

LANES = 128
SUBLANES = 8
VMEM_LIMIT_BYTES = 56 * 1024 * 1024


def _cparams(semantics):
    return pltpu.CompilerParams(dimension_semantics=semantics,
                                vmem_limit_bytes=VMEM_LIMIT_BYTES)


def _silu(x):
    return x * (1.0 / (1.0 + jnp.exp(-x)))


def _softplus(x):
    return jnp.maximum(x, 0.0) + jnp.log1p(jnp.exp(-jnp.abs(x)))


def _dot(a, b):
    return jnp.dot(a.astype(BF16), b.astype(BF16), preferred_element_type=F32)


def _dot_nt(a, b):
    return lax.dot_general(a.astype(BF16), b.astype(BF16), (((1,), (1,)), ((), ())),
                           preferred_element_type=F32)


def _dot_tn(a, b):
    return lax.dot_general(a.astype(BF16), b.astype(BF16), (((0,), (0,)), ((), ())),
                           preferred_element_type=F32)


def _split3(x):
    x1 = x.astype(BF16)
    r = x - x1.astype(F32)
    x2 = r.astype(BF16)
    x3 = (r - x2.astype(F32)).astype(BF16)
    return x1, x2, x3


def _dot_sel(sel, x):
    s = sel.astype(BF16)
    out = None
    for p in _split3(x):
        t = jnp.dot(s, p, preferred_element_type=F32)
        out = t if out is None else out + t
    return out


def _dot_x_sel(x, sel):
    s = sel.astype(BF16)
    out = None
    for p in _split3(x):
        t = jnp.dot(p, s, preferred_element_type=F32)
        out = t if out is None else out + t
    return out


def _mod_kernel(c_ref, w_ref, b_ref, o_ref):
    s = _silu(c_ref[...])
    o_ref[...] = _dot(s, w_ref[...]) + b_ref[...]


def _ada_mod(c_all, ada_w, ada_b):
    n_layers = ada_w.shape[0]
    rows = c_all.shape[0]
    tn = 768
    return pl.pallas_call(
        _mod_kernel,
        grid=(n_layers, 3 * D_MODEL // tn),
        in_specs=[
            pl.BlockSpec((rows, D_MODEL), lambda l, j: (0, 0)),
            pl.BlockSpec((None, D_MODEL, tn), lambda l, j: (l, 0, j)),
            pl.BlockSpec((None, 1, tn), lambda l, j: (l, 0, j)),
        ],
        out_specs=pl.BlockSpec((None, rows, tn), lambda l, j: (l, 0, j)),
        out_shape=jax.ShapeDtypeStruct((n_layers, rows, 3 * D_MODEL), F32),
        compiler_params=_cparams(("arbitrary", "arbitrary")),
        name="ada_mod",
    )(c_all, ada_w, ada_b.reshape(n_layers, 1, 3 * D_MODEL))


def _proj_kernel(seg_bounds, has_ab, *refs):
    x_ref, mod_ref, g_ref, w_ref = refs[:4]
    pos = 4
    if has_ab:
        wab_ref = refs[4]
        pos = 5
    n_seg = len(seg_bounds)
    seg_refs = refs[pos:pos + n_seg]
    pos += n_seg
    if has_ab:
        ab_ref, abt_ref = refs[pos:pos + 2]
        pos += 2
    h_ref = refs[pos]
    j = pl.program_id(2)

    @pl.when(j == 0)
    def _():
        hb = _mod_norm(x_ref[...], g_ref[...], mod_ref[...]).astype(BF16)
        h_ref[...] = hb
        if has_ab:
            ab = jnp.dot(hb, wab_ref[...], preferred_element_type=F32)
            ab_ref[...] = ab
            abt_ref[...] = ab.T[0:16, :]

    res = jnp.dot(h_ref[...], w_ref[...], preferred_element_type=F32)
    for (lo, hi), o_ref in zip(seg_bounds, seg_refs):
        @pl.when((j >= lo) & (j < hi))
        def _(o_ref=o_ref):
            o_ref[...] = res


def _project(x, mod, norm_g, w, seg_slabs, tm, w_ab=None):
    n, t, _ = x.shape
    r = mod.shape[1]
    slab = 1024
    n_slab = w.shape[1] // slab
    assert sum(seg_slabs) == n_slab and t % tm == 0
    has_ab = w_ab is not None
    bounds, lo = [], 0
    for s in seg_slabs:
        bounds.append((lo, lo + s))
        lo += s
    mod_rows = 1 if r == 1 else tm
    mod_map = (lambda b, i, j: (b, 0, 0)) if r == 1 else (lambda b, i, j: (b, i, 0))
    in_specs = [
        pl.BlockSpec((None, tm, D_MODEL), lambda b, i, j: (b, i, 0)),
        pl.BlockSpec((None, mod_rows, 3 * D_MODEL), mod_map),
        pl.BlockSpec((1, D_MODEL), lambda b, i, j: (0, 0)),
        pl.BlockSpec((D_MODEL, slab), lambda b, i, j: (0, j)),
    ]
    args = [x, mod, norm_g.reshape(1, D_MODEL), w]
    if has_ab:
        in_specs.append(pl.BlockSpec((D_MODEL, LANES), lambda b, i, j: (0, 0)))
        args.append(w_ab)
    out_specs, out_shapes = [], []
    for (lo, hi) in bounds:
        def seg_map(b, i, j, lo=lo, hi=hi):
            return (b, i, jnp.clip(j - lo, 0, hi - lo - 1))
        out_specs.append(pl.BlockSpec((None, tm, slab), seg_map))
        out_shapes.append(jax.ShapeDtypeStruct((n, t, slab * (hi - lo)), F32))
    if has_ab:
        out_specs += [pl.BlockSpec((None, tm, LANES), lambda b, i, j: (b, i, 0)),
                      pl.BlockSpec((None, 16, tm), lambda b, i, j: (b, 0, i))]
        out_shapes += [jax.ShapeDtypeStruct((n, t, LANES), F32),
                       jax.ShapeDtypeStruct((n, 16, t), F32)]
    return pl.pallas_call(
        functools.partial(_proj_kernel, tuple(bounds), has_ab),
        grid=(n, t // tm, n_slab),
        in_specs=in_specs,
        out_specs=out_specs,
        out_shape=out_shapes,
        scratch_shapes=[pltpu.VMEM((tm, D_MODEL), BF16)],
        compiler_params=_cparams(("arbitrary", "arbitrary", "arbitrary")),
        name="norm_proj",
    )(*args)


def _mod_norm(x, g, mod):
    ms = jnp.mean(x * x, axis=-1, keepdims=True)
    y = x * lax.rsqrt(ms + EPS) * g
    return y * (1.0 + mod[:, D_MODEL:2 * D_MODEL]) + mod[:, 0:D_MODEL]


def _out_kernel(final_norm, x_ref, o_ref, mod_ref, w_ref, fg_ref, out_ref):
    y = _dot(o_ref[...], w_ref[...])
    gate = mod_ref[:, 2 * D_MODEL:3 * D_MODEL]
    x = x_ref[...] + gate * y
    if final_norm:
        ms = jnp.mean(x * x, axis=-1, keepdims=True)
        x = x * lax.rsqrt(ms + EPS) * fg_ref[...]
    out_ref[...] = x


def _out_project(x, o, mod, w_out, final_g, tm, final_norm):
    n, t, _ = x.shape
    r = mod.shape[1]
    mod_rows = 1 if r == 1 else tm
    mod_map = (lambda b, i: (b, 0, 0)) if r == 1 else (lambda b, i: (b, i, 0))
    row_spec = pl.BlockSpec((None, tm, D_MODEL), lambda b, i: (b, i, 0))
    return pl.pallas_call(
        functools.partial(_out_kernel, final_norm),
        grid=(n, t // tm),
        in_specs=[
            row_spec,
            pl.BlockSpec((None, tm, WIDTH), lambda b, i: (b, i, 0)),
            pl.BlockSpec((None, mod_rows, 3 * D_MODEL), mod_map),
            pl.BlockSpec((WIDTH, D_MODEL), lambda b, i: (0, 0)),
            pl.BlockSpec((1, D_MODEL), lambda b, i: (0, 0)),
        ],
        out_specs=row_spec,
        out_shape=jax.ShapeDtypeStruct((n, t, D_MODEL), F32),
        compiler_params=_cparams(("arbitrary", "arbitrary")),
        name="out_proj",
    )(x, o, mod, w_out, final_g.reshape(1, D_MODEL))


def _unit_lower_inverses(n_mats, c):
    row = lax.broadcasted_iota(jnp.int32, (c, c), 0)
    col = lax.broadcasted_iota(jnp.int32, (c, c), 1)
    base = min(16, c)
    blk_id = lambda v, size: jnp.right_shift(v, size.bit_length() - 1)
    eye = (row == col).astype(F32)
    same = blk_id(row, base) == blk_id(col, base)
    ps = [jnp.where(same, -n, 0.0) for n in n_mats]
    invs = [eye + p for p in ps]
    assert base >= 4
    ps = [_dot(p, p).astype(BF16) for p in ps]
    power = 2
    while 2 * power < base:
        if c % LANES == 0:
            wide = [_dot(p, jnp.concatenate([p, inv.astype(BF16)], axis=1))
                    for p, inv in zip(ps, invs)]
            invs = [inv + w[:, c:] for inv, w in zip(invs, wide)]
            ps = [w[:, 0:c].astype(BF16) for w in wide]
        else:
            invs = [inv + _dot(p, inv) for p, inv in zip(ps, invs)]
            ps = [_dot(p, p).astype(BF16) for p in ps]
        power *= 2
    invs = [inv + _dot(p, inv) for p, inv in zip(ps, invs)]
    size = base
    while size < c:
        same_next = blk_id(row, 2 * size) == blk_id(col, 2 * size)
        off = same_next & jnp.logical_not(same)
        ls = [jnp.where(off, n, 0.0).astype(BF16) for n in n_mats]
        invs_b = [inv.astype(BF16) for inv in invs]
        ts = [_dot(ib, l) for ib, l in zip(invs_b, ls)]
        invs = [inv - _dot(t, ib) for inv, t, ib in zip(invs, ts, invs_b)]
        same = same_next
        size *= 2
    return invs


def _aligned(v, m):
    return v if isinstance(v, int) else pl.multiple_of(v, m)


EXT_SLABS = CONV_DIM // D_HEAD
EXT_LEAD = SUBLANES


def _ext_idx(slab, start, n):
    return (slab // 2, pl.ds(2 * start + slab % 2, n, stride=2), slice(None))


def _ext_shape(rows):
    return (EXT_SLABS // 2, 2 * (rows + EXT_LEAD), D_HEAD)


def _chunk_masks(c):
    row = lax.broadcasted_iota(jnp.int32, (c, c), 0)
    col = lax.broadcasted_iota(jnp.int32, (c, c), 1)
    tril_sel = jnp.where(row >= col, 1.0, 0.0).astype(BF16)
    triu_sel = jnp.where(row <= col, 1.0, 0.0).astype(BF16)
    return row >= col, row > col, tril_sel, triu_sel


def _problems(n_req):
    return [(i, h) for i in range(n_req) for h in range(N_HEADS)]


def _gdn_chunk_prepare(c, r0, masks, requests, cw_ref):
    tril, strict, tril_sel, triu_sel = masks
    pad = EXT_LEAD
    halo = CONV_W - 1
    heads = _problems(len(requests))
    gc_alls, gct_alls, beta_alls = [], [], []
    for _, _, ab_ref, g_ref, gt_ref, _ in requests:
        gc_alls.append(_dot_sel(tril_sel, g_ref[pl.ds(r0, c), :]))
        gct_alls.append(_dot_x_sel(gt_ref[0:N_HEADS, pl.ds(r0, c)], triu_sel))
        beta_alls.append(1.0 / (1.0 + jnp.exp(-ab_ref[pl.ds(r0, c), :])))

    def conv(off, i, h):
        slab = off // D_HEAD + h
        cols = slice(slab * D_HEAD, (slab + 1) * D_HEAD)
        acc = None
        for tap in range(CONV_W):
            rows = _ext_idx(slab, r0 + pad - halo + tap, c)
            term = requests[i][0][rows] * cw_ref[tap:tap + 1, cols]
            acc = term if acc is None else acc + term
        return _silu(acc)

    qs, ks, vs = [], [], []
    for i, h in heads:
        q = conv(0, i, h)
        k = conv(WIDTH, i, h)
        qs.append(q * lax.rsqrt(jnp.sum(q * q, axis=-1, keepdims=True) + EPS)
                  * (D_HEAD ** -0.5))
        ks.append(k * lax.rsqrt(jnp.sum(k * k, axis=-1, keepdims=True) + EPS))
        vs.append(conv(2 * WIDTH, i, h))
    gcs = [jnp.broadcast_to(gc_alls[i][:, h:h + 1], (c, D_HEAD)) for i, h in heads]
    betas = [jnp.broadcast_to(beta_alls[i][:, N_HEADS + h:N_HEADS + h + 1], (c, D_HEAD))
             for i, h in heads]
    g_lasts = [gc[c - 1:c, :] for gc in gcs]
    decays = [jnp.where(tril, jnp.exp2(gc[:, 0:c] - gct_alls[i][h:h + 1, :]), 0.0)
              for gc, (i, h) in zip(gcs, heads)]
    kbs = [k * beta for k, beta in zip(ks, betas)]
    ks_b = [k.astype(BF16) for k in ks]
    kks = [_dot_nt(kb, k) for kb, k in zip(kbs, ks_b)]
    qks = [_dot_nt(q, k) for q, k in zip(qs, ks_b)]
    n_mats = [jnp.where(strict, kk * decay, 0.0) for kk, decay in zip(kks, decays)]
    qks = [(qk * decay).astype(BF16) for qk, decay in zip(qks, decays)]
    invs = _unit_lower_inverses(n_mats, c)
    e_gcs = [jnp.exp2(gc) for gc in gcs]
    rhss = [jnp.concatenate([v * beta, kb * e_gc], axis=1).astype(BF16)
            for v, beta, kb, e_gc in zip(vs, betas, kbs, e_gcs)]
    sols = [_dot(inv, rhs) for inv, rhs in zip(invs, rhss)]
    q_decs = [(q * e_gc).astype(BF16) for q, e_gc in zip(qs, e_gcs)]
    k_decs = [(k * jnp.exp2(g_last - gc)).astype(BF16)
              for k, g_last, gc in zip(ks, g_lasts, gcs)]
    return [(sol[:, 0:D_HEAD], sol[:, D_HEAD:2 * D_HEAD].astype(BF16), qk, q_dec, k_dec,
             jnp.exp2(g_last))
            for sol, qk, q_dec, k_dec, g_last in zip(sols, qks, q_decs, k_decs, g_lasts)]


def _gdn_chunk_apply(c, r0, prepared, requests, og_ref):
    n_req = len(requests)
    heads = _problems(n_req)
    s_olds = [requests[i][5][h] for i, h in heads]
    s_bs = [s.astype(BF16) for s in s_olds]
    on_s = [_dot(jnp.concatenate([k_cum, q_dec], axis=0), s_b)
            for (_, k_cum, _, q_dec, _, _), s_b in zip(prepared, s_bs)]
    us = [w - ks[0:c] for (w, _, _, _, _, _), ks in zip(prepared, on_s)]
    us_b = [u.astype(BF16) for u in us]
    os_ = [ks[c:2 * c] + _dot(qk, u_b)
           for (_, _, qk, _, _, _), ks, u_b in zip(prepared, on_s, us_b)]
    for n, (i, h) in enumerate(heads):
        k_dec, s_decay = prepared[n][4], prepared[n][5]
        requests[i][5][h] = s_olds[n] * s_decay + _dot_tn(k_dec, us_b[n])
    outs = [[] for _ in range(n_req)]
    for n, (i, h) in enumerate(heads):
        o = os_[n]
        o = o * lax.rsqrt(jnp.mean(o * o, axis=-1, keepdims=True) + EPS) * og_ref[...]
        z = requests[i][1][pl.ds(r0, c), h * D_HEAD:(h + 1) * D_HEAD]
        outs[i].append(o * _silu(z))
    return outs


def _decay_logits(ab, alog, dtb):
    return (-LOG2_E * jnp.exp(alog)) * _softplus(ab + dtb)


def _gdn_decode_kernel(c, n_req, qkv_ref, z_ref, ab_ref, abt_ref, cw_ref, alog_ref, dtb_ref,
                       alogc_ref, dtbc_ref, og_ref, s0_ref, conv0_ref,
                       o_ref, s_ref, conv_ref, ext_ref, g_ref, gt_ref):
    halo = CONV_W - 1
    s_ref[...] = s0_ref[...]
    lead_fill = jnp.zeros((EXT_LEAD - halo, D_HEAD), F32)
    for i in range(n_req):
        for slab in range(EXT_SLABS):
            cols = slice(slab * D_HEAD, (slab + 1) * D_HEAD)
            ext_ref[(i,) + _ext_idx(slab, 0, EXT_LEAD)] = jnp.concatenate(
                [lead_fill, conv0_ref[i, :, cols]], axis=0)
            ext_ref[(i,) + _ext_idx(slab, EXT_LEAD, c)] = qkv_ref[i, :, cols]
    g_ref[...] = _decay_logits(ab_ref[...], alog_ref[...], dtb_ref[...])
    gt_ref[...] = _decay_logits(abt_ref[...], alogc_ref[...], dtbc_ref[...])
    requests = [(ext_ref.at[i], z_ref.at[i], ab_ref.at[i], g_ref.at[i], gt_ref.at[i], s_ref.at[i])
                for i in range(n_req)]
    prepared = _gdn_chunk_prepare(c, 0, _chunk_masks(c), requests, cw_ref)
    outs = _gdn_chunk_apply(c, 0, prepared, requests, og_ref)
    for i in range(n_req):
        for h, o in enumerate(outs[i]):
            o_ref[i, :, h * D_HEAD:(h + 1) * D_HEAD] = o
        for slab in range(EXT_SLABS):
            last = ext_ref[(i,) + _ext_idx(slab, c, EXT_LEAD)]
            conv_ref[i, :, slab * D_HEAD:(slab + 1) * D_HEAD] = last[EXT_LEAD - halo:]


def _gdn_core_decode(qkv, z, ab, abt, conv_w, a_log, dt_bias, out_g, s0, conv0, n_req):
    b, c, _ = qkv.shape
    assert b % n_req == 0
    const = lambda shape: pl.BlockSpec(shape, lambda i: (0,) * len(shape))
    per_req = lambda *tail: pl.BlockSpec((n_req,) + tail, lambda i: (i,) + (0,) * len(tail))
    return pl.pallas_call(
        functools.partial(_gdn_decode_kernel, c, n_req),
        grid=(b // n_req,),
        in_specs=[
            per_req(c, CONV_DIM), per_req(c, WIDTH), per_req(c, LANES), per_req(16, c),
            const((CONV_W, CONV_DIM)), const((1, LANES)), const((1, LANES)),
            const((16, 1)), const((16, 1)), const((1, D_HEAD)),
            per_req(N_HEADS, D_HEAD, D_HEAD), per_req(CONV_W - 1, CONV_DIM),
        ],
        out_specs=[per_req(c, WIDTH), per_req(N_HEADS, D_HEAD, D_HEAD),
                   per_req(CONV_W - 1, CONV_DIM)],
        out_shape=[
            jax.ShapeDtypeStruct((b, c, WIDTH), F32),
            jax.ShapeDtypeStruct((b, N_HEADS, D_HEAD, D_HEAD), F32),
            jax.ShapeDtypeStruct((b, CONV_W - 1, CONV_DIM), F32),
        ],
        scratch_shapes=[
            pltpu.VMEM((n_req,) + _ext_shape(c), F32),
            pltpu.VMEM((n_req, c, LANES), F32),
            pltpu.VMEM((n_req, 16, c), F32),
        ],
        compiler_params=_cparams(("arbitrary",)),
        name="gdn_core",
    )(qkv, z, ab, abt, conv_w, *_head_params(a_log, dt_bias), out_g.reshape(1, D_HEAD),
      s0, conv0)


def _head_params(a_log, dt_bias):
    row = lambda v: jnp.pad(v.reshape(1, N_HEADS), ((0, 0), (0, LANES - N_HEADS)))
    col = lambda v: jnp.pad(v.reshape(N_HEADS, 1), ((0, 16 - N_HEADS), (0, 0)))
    return row(a_log), row(dt_bias), col(a_log), col(dt_bias)


def _gdn_layer_kernel(tm, c, x_ref, mod_ref, modn_ref, g_ref, gn_ref, w_ref, wab_ref,
                      wout_ref, cw_ref, alog_ref, dtb_ref, alogc_ref, dtbc_ref, og_ref,
                      x1_ref, h1_ref, s_ref, conv_ref, ext_ref, z_s, ab_s, gl_s, gt_s):
    t = pl.program_id(1)
    pad = EXT_LEAD
    halo = CONV_W - 1
    n_chunks = tm // c
    n_col = 512
    request = (ext_ref, z_s, ab_s, gl_s, gt_s, s_ref)

    @pl.when(t == 0)
    def _():
        s_ref[...] = jnp.zeros_like(s_ref)
        ext_ref[:, 0:2 * pad, :] = jnp.zeros((EXT_SLABS // 2, 2 * pad, D_HEAD), F32)

    def project(r0):
        hb = _mod_norm(x_ref[pl.ds(r0, c), :], g_ref[...], mod_ref[...]).astype(BF16)
        for lo in range(0, CONV_DIM + WIDTH, n_col):
            res = jnp.dot(hb, w_ref[:, lo:lo + n_col], preferred_element_type=F32)
            if lo < CONV_DIM:
                for j in range(n_col // D_HEAD):
                    ext_ref[_ext_idx(lo // D_HEAD + j, pad + r0, c)] = (
                        res[:, j * D_HEAD:(j + 1) * D_HEAD])
            else:
                z_s[pl.ds(r0, c), lo - CONV_DIM:lo - CONV_DIM + n_col] = res
        ab = jnp.dot(hb, wab_ref[...], preferred_element_type=F32)
        ab_s[pl.ds(r0, c), :] = ab
        gl_s[pl.ds(r0, c), :] = _decay_logits(ab, alog_ref[...], dtb_ref[...])
        gt_s[:, pl.ds(r0, c)] = _decay_logits(ab.T[0:16, :], alogc_ref[...], dtbc_ref[...])

    masks = _chunk_masks(c)
    project(0)

    def step(i, carry):
        r0 = pl.multiple_of(i * c, c)
        prepared = _gdn_chunk_prepare(c, r0, masks, [request], cw_ref)
        outs, = _gdn_chunk_apply(c, r0, prepared, [request], og_ref)
        o = jnp.concatenate([o.astype(BF16) for o in outs], axis=1)
        y = jnp.dot(o, wout_ref[...], preferred_element_type=F32)
        x1 = x_ref[pl.ds(r0, c), :] + mod_ref[:, 2 * D_MODEL:3 * D_MODEL] * y
        x1_ref[pl.ds(r0, c), :] = x1
        h1_ref[pl.ds(r0, c), :] = _mod_norm(x1, gn_ref[...], modn_ref[...]).astype(BF16)
        project(pl.multiple_of(jnp.minimum(i + 1, n_chunks - 1) * c, c))
        return carry

    lax.fori_loop(0, n_chunks, step, 0)

    for slab in range(EXT_SLABS):
        last = ext_ref[_ext_idx(slab, tm, pad)]
        ext_ref[_ext_idx(slab, 0, pad)] = last
        conv_ref[:, slab * D_HEAD:(slab + 1) * D_HEAD] = last[pad - halo:]


def _gdn_layer_prompt(x, mod, mod_next, norm_g, norm_g_next, w_main, w_ab, w_out, conv_w,
                      a_log, dt_bias, out_g, tm, c):
    b, t, _ = x.shape
    const = lambda shape: pl.BlockSpec(shape, lambda i, j: (0,) * len(shape))
    resident = lambda shape: pl.BlockSpec(shape, lambda i, j: (0,) * len(shape),
                                          pipeline_mode=pl.Buffered(1))
    mod_spec = pl.BlockSpec((None, 1, 3 * D_MODEL), lambda i, j: (i, 0, 0))
    row_spec = pl.BlockSpec((None, tm, D_MODEL), lambda i, j: (i, j, 0))
    return pl.pallas_call(
        functools.partial(_gdn_layer_kernel, tm, c),
        grid=(b, t // tm),
        in_specs=[
            row_spec, mod_spec, mod_spec, const((1, D_MODEL)), const((1, D_MODEL)),
            resident((D_MODEL, CONV_DIM + WIDTH)), const((D_MODEL, LANES)),
            resident((WIDTH, D_MODEL)), const((CONV_W, CONV_DIM)),
            const((1, LANES)), const((1, LANES)), const((16, 1)), const((16, 1)),
            const((1, D_HEAD)),
        ],
        out_specs=[
            row_spec, row_spec,
            pl.BlockSpec((None, N_HEADS, D_HEAD, D_HEAD), lambda i, j: (i, 0, 0, 0)),
            pl.BlockSpec((None, CONV_W - 1, CONV_DIM), lambda i, j: (i, 0, 0)),
        ],
        out_shape=[
            jax.ShapeDtypeStruct((b, t, D_MODEL), F32),
            jax.ShapeDtypeStruct((b, t, D_MODEL), BF16),
            jax.ShapeDtypeStruct((b, N_HEADS, D_HEAD, D_HEAD), F32),
            jax.ShapeDtypeStruct((b, CONV_W - 1, CONV_DIM), F32),
        ],
        scratch_shapes=[
            pltpu.VMEM(_ext_shape(tm), F32),
            pltpu.VMEM((tm, WIDTH), F32),
            pltpu.VMEM((tm, LANES), F32),
            pltpu.VMEM((tm, LANES), F32),
            pltpu.VMEM((16, tm), F32),
        ],
        compiler_params=_cparams(("arbitrary", "arbitrary")),
        name="gdn_layer",
    )(x, mod, mod_next, norm_g.reshape(1, D_MODEL), norm_g_next.reshape(1, D_MODEL),
      w_main, w_ab, w_out, conv_w, *_head_params(a_log, dt_bias), out_g.reshape(1, D_HEAD))


def _banded_attention(dil, seq, q_ref, k_ref, v_ref, acc_ref, m_ref, l_ref):
    blk = D_HEAD
    per_step = 8
    n_blk = seq // dil // blk
    with_prev = n_blk > 1
    n_keys = 2 * blk if with_prev else blk
    row = lax.broadcasted_iota(jnp.int32, (blk, n_keys), 0)
    col = lax.broadcasted_iota(jnp.int32, (blk, n_keys), 1)
    is_prev = col < (n_keys - blk)
    slack = jnp.where(is_prev, col - row, row + (n_keys - blk) - col)
    cur_ok = slack >= 0
    ones = jnp.ones((n_keys, blk), BF16)

    def rows(start):
        if dil == 1:
            return pl.ds(start, blk)
        return pl.ds(start, blk, stride=dil)

    def body(i, carry):
        starts, masks, qs, kcs, vcs = [], [], [], [], []
        for u in range(per_step):
            idx = i * per_step + u
            r = idx // n_blk
            n = idx % n_blk
            start = r + dil * blk * n
            starts.append(start)
            qs.append(q_ref[rows(start), :].astype(BF16))
            k_cur = k_ref[rows(start), :].astype(BF16)
            v_cur = v_ref[rows(start), :].astype(BF16)
            if with_prev:
                if u % min(n_blk, per_step) != 0:
                    k_prev, v_prev = k_last, v_last
                else:
                    start_p = r + dil * blk * jnp.maximum(n - 1, 0)
                    k_prev = k_ref[rows(start_p), :].astype(BF16)
                    v_prev = v_ref[rows(start_p), :].astype(BF16)
                k_last, v_last = k_cur, v_cur
                first = jnp.where(n > 0, 0, 2 * blk)
                masks.append(slack - jnp.where(is_prev, first, 0) >= 0)
                k_cur = jnp.concatenate([k_prev, k_cur], axis=0)
                v_cur = jnp.concatenate([v_prev, v_cur], axis=0)
            else:
                masks.append(cur_ok)
            kcs.append(k_cur)
            vcs.append(jnp.concatenate([v_cur, ones], axis=1))
        ss = [jnp.where(mask, _dot_nt(q, kc), NEG_BIG) for q, kc, mask in zip(qs, kcs, masks)]
        ms = [jnp.max(s, axis=-1, keepdims=True) for s in ss]
        ps = [jnp.exp2(s - m).astype(BF16) for s, m in zip(ss, ms)]
        accs = [jnp.dot(p, vc, preferred_element_type=F32) for p, vc in zip(ps, vcs)]
        for start, acc, m in zip(starts, accs, ms):
            acc_ref[rows(start), :] = acc[:, 0:blk]
            l_ref[rows(start), :] = acc[:, blk:2 * blk]
            m_ref[rows(start), :] = jnp.broadcast_to(m, (blk, blk))
        return carry

    lax.fori_loop(0, dil * n_blk // per_step, body, 0)


def _attn_layer_kernel(seq, h_ref, *refs):
    n_g = len(DIL_GROUPS)
    w_refs = refs[:3 * n_g + 1]
    (o_ref, kv0_ref, kv1_ref, kv2_ref, q_s, k_s, v_s, z_s, acc_s, m_s, l_s,
     sems) = refs[3 * n_g + 1:]
    b = pl.program_id(0)
    hp = pl.program_id(1)
    kv_refs = (kv0_ref, kv1_ref, kv2_ref)

    def proj(slab):
        return jnp.dot(h_ref[...], w_refs[slab][...], preferred_element_type=F32)

    def kv_copy(g, i, kv):
        keep = kv_refs[g].shape[1]
        src = (k_s, v_s)[kv].at[g, i, pl.ds(seq - keep, keep), :]
        col = pl.multiple_of(kv * WIDTH + (2 * hp + i) * D_HEAD, D_HEAD)
        dst = kv_refs[g].at[b, :, pl.ds(col, D_HEAD)]
        return pltpu.make_async_copy(src, dst, sems.at[(g * 2 + i) * 2 + kv])

    for g in range(n_g):
        k = proj(n_g + g)
        v = proj(2 * n_g + g)
        for i in range(2):
            k_s[g, i] = k[:, i * D_HEAD:(i + 1) * D_HEAD]
            v_s[g, i] = v[:, i * D_HEAD:(i + 1) * D_HEAD]
        for i in range(2):
            kv_copy(g, i, 0).start()
            kv_copy(g, i, 1).start()
        q = proj(g) * (ATTN_SCALE * LOG2_E)
        for i in range(2):
            q_s[g, i] = q[:, i * D_HEAD:(i + 1) * D_HEAD]

    for i in range(2):
        for g, (_, dil) in enumerate(DIL_GROUPS):
            _banded_attention(dil, seq, q_s.at[g, i], k_s.at[g, i], v_s.at[g, i],
                              acc_s.at[g], m_s.at[g], l_s.at[g])
        if i == 0:
            z = proj(3 * n_g)
            for j in range(2):
                z_s[j] = z[:, j * D_HEAD:(j + 1) * D_HEAD]
        m_all = jnp.maximum(jnp.maximum(m_s[0], m_s[1]), m_s[2])
        num = jnp.zeros((seq, D_HEAD), F32)
        den = jnp.zeros((seq, D_HEAD), F32)
        for g in range(n_g):
            w = jnp.exp2(m_s[g] - m_all)
            num = num + w * acc_s[g]
            den = den + w * l_s[g]
        o_ref[:, i * D_HEAD:(i + 1) * D_HEAD] = (num / den * _silu(z_s[i])).astype(BF16)

    for g in range(n_g):
        for i in range(2):
            kv_copy(g, i, 0).wait()
            kv_copy(g, i, 1).wait()


def _attn_layer_prompt(h, w):
    b, seq, _ = h.shape
    n_pair = N_HEADS // 2
    n_g = len(DIL_GROUPS)
    pair = 2 * D_HEAD
    keeps = [min(window, seq) for window, _ in DIL_GROUPS]
    slab = lambda n: pltpu.VMEM((n, seq, D_HEAD), F32)
    w_specs = [pl.BlockSpec((D_MODEL, pair), lambda i, p, s=s: (0, s * n_pair + p))
               for s in range(3 * n_g + 1)]
    return pl.pallas_call(
        functools.partial(_attn_layer_kernel, seq),
        grid=(b, n_pair),
        in_specs=[
            pl.BlockSpec((None, seq, D_MODEL), lambda i, p: (i, 0, 0),
                         pipeline_mode=pl.Buffered(1)),
        ] + w_specs,
        out_specs=[pl.BlockSpec((None, seq, 2 * D_HEAD), lambda i, p: (i, 0, p))]
        + [pl.BlockSpec(memory_space=pl.ANY)] * n_g,
        out_shape=[jax.ShapeDtypeStruct((b, seq, WIDTH), BF16)]
        + [jax.ShapeDtypeStruct((b, keep, 2 * WIDTH), F32) for keep in keeps],
        scratch_shapes=[
            pltpu.VMEM((n_g, 2, seq, D_HEAD), F32),
            pltpu.VMEM((n_g, 2, seq, D_HEAD), F32),
            pltpu.VMEM((n_g, 2, seq, D_HEAD), F32),
            slab(2),
            slab(n_g), slab(n_g), slab(n_g),
            pltpu.SemaphoreType.DMA((n_g * 2 * 2,)),
        ],
        compiler_params=_cparams(("arbitrary", "arbitrary")),
        name="attn_layer",
    )(h, *([w] * (3 * n_g + 1)))


def _attn_sample_kernel(n_new, *refs):
    q_ref, kv0_ref, kv1_ref, kv2_ref, z_ref, c0_ref, c1_ref, c2_ref, o_ref = refs
    n_t = KEYS_PER_QUERY
    ones = jnp.ones((D_HEAD, D_HEAD), BF16)
    dil1 = DIL_GROUPS[1][1]

    def key_tiles(g, l, kv):
        if g == 0:
            return jnp.concatenate([c0_ref[l:, kv], kv0_ref[0:l + 1, kv]], axis=0)
        if g == 1:
            a, r = divmod(l, dil1)
            parts = [c1_ref[a:, r, kv]]
            if a:
                parts.append(kv1_ref[r:r + 1, kv])
            parts.append(kv1_ref[l:l + 1, kv])
            return jnp.concatenate(parts, axis=0)
        return jnp.concatenate([c2_ref[:, l, kv], kv2_ref[l:l + 1, kv]], axis=0)

    for l in range(n_new):
        m_g, l_g, acc_g = [], [], []
        for g in range(len(DIL_GROUPS)):
            q = q_ref[l, g] * (ATTN_SCALE * LOG2_E)
            prod = (key_tiles(g, l, 0) * q[None]).reshape(n_t * N_HEADS, D_HEAD)
            s = jnp.dot(prod.astype(BF16), ones, preferred_element_type=F32)
            s = s.reshape(n_t, N_HEADS, D_HEAD)
            m = jnp.max(s, axis=0)
            p = jnp.exp2(s - m[None])
            m_g.append(m)
            l_g.append(jnp.sum(p, axis=0))
            acc_g.append(jnp.sum(p * key_tiles(g, l, 1), axis=0))
        m_all = jnp.maximum(jnp.maximum(m_g[0], m_g[1]), m_g[2])
        w_g = [jnp.exp2(m - m_all) for m in m_g]
        num = w_g[0] * acc_g[0] + w_g[1] * acc_g[1] + w_g[2] * acc_g[2]
        den = w_g[0] * l_g[0] + w_g[1] * l_g[1] + w_g[2] * l_g[2]
        o_ref[l] = num / den * _silu(z_ref[l])


def _attn_sample(q, kvs, z, caches):
    b, n_new = q.shape[:2]
    assert n_new == SUBLANES
    (w0, d0), (w1, d1), (w2, d2) = DIL_GROUPS
    assert caches[0].shape[1] == w0 and caches[1].shape[1] == w1 and caches[2].shape[1] == w2
    assert d0 == 1 and n_new % d1 == 0 and d2 == 2 * n_new
    assert w0 // d0 + 1 == KEYS_PER_QUERY and w1 // d1 + 1 == KEYS_PER_QUERY
    assert w2 // d2 + 1 == KEYS_PER_QUERY
    c1 = caches[1].reshape(b, w1 // d1, d1, 2, N_HEADS, D_HEAD)
    c2 = caches[2].reshape(b, w2 // d2, d2, 2, N_HEADS, D_HEAD)
    tail = (2, N_HEADS, D_HEAD)
    new_spec = pl.BlockSpec((None, n_new) + tail, lambda i: (i, 0, 0, 0, 0))
    return pl.pallas_call(
        functools.partial(_attn_sample_kernel, n_new),
        grid=(b,),
        in_specs=[
            pl.BlockSpec((None, n_new, 3, N_HEADS, D_HEAD), lambda i: (i, 0, 0, 0, 0)),
            new_spec, new_spec, new_spec,
            pl.BlockSpec((None, n_new, N_HEADS, D_HEAD), lambda i: (i, 0, 0, 0)),
            pl.BlockSpec((None, w0) + tail, lambda i: (i, 0, 0, 0, 0)),
            pl.BlockSpec((None, w1 // d1, d1) + tail, lambda i: (i, 0, 0, 0, 0, 0)),
            pl.BlockSpec((None, w2 // d2, n_new) + tail, lambda i: (i, 0, 0, 0, 0, 0)),
        ],
        out_specs=pl.BlockSpec((None, n_new, N_HEADS, D_HEAD), lambda i: (i, 0, 0, 0)),
        out_shape=jax.ShapeDtypeStruct((b, n_new, N_HEADS, D_HEAD), F32),
        compiler_params=_cparams(("arbitrary",)),
        name="attn_sample",
    )(q, kvs[0], kvs[1], kvs[2], z, caches[0], c1, c2)


def kernel(x_prompt, x_sample, state_delta, state_conv, cache_kv_w128, cache_kv_w512, cache_kv_w2048,
           c_prompt, c_sample, norm_g, ada_w, ada_b, a_w_in, a_conv_w, a_A_log, a_dt_bias,
           a_out_norm_g, a_w_out, b_w_in, b_w_out, final_norm_g):
    bp, seq, _ = x_prompt.shape
    bs, n_new, _ = x_sample.shape
    n_s = bs * n_new

    mod = _ada_mod(jnp.concatenate([c_prompt, c_sample], axis=0), ada_w, ada_b)
    mod_p = [mod[l, :bp].reshape(bp, 1, 3 * D_MODEL) for l in range(2)]
    mod_s = [jnp.repeat(mod[l, bp:], n_new, axis=0).reshape(1, n_s, 3 * D_MODEL) for l in range(2)]

    w_main = a_w_in[0].astype(BF16)
    w_ab = jnp.pad(w_main[:, CONV_DIM + WIDTH:], ((0, 0), (0, LANES - 2 * N_HEADS)))
    w_out_a = a_w_out[0].astype(BF16)
    xs_flat = x_sample.reshape(1, n_s, D_MODEL)

    x1_p, h1_p, delta_p, conv_p = _gdn_layer_prompt(
        x_prompt, mod_p[0], mod_p[1], norm_g[0], norm_g[1], w_main, w_ab, w_out_a,
        a_conv_w[0], a_A_log[0], a_dt_bias[0], a_out_norm_g[0], 1024, 128)

    qkv_s, z_s, ab_s, abt_s = _project(xs_flat, mod_s[0], norm_g[0], w_main, (3, 1), n_s, w_ab)
    abt_s = abt_s.reshape(16, bs, n_new).transpose(1, 0, 2)
    o_s, delta_s, conv_s = _gdn_core_decode(
        qkv_s.reshape(bs, n_new, CONV_DIM), z_s.reshape(bs, n_new, WIDTH),
        ab_s.reshape(bs, n_new, LANES), abt_s, a_conv_w[0], a_A_log[0], a_dt_bias[0],
        a_out_norm_g[0], state_delta[0], state_conv[0], 8)
    x1_s = _out_project(xs_flat, o_s.reshape(1, n_s, WIDTH), mod_s[0], w_out_a, final_norm_g,
                        n_s, False)

    n_g = len(DIL_GROUPS)
    w_b = b_w_in[0].astype(BF16)
    w_out_b = b_w_out[0].astype(BF16)

    ob_p, kv0_p, kv1_p, kv2_p = _attn_layer_prompt(h1_p, w_b)
    y_p = _out_project(x1_p, ob_p, mod_p[1], w_out_b, final_norm_g, 512, True)

    q_s, k_s, v_s, zb_s = _project(x1_s, mod_s[1], norm_g[1], w_b, (n_g, n_g, n_g, 1), n_s)
    k_s = k_s.reshape(bs, n_new, n_g, N_HEADS, D_HEAD)
    v_s = v_s.reshape(bs, n_new, n_g, N_HEADS, D_HEAD)
    kvn = [jnp.stack([k_s[:, :, g], v_s[:, :, g]], axis=2) for g in range(n_g)]
    ob_s = _attn_sample(q_s.reshape(bs, n_new, n_g, N_HEADS, D_HEAD), kvn,
                        zb_s.reshape(bs, n_new, N_HEADS, D_HEAD),
                        (cache_kv_w128[0], cache_kv_w512[0], cache_kv_w2048[0]))
    y_s = _out_project(x1_s, ob_s.reshape(1, n_s, WIDTH), mod_s[1], w_out_b, final_norm_g,
                       n_s, True)

    def kv_prompt(kv):
        return kv.reshape(1, bp, kv.shape[1], 2, N_HEADS, D_HEAD)

    return (y_p, y_s.reshape(bs, n_new, D_MODEL),
            delta_p[None], delta_s[None], conv_p[None], conv_s[None],
            kv_prompt(kv0_p), kvn[0][None], kv_prompt(kv1_p), kvn[1][None],
            kv_prompt(kv2_p), kvn[2][None])
```

```python
import functools

import jax
import jax.numpy as jnp
from jax import lax
from jax.experimental import pallas as pl
from jax.experimental.pallas import tpu as pltpu

F32 = jnp.float32
BF16 = jnp.bfloat16

D_MODEL = 1024
N_HEADS = 8
D_HEAD = 128
WIDTH = N_HEADS * D_HEAD
CONV_W = 4
CONV_DIM = 3 * WIDTH
DIL_GROUPS = ((128, 1), (512, 4), (2048, 16))
KEYS_PER_QUERY = 129
EPS = 1e-6
ATTN_SCALE = D_HEAD ** -0.5
LOG2_E = 1.4426950408889634
NEG_BIG = -1e30

LANES = 128
SUBLANES = 8
VMEM_LIMIT_BYTES = 56 * 1024 * 1024


def _cparams(semantics):
    return pltpu.CompilerParams(dimension_semantics=semantics,
                                vmem_limit_bytes=VMEM_LIMIT_BYTES)


def _silu(x):
    return x * (1.0 / (1.0 + jnp.exp(-x)))


def _softplus(x):
    return jnp.maximum(x, 0.0) + jnp.log1p(jnp.exp(-jnp.abs(x)))


def _dot(a, b):
    return jnp.dot(a.astype(BF16), b.astype(BF16), preferred_element_type=F32)


def _dot_nt(a, b):
    return lax.dot_general(a.astype(BF16), b.astype(BF16), (((1,), (1,)), ((), ())),
                           preferred_element_type=F32)


def _dot_tn(a, b):
    return lax.dot_general(a.astype(BF16), b.astype(BF16), (((0,), (0,)), ((), ())),
                           preferred_element_type=F32)


def _split3(x):
    x1 = x.astype(BF16)
    r = x - x1.astype(F32)
    x2 = r.astype(BF16)
    x3 = (r - x2.astype(F32)).astype(BF16)
    return x1, x2, x3


def _dot_sel(sel, x):
    s = sel.astype(BF16)
    out = None
    for p in _split3(x):
        t = jnp.dot(s, p, preferred_element_type=F32)
        out = t if out is None else out + t
    return out


def _dot_x_sel(x, sel):
    s = sel.astype(BF16)
    out = None
    for p in _split3(x):
        t = jnp.dot(p, s, preferred_element_type=F32)
        out = t if out is None else out + t
    return out


def _mod_kernel(c_ref, w_ref, b_ref, o_ref):
    s = _silu(c_ref[...])
    o_ref[...] = _dot(s, w_ref[...]) + b_ref[...]


def _ada_mod(c_all, ada_w, ada_b):
    n_layers = ada_w.shape[0]
    rows = c_all.shape[0]
    tn = 768
    return pl.pallas_call(
        _mod_kernel,
        grid=(n_layers, 3 * D_MODEL // tn),
        in_specs=[
            pl.BlockSpec((rows, D_MODEL), lambda l, j: (0, 0)),
            pl.BlockSpec((None, D_MODEL, tn), lambda l, j: (l, 0, j)),
            pl.BlockSpec((None, 1, tn), lambda l, j: (l, 0, j)),
        ],
        out_specs=pl.BlockSpec((None, rows, tn), lambda l, j: (l, 0, j)),
        out_shape=jax.ShapeDtypeStruct((n_layers, rows, 3 * D_MODEL), F32),
        compiler_params=_cparams(("arbitrary", "arbitrary")),
        name="ada_mod",
    )(c_all, ada_w, ada_b.reshape(n_layers, 1, 3 * D_MODEL))


def _proj_kernel(seg_bounds, has_ab, *refs):
    x_ref, mod_ref, g_ref, w_ref = refs[:4]
    pos = 4
    if has_ab:
        wab_ref = refs[4]
        pos = 5
    n_seg = len(seg_bounds)
    seg_refs = refs[pos:pos + n_seg]
    pos += n_seg
    if has_ab:
        ab_ref, abt_ref = refs[pos:pos + 2]
        pos += 2
    h_ref = refs[pos]
    j = pl.program_id(2)

    @pl.when(j == 0)
    def _():
        hb = _mod_norm(x_ref[...], g_ref[...], mod_ref[...]).astype(BF16)
        h_ref[...] = hb
        if has_ab:
            ab = jnp.dot(hb, wab_ref[...], preferred_element_type=F32)
            ab_ref[...] = ab
            abt_ref[...] = ab.T[0:16, :]

    res = jnp.dot(h_ref[...], w_ref[...], preferred_element_type=F32)
    for (lo, hi), o_ref in zip(seg_bounds, seg_refs):
        @pl.when((j >= lo) & (j < hi))
        def _(o_ref=o_ref):
            o_ref[...] = res


def _project(x, mod, norm_g, w, seg_slabs, tm, w_ab=None):
    n, t, _ = x.shape
    r = mod.shape[1]
    slab = 1024
    n_slab = w.shape[1] // slab
    assert sum(seg_slabs) == n_slab and t % tm == 0
    has_ab = w_ab is not None
    bounds, lo = [], 0
    for s in seg_slabs:
        bounds.append((lo, lo + s))
        lo += s
    mod_rows = 1 if r == 1 else tm
    mod_map = (lambda b, i, j: (b, 0, 0)) if r == 1 else (lambda b, i, j: (b, i, 0))
    in_specs = [
        pl.BlockSpec((None, tm, D_MODEL), lambda b, i, j: (b, i, 0)),
        pl.BlockSpec((None, mod_rows, 3 * D_MODEL), mod_map),
        pl.BlockSpec((1, D_MODEL), lambda b, i, j: (0, 0)),
        pl.BlockSpec((D_MODEL, slab), lambda b, i, j: (0, j)),
    ]
    args = [x, mod, norm_g.reshape(1, D_MODEL), w]
    if has_ab:
        in_specs.append(pl.BlockSpec((D_MODEL, LANES), lambda b, i, j: (0, 0)))
        args.append(w_ab)
    out_specs, out_shapes = [], []
    for (lo, hi) in bounds:
        def seg_map(b, i, j, lo=lo, hi=hi):
            return (b, i, jnp.clip(j - lo, 0, hi - lo - 1))
        out_specs.append(pl.BlockSpec((None, tm, slab), seg_map))
        out_shapes.append(jax.ShapeDtypeStruct((n, t, slab * (hi - lo)), F32))
    if has_ab:
        out_specs += [pl.BlockSpec((None, tm, LANES), lambda b, i, j: (b, i, 0)),
                      pl.BlockSpec((None, 16, tm), lambda b, i, j: (b, 0, i))]
        out_shapes += [jax.ShapeDtypeStruct((n, t, LANES), F32),
                       jax.ShapeDtypeStruct((n, 16, t), F32)]
    return pl.pallas_call(
        functools.partial(_proj_kernel, tuple(bounds), has_ab),
        grid=(n, t // tm, n_slab),
        in_specs=in_specs,
        out_specs=out_specs,
        out_shape=out_shapes,
        scratch_shapes=[pltpu.VMEM((tm, D_MODEL), BF16)],
        compiler_params=_cparams(("arbitrary", "arbitrary", "arbitrary")),
        name="norm_proj",
    )(*args)


def _mod_norm(x, g, mod):
    ms = jnp.mean(x * x, axis=-1, keepdims=True)
    y = x * lax.rsqrt(ms + EPS) * g
    return y * (1.0 + mod[:, D_MODEL:2 * D_MODEL]) + mod[:, 0:D_MODEL]


def _out_kernel(final_norm, x_ref, o_ref, mod_ref, w_ref, fg_ref, out_ref):
    y = _dot(o_ref[...], w_ref[...])
    gate = mod_ref[:, 2 * D_MODEL:3 * D_MODEL]
    x = x_ref[...] + gate * y
    if final_norm:
        ms = jnp.mean(x * x, axis=-1, keepdims=True)
        x = x * lax.rsqrt(ms + EPS) * fg_ref[...]
    out_ref[...] = x


def _out_project(x, o, mod, w_out, final_g, tm, final_norm):
    n, t, _ = x.shape
    r = mod.shape[1]
    mod_rows = 1 if r == 1 else tm
    mod_map = (lambda b, i: (b, 0, 0)) if r == 1 else (lambda b, i: (b, i, 0))
    row_spec = pl.BlockSpec((None, tm, D_MODEL), lambda b, i: (b, i, 0))
    return pl.pallas_call(
        functools.partial(_out_kernel, final_norm),
        grid=(n, t // tm),
        in_specs=[
            row_spec,
            pl.BlockSpec((None, tm, WIDTH), lambda b, i: (b, i, 0)),
            pl.BlockSpec((None, mod_rows, 3 * D_MODEL), mod_map),
            pl.BlockSpec((WIDTH, D_MODEL), lambda b, i: (0, 0)),
            pl.BlockSpec((1, D_MODEL), lambda b, i: (0, 0)),
        ],
        out_specs=row_spec,
        out_shape=jax.ShapeDtypeStruct((n, t, D_MODEL), F32),
        compiler_params=_cparams(("arbitrary", "arbitrary")),
        name="out_proj",
    )(x, o, mod, w_out, final_g.reshape(1, D_MODEL))


def _unit_lower_inverses(n_mats, c):
    row = lax.broadcasted_iota(jnp.int32, (c, c), 0)
    col = lax.broadcasted_iota(jnp.int32, (c, c), 1)
    base = min(16, c)
    blk_id = lambda v, size: jnp.right_shift(v, size.bit_length() - 1)
    eye = (row == col).astype(F32)
    same = blk_id(row, base) == blk_id(col, base)
    ps = [jnp.where(same, -n, 0.0) for n in n_mats]
    invs = [eye + p for p in ps]
    assert base >= 4
    ps = [_dot(p, p).astype(BF16) for p in ps]
    power = 2
    while 2 * power < base:
        if c % LANES == 0:
            wide = [_dot(p, jnp.concatenate([p, inv.astype(BF16)], axis=1))
                    for p, inv in zip(ps, invs)]
            invs = [inv + w[:, c:] for inv, w in zip(invs, wide)]
            ps = [w[:, 0:c].astype(BF16) for w in wide]
        else:
            invs = [inv + _dot(p, inv) for p, inv in zip(ps, invs)]
            ps = [_dot(p, p).astype(BF16) for p in ps]
        power *= 2
    invs = [inv + _dot(p, inv) for p, inv in zip(ps, invs)]
    size = base
    while size < c:
        same_next = blk_id(row, 2 * size) == blk_id(col, 2 * size)
        off = same_next & jnp.logical_not(same)
        ls = [jnp.where(off, n, 0.0).astype(BF16) for n in n_mats]
        invs_b = [inv.astype(BF16) for inv in invs]
        ts = [_dot(ib, l) for ib, l in zip(invs_b, ls)]
        invs = [inv - _dot(t, ib) for inv, t, ib in zip(invs, ts, invs_b)]
        same = same_next
        size *= 2
    return invs


def _aligned(v, m):
    return v if isinstance(v, int) else pl.multiple_of(v, m)


EXT_SLABS = CONV_DIM // D_HEAD
EXT_LEAD = SUBLANES


def _ext_idx(slab, start, n):
    return (slab // 2, pl.ds(2 * start + slab % 2, n, stride=2), slice(None))


def _ext_shape(rows):
    return (EXT_SLABS // 2, 2 * (rows + EXT_LEAD), D_HEAD)


def _chunk_masks(c):
    row = lax.broadcasted_iota(jnp.int32, (c, c), 0)
    col = lax.broadcasted_iota(jnp.int32, (c, c), 1)
    tril_sel = jnp.where(row >= col, 1.0, 0.0).astype(BF16)
    triu_sel = jnp.where(row <= col, 1.0, 0.0).astype(BF16)
    return row >= col, row > col, tril_sel, triu_sel


def _problems(n_req):
    return [(i, h) for i in range(n_req) for h in range(N_HEADS)]


def _gdn_chunk_prepare(c, r0, masks, requests, cw_ref):
    tril, strict, tril_sel, triu_sel = masks
    pad = EXT_LEAD
    halo = CONV_W - 1
    heads = _problems(len(requests))
    gc_alls, gct_alls, beta_alls = [], [], []
    for _, _, ab_ref, g_ref, gt_ref, _ in requests:
        gc_alls.append(_dot_sel(tril_sel, g_ref[pl.ds(r0, c), :]))
        gct_alls.append(_dot_x_sel(gt_ref[0:N_HEADS, pl.ds(r0, c)], triu_sel))
        beta_alls.append(1.0 / (1.0 + jnp.exp(-ab_ref[pl.ds(r0, c), :])))

    def conv(off, i, h):
        slab = off // D_HEAD + h
        cols = slice(slab * D_HEAD, (slab + 1) * D_HEAD)
        acc = None
        for tap in range(CONV_W):
            rows = _ext_idx(slab, r0 + pad - halo + tap, c)
            term = requests[i][0][rows] * cw_ref[tap:tap + 1, cols]
            acc = term if acc is None else acc + term
        return _silu(acc)

    qs, ks, vs = [], [], []
    for i, h in heads:
        q = conv(0, i, h)
        k = conv(WIDTH, i, h)
        qs.append(q * lax.rsqrt(jnp.sum(q * q, axis=-1, keepdims=True) + EPS)
                  * (D_HEAD ** -0.5))
        ks.append(k * lax.rsqrt(jnp.sum(k * k, axis=-1, keepdims=True) + EPS))
        vs.append(conv(2 * WIDTH, i, h))
    gcs = [jnp.broadcast_to(gc_alls[i][:, h:h + 1], (c, D_HEAD)) for i, h in heads]
    betas = [jnp.broadcast_to(beta_alls[i][:, N_HEADS + h:N_HEADS + h + 1], (c, D_HEAD))
             for i, h in heads]
    g_lasts = [gc[c - 1:c, :] for gc in gcs]
    decays = [jnp.where(tril, jnp.exp2(gc[:, 0:c] - gct_alls[i][h:h + 1, :]), 0.0)
              for gc, (i, h) in zip(gcs, heads)]
    kbs = [k * beta for k, beta in zip(ks, betas)]
    ks_b = [k.astype(BF16) for k in ks]
    kks = [_dot_nt(kb, k) for kb, k in zip(kbs, ks_b)]
    qks = [_dot_nt(q, k) for q, k in zip(qs, ks_b)]
    n_mats = [jnp.where(strict, kk * decay, 0.0) for kk, decay in zip(kks, decays)]
    qks = [(qk * decay).astype(BF16) for qk, decay in zip(qks, decays)]
    invs = _unit_lower_inverses(n_mats, c)
    e_gcs = [jnp.exp2(gc) for gc in gcs]
    rhss = [jnp.concatenate([v * beta, kb * e_gc], axis=1).astype(BF16)
            for v, beta, kb, e_gc in zip(vs, betas, kbs, e_gcs)]
    sols = [_dot(inv, rhs) for inv, rhs in zip(invs, rhss)]
    q_decs = [(q * e_gc).astype(BF16) for q, e_gc in zip(qs, e_gcs)]
    k_decs = [(k * jnp.exp2(g_last - gc)).astype(BF16)
              for k, g_last, gc in zip(ks, g_lasts, gcs)]
    return [(sol[:, 0:D_HEAD], sol[:, D_HEAD:2 * D_HEAD].astype(BF16), qk, q_dec, k_dec,
             jnp.exp2(g_last))
            for sol, qk, q_dec, k_dec, g_last in zip(sols, qks, q_decs, k_decs, g_lasts)]


def _gdn_chunk_apply(c, r0, prepared, requests, og_ref):
    n_req = len(requests)
    heads = _problems(n_req)
    s_olds = [requests[i][5][h] for i, h in heads]
    s_bs = [s.astype(BF16) for s in s_olds]
    on_s = [_dot(jnp.concatenate([k_cum, q_dec], axis=0), s_b)
            for (_, k_cum, _, q_dec, _, _), s_b in zip(prepared, s_bs)]
    us = [w - ks[0:c] for (w, _, _, _, _, _), ks in zip(prepared, on_s)]
    us_b = [u.astype(BF16) for u in us]
    os_ = [ks[c:2 * c] + _dot(qk, u_b)
           for (_, _, qk, _, _, _), ks, u_b in zip(prepared, on_s, us_b)]
    for n, (i, h) in enumerate(heads):
        k_dec, s_decay = prepared[n][4], prepared[n][5]
        requests[i][5][h] = s_olds[n] * s_decay + _dot_tn(k_dec, us_b[n])
    outs = [[] for _ in range(n_req)]
    for n, (i, h) in enumerate(heads):
        o = os_[n]
        o = o * lax.rsqrt(jnp.mean(o * o, axis=-1, keepdims=True) + EPS) * og_ref[...]
        z = requests[i][1][pl.ds(r0, c), h * D_HEAD:(h + 1) * D_HEAD]
        outs[i].append(o * _silu(z))
    return outs


def _decay_logits(ab, alog, dtb):
    return (-LOG2_E * jnp.exp(alog)) * _softplus(ab + dtb)


def _gdn_decode_kernel(c, n_req, qkv_ref, z_ref, ab_ref, abt_ref, cw_ref, alog_ref, dtb_ref,
                       alogc_ref, dtbc_ref, og_ref, s0_ref, conv0_ref,
                       o_ref, s_ref, conv_ref, ext_ref, g_ref, gt_ref):
    halo = CONV_W - 1
    s_ref[...] = s0_ref[...]
    lead_fill = jnp.zeros((EXT_LEAD - halo, D_HEAD), F32)
    for i in range(n_req):
        for slab in range(EXT_SLABS):
            cols = slice(slab * D_HEAD, (slab + 1) * D_HEAD)
            ext_ref[(i,) + _ext_idx(slab, 0, EXT_LEAD)] = jnp.concatenate(
                [lead_fill, conv0_ref[i, :, cols]], axis=0)
            ext_ref[(i,) + _ext_idx(slab, EXT_LEAD, c)] = qkv_ref[i, :, cols]
    g_ref[...] = _decay_logits(ab_ref[...], alog_ref[...], dtb_ref[...])
    gt_ref[...] = _decay_logits(abt_ref[...], alogc_ref[...], dtbc_ref[...])
    requests = [(ext_ref.at[i], z_ref.at[i], ab_ref.at[i], g_ref.at[i], gt_ref.at[i], s_ref.at[i])
                for i in range(n_req)]
    prepared = _gdn_chunk_prepare(c, 0, _chunk_masks(c), requests, cw_ref)
    outs = _gdn_chunk_apply(c, 0, prepared, requests, og_ref)
    for i in range(n_req):
        for h, o in enumerate(outs[i]):
            o_ref[i, :, h * D_HEAD:(h + 1) * D_HEAD] = o
        for slab in range(EXT_SLABS):
            last = ext_ref[(i,) + _ext_idx(slab, c, EXT_LEAD)]
            conv_ref[i, :, slab * D_HEAD:(slab + 1) * D_HEAD] = last[EXT_LEAD - halo:]


def _gdn_core_decode(qkv, z, ab, abt, conv_w, a_log, dt_bias, out_g, s0, conv0, n_req):
    b, c, _ = qkv.shape
    assert b % n_req == 0
    const = lambda shape: pl.BlockSpec(shape, lambda i: (0,) * len(shape))
    per_req = lambda *tail: pl.BlockSpec((n_req,) + tail, lambda i: (i,) + (0,) * len(tail))
    return pl.pallas_call(
        functools.partial(_gdn_decode_kernel, c, n_req),
        grid=(b // n_req,),
        in_specs=[
            per_req(c, CONV_DIM), per_req(c, WIDTH), per_req(c, LANES), per_req(16, c),
            const((CONV_W, CONV_DIM)), const((1, LANES)), const((1, LANES)),
            const((16, 1)), const((16, 1)), const((1, D_HEAD)),
            per_req(N_HEADS, D_HEAD, D_HEAD), per_req(CONV_W - 1, CONV_DIM),
        ],
        out_specs=[per_req(c, WIDTH), per_req(N_HEADS, D_HEAD, D_HEAD),
                   per_req(CONV_W - 1, CONV_DIM)],
        out_shape=[
            jax.ShapeDtypeStruct((b, c, WIDTH), F32),
            jax.ShapeDtypeStruct((b, N_HEADS, D_HEAD, D_HEAD), F32),
            jax.ShapeDtypeStruct((b, CONV_W - 1, CONV_DIM), F32),
        ],
        scratch_shapes=[
            pltpu.VMEM((n_req,) + _ext_shape(c), F32),
            pltpu.VMEM((n_req, c, LANES), F32),
            pltpu.VMEM((n_req, 16, c), F32),
        ],
        compiler_params=_cparams(("arbitrary",)),
        name="gdn_core",
    )(qkv, z, ab, abt, conv_w, *_head_params(a_log, dt_bias), out_g.reshape(1, D_HEAD),
      s0, conv0)


def _head_params(a_log, dt_bias):
    row = lambda v: jnp.pad(v.reshape(1, N_HEADS), ((0, 0), (0, LANES - N_HEADS)))
    col = lambda v: jnp.pad(v.reshape(N_HEADS, 1), ((0, 16 - N_HEADS), (0, 0)))
    return row(a_log), row(dt_bias), col(a_log), col(dt_bias)


def _gdn_layer_kernel(tm, c, x_ref, mod_ref, modn_ref, g_ref, gn_ref, w_ref, wab_ref,
                      wout_ref, cw_ref, alog_ref, dtb_ref, alogc_ref, dtbc_ref, og_ref,
                      x1_ref, h1_ref, s_ref, conv_ref, ext_ref, z_s, ab_s, gl_s, gt_s):
    t = pl.program_id(1)
    pad = EXT_LEAD
    halo = CONV_W - 1
    n_chunks = tm // c
    n_col = 512
    request = (ext_ref, z_s, ab_s, gl_s, gt_s, s_ref)

    @pl.when(t == 0)
    def _():
        s_ref[...] = jnp.zeros_like(s_ref)
        ext_ref[:, 0:2 * pad, :] = jnp.zeros((EXT_SLABS // 2, 2 * pad, D_HEAD), F32)

    def project(r0):
        hb = _mod_norm(x_ref[pl.ds(r0, c), :], g_ref[...], mod_ref[...]).astype(BF16)
        for lo in range(0, CONV_DIM + WIDTH, n_col):
            res = jnp.dot(hb, w_ref[:, lo:lo + n_col], preferred_element_type=F32)
            if lo < CONV_DIM:
                for j in range(n_col // D_HEAD):
                    ext_ref[_ext_idx(lo // D_HEAD + j, pad + r0, c)] = (
                        res[:, j * D_HEAD:(j + 1) * D_HEAD])
            else:
                z_s[pl.ds(r0, c), lo - CONV_DIM:lo - CONV_DIM + n_col] = res
        ab = jnp.dot(hb, wab_ref[...], preferred_element_type=F32)
        ab_s[pl.ds(r0, c), :] = ab
        gl_s[pl.ds(r0, c), :] = _decay_logits(ab, alog_ref[...], dtb_ref[...])
        gt_s[:, pl.ds(r0, c)] = _decay_logits(ab.T[0:16, :], alogc_ref[...], dtbc_ref[...])

    masks = _chunk_masks(c)
    project(0)

    def step(i, carry):
        r0 = pl.multiple_of(i * c, c)
        prepared = _gdn_chunk_prepare(c, r0, masks, [request], cw_ref)
        outs, = _gdn_chunk_apply(c, r0, prepared, [request], og_ref)
        o = jnp.concatenate([o.astype(BF16) for o in outs], axis=1)
        y = jnp.dot(o, wout_ref[...], preferred_element_type=F32)
        x1 = x_ref[pl.ds(r0, c), :] + mod_ref[:, 2 * D_MODEL:3 * D_MODEL] * y
        x1_ref[pl.ds(r0, c), :] = x1
        h1_ref[pl.ds(r0, c), :] = _mod_norm(x1, gn_ref[...], modn_ref[...]).astype(BF16)
        project(pl.multiple_of(jnp.minimum(i + 1, n_chunks - 1) * c, c))
        return carry

    lax.fori_loop(0, n_chunks, step, 0)

    for slab in range(EXT_SLABS):
        last = ext_ref[_ext_idx(slab, tm, pad)]
        ext_ref[_ext_idx(slab, 0, pad)] = last
        conv_ref[:, slab * D_HEAD:(slab + 1) * D_HEAD] = last[pad - halo:]


def _gdn_layer_prompt(x, mod, mod_next, norm_g, norm_g_next, w_main, w_ab, w_out, conv_w,
                      a_log, dt_bias, out_g, tm, c):
    b, t, _ = x.shape
    const = lambda shape: pl.BlockSpec(shape, lambda i, j: (0,) * len(shape))
    resident = lambda shape: pl.BlockSpec(shape, lambda i, j: (0,) * len(shape),
                                          pipeline_mode=pl.Buffered(1))
    mod_spec = pl.BlockSpec((None, 1, 3 * D_MODEL), lambda i, j: (i, 0, 0))
    row_spec = pl.BlockSpec((None, tm, D_MODEL), lambda i, j: (i, j, 0))
    return pl.pallas_call(
        functools.partial(_gdn_layer_kernel, tm, c),
        grid=(b, t // tm),
        in_specs=[
            row_spec, mod_spec, mod_spec, const((1, D_MODEL)), const((1, D_MODEL)),
            resident((D_MODEL, CONV_DIM + WIDTH)), const((D_MODEL, LANES)),
            resident((WIDTH, D_MODEL)), const((CONV_W, CONV_DIM)),
            const((1, LANES)), const((1, LANES)), const((16, 1)), const((16, 1)),
            const((1, D_HEAD)),
        ],
        out_specs=[
            row_spec, row_spec,
            pl.BlockSpec((None, N_HEADS, D_HEAD, D_HEAD), lambda i, j: (i, 0, 0, 0)),
            pl.BlockSpec((None, CONV_W - 1, CONV_DIM), lambda i, j: (i, 0, 0)),
        ],
        out_shape=[
            jax.ShapeDtypeStruct((b, t, D_MODEL), F32),
            jax.ShapeDtypeStruct((b, t, D_MODEL), BF16),
            jax.ShapeDtypeStruct((b, N_HEADS, D_HEAD, D_HEAD), F32),
            jax.ShapeDtypeStruct((b, CONV_W - 1, CONV_DIM), F32),
        ],
        scratch_shapes=[
            pltpu.VMEM(_ext_shape(tm), F32),
            pltpu.VMEM((tm, WIDTH), F32),
            pltpu.VMEM((tm, LANES), F32),
            pltpu.VMEM((tm, LANES), F32),
            pltpu.VMEM((16, tm), F32),
        ],
        compiler_params=_cparams(("arbitrary", "arbitrary")),
        name="gdn_layer",
    )(x, mod, mod_next, norm_g.reshape(1, D_MODEL), norm_g_next.reshape(1, D_MODEL),
      w_main, w_ab, w_out, conv_w, *_head_params(a_log, dt_bias), out_g.reshape(1, D_HEAD))


def _banded_attention(dil, seq, q_ref, k_ref, v_ref, acc_ref, m_ref, l_ref, normalized):
    blk = D_HEAD
    per_step = 8
    n_blk = seq // dil // blk
    with_prev = n_blk > 1
    n_keys = 2 * blk if with_prev else blk
    row = lax.broadcasted_iota(jnp.int32, (blk, n_keys), 0)
    col = lax.broadcasted_iota(jnp.int32, (blk, n_keys), 1)
    is_prev = col < (n_keys - blk)
    slack = jnp.where(is_prev, col - row, row + (n_keys - blk) - col)
    cur_ok = slack >= 0
    ones = jnp.ones((n_keys, blk), BF16)

    def rows(start):
        if dil == 1:
            return pl.ds(start, blk)
        return pl.ds(start, blk, stride=dil)

    def body(i, carry):
        starts, masks, qs, kcs, vcs = [], [], [], [], []
        for u in range(per_step):
            idx = i * per_step + u
            r = idx // n_blk
            n = idx % n_blk
            start = r + dil * blk * n
            starts.append(start)
            qs.append(q_ref[rows(start), :].astype(BF16))
            k_cur = k_ref[rows(start), :].astype(BF16)
            v_cur = v_ref[rows(start), :].astype(BF16)
            if with_prev:
                if u % min(n_blk, per_step) != 0:
                    k_prev, v_prev = k_last, v_last
                else:
                    start_p = r + dil * blk * jnp.maximum(n - 1, 0)
                    k_prev = k_ref[rows(start_p), :].astype(BF16)
                    v_prev = v_ref[rows(start_p), :].astype(BF16)
                k_last, v_last = k_cur, v_cur
                first = jnp.where(n > 0, 0, 2 * blk)
                masks.append(slack - jnp.where(is_prev, first, 0) >= 0)
                k_cur = jnp.concatenate([k_prev, k_cur], axis=0)
                v_cur = jnp.concatenate([v_prev, v_cur], axis=0)
            else:
                masks.append(cur_ok)
            kcs.append(k_cur)
            vcs.append(jnp.concatenate([v_cur, ones], axis=1))
        ss = [jnp.where(mask, _dot_nt(q, kc), NEG_BIG) for q, kc, mask in zip(qs, kcs, masks)]
        ms = [jnp.max(s, axis=-1, keepdims=True) for s in ss]
        ps = [jnp.exp2(s - m).astype(BF16) for s, m in zip(ss, ms)]
        accs = [jnp.dot(p, vc, preferred_element_type=F32) for p, vc in zip(ps, vcs)]
        for start, acc, m in zip(starts, accs, ms):
            if normalized:
                l = acc[:, blk:2 * blk]
                acc_ref[rows(start), :] = acc[:, 0:blk] / l
                m_ref[rows(start), :] = m + jnp.log2(l)
            else:
                acc_ref[rows(start), :] = acc[:, 0:blk]
                l_ref[rows(start), :] = acc[:, blk:2 * blk]
                m_ref[rows(start), :] = jnp.broadcast_to(m, (blk, blk))
        return carry

    lax.fori_loop(0, dil * n_blk // per_step, body, 0)


def _attn_layer_kernel(seq, h_ref, *refs):
    n_g = len(DIL_GROUPS)
    w_refs = refs[:3 * n_g + 1]
    (o_ref, kv0_ref, kv1_ref, kv2_ref, q_s, k_s, v_s, z_s, acc_s, m_s, l_s,
     sems) = refs[3 * n_g + 1:]
    b = pl.program_id(0)
    hp = pl.program_id(1)
    kv_refs = (kv0_ref, kv1_ref, kv2_ref)

    def proj(slab):
        return jnp.dot(h_ref[...], w_refs[slab][...], preferred_element_type=F32)

    def kv_copy(g, i, kv):
        keep = kv_refs[g].shape[1]
        src = (k_s, v_s)[kv].at[g, i, pl.ds(seq - keep, keep), :]
        col = pl.multiple_of(kv * WIDTH + (2 * hp + i) * D_HEAD, D_HEAD)
        dst = kv_refs[g].at[b, :, pl.ds(col, D_HEAD)]
        return pltpu.make_async_copy(src, dst, sems.at[(g * 2 + i) * 2 + kv])

    for g in range(n_g):
        k = proj(n_g + g)
        v = proj(2 * n_g + g)
        for i in range(2):
            k_s[g, i] = k[:, i * D_HEAD:(i + 1) * D_HEAD]
            v_s[g, i] = v[:, i * D_HEAD:(i + 1) * D_HEAD]
        for i in range(2):
            kv_copy(g, i, 0).start()
            kv_copy(g, i, 1).start()
        q = proj(g) * (ATTN_SCALE * LOG2_E)
        for i in range(2):
            q_s[g, i] = q[:, i * D_HEAD:(i + 1) * D_HEAD]

    normalized = [dil % 16 == 0 for _, dil in DIL_GROUPS]
    for i in range(2):
        for g, (_, dil) in enumerate(DIL_GROUPS):
            _banded_attention(dil, seq, q_s.at[g, i], k_s.at[g, i], v_s.at[g, i],
                              acc_s.at[g], m_s.at[g], l_s.at[g], normalized[g])
        if i == 0:
            z = proj(3 * n_g)
            for j in range(2):
                z_s[j] = z[:, j * D_HEAD:(j + 1) * D_HEAD]
        m_all = jnp.maximum(jnp.maximum(m_s[0], m_s[1]), m_s[2])
        num = jnp.zeros((seq, D_HEAD), F32)
        den = jnp.zeros((seq, D_HEAD), F32)
        for g in range(n_g):
            w = jnp.exp2(m_s[g] - m_all)
            num = num + w * acc_s[g]
            den = den + (w if normalized[g] else w * l_s[g])
        o_ref[:, i * D_HEAD:(i + 1) * D_HEAD] = (num / den * _silu(z_s[i])).astype(BF16)

    for g in range(n_g):
        for i in range(2):
            kv_copy(g, i, 0).wait()
            kv_copy(g, i, 1).wait()


def _attn_layer_prompt(h, w):
    b, seq, _ = h.shape
    n_pair = N_HEADS // 2
    n_g = len(DIL_GROUPS)
    pair = 2 * D_HEAD
    keeps = [min(window, seq) for window, _ in DIL_GROUPS]
    slab = lambda n: pltpu.VMEM((n, seq, D_HEAD), F32)
    w_specs = [pl.BlockSpec((D_MODEL, pair), lambda i, p, s=s: (0, s * n_pair + p))
               for s in range(3 * n_g + 1)]
    return pl.pallas_call(
        functools.partial(_attn_layer_kernel, seq),
        grid=(b, n_pair),
        in_specs=[
            pl.BlockSpec((None, seq, D_MODEL), lambda i, p: (i, 0, 0),
                         pipeline_mode=pl.Buffered(1)),
        ] + w_specs,
        out_specs=[pl.BlockSpec((None, seq, 2 * D_HEAD), lambda i, p: (i, 0, p))]
        + [pl.BlockSpec(memory_space=pl.ANY)] * n_g,
        out_shape=[jax.ShapeDtypeStruct((b, seq, WIDTH), BF16)]
        + [jax.ShapeDtypeStruct((b, keep, 2 * WIDTH), F32) for keep in keeps],
        scratch_shapes=[
            pltpu.VMEM((n_g, 2, seq, D_HEAD), F32),
            pltpu.VMEM((n_g, 2, seq, D_HEAD), F32),
            pltpu.VMEM((n_g, 2, seq, D_HEAD), F32),
            slab(2),
            slab(n_g), slab(n_g), slab(n_g),
            pltpu.SemaphoreType.DMA((n_g * 2 * 2,)),
        ],
        compiler_params=_cparams(("arbitrary", "arbitrary")),
        name="attn_layer",
    )(h, *([w] * (3 * n_g + 1)))


def _attn_sample_kernel(n_new, *refs):
    q_ref, kv0_ref, kv1_ref, kv2_ref, z_ref, c0_ref, c1_ref, c2_ref, o_ref = refs
    n_t = KEYS_PER_QUERY
    ones = jnp.ones((D_HEAD, D_HEAD), BF16)
    dil1 = DIL_GROUPS[1][1]

    def key_tiles(g, l, kv):
        if g == 0:
            return jnp.concatenate([c0_ref[l:, kv], kv0_ref[0:l + 1, kv]], axis=0)
        if g == 1:
            a, r = divmod(l, dil1)
            parts = [c1_ref[a:, r, kv]]
            if a:
                parts.append(kv1_ref[r:r + 1, kv])
            parts.append(kv1_ref[l:l + 1, kv])
            return jnp.concatenate(parts, axis=0)
        return jnp.concatenate([c2_ref[:, l, kv], kv2_ref[l:l + 1, kv]], axis=0)

    for l in range(n_new):
        m_g, l_g, acc_g = [], [], []
        for g in range(len(DIL_GROUPS)):
            q = q_ref[l, g] * (ATTN_SCALE * LOG2_E)
            prod = (key_tiles(g, l, 0) * q[None]).reshape(n_t * N_HEADS, D_HEAD)
            s = jnp.dot(prod.astype(BF16), ones, preferred_element_type=F32)
            s = s.reshape(n_t, N_HEADS, D_HEAD)
            m = jnp.max(s, axis=0)
            p = jnp.exp2(s - m[None])
            m_g.append(m)
            l_g.append(jnp.sum(p, axis=0))
            acc_g.append(jnp.sum(p * key_tiles(g, l, 1), axis=0))
        m_all = jnp.maximum(jnp.maximum(m_g[0], m_g[1]), m_g[2])
        w_g = [jnp.exp2(m - m_all) for m in m_g]
        num = w_g[0] * acc_g[0] + w_g[1] * acc_g[1] + w_g[2] * acc_g[2]
        den = w_g[0] * l_g[0] + w_g[1] * l_g[1] + w_g[2] * l_g[2]
        o_ref[l] = num / den * _silu(z_ref[l])


def _attn_sample(q, kvs, z, caches):
    b, n_new = q.shape[:2]
    assert n_new == SUBLANES
    (w0, d0), (w1, d1), (w2, d2) = DIL_GROUPS
    assert caches[0].shape[1] == w0 and caches[1].shape[1] == w1 and caches[2].shape[1] == w2
    assert d0 == 1 and n_new % d1 == 0 and d2 == 2 * n_new
    assert w0 // d0 + 1 == KEYS_PER_QUERY and w1 // d1 + 1 == KEYS_PER_QUERY
    assert w2 // d2 + 1 == KEYS_PER_QUERY
    c1 = caches[1].reshape(b, w1 // d1, d1, 2, N_HEADS, D_HEAD)
    c2 = caches[2].reshape(b, w2 // d2, d2, 2, N_HEADS, D_HEAD)
    tail = (2, N_HEADS, D_HEAD)
    new_spec = pl.BlockSpec((None, n_new) + tail, lambda i: (i, 0, 0, 0, 0))
    return pl.pallas_call(
        functools.partial(_attn_sample_kernel, n_new),
        grid=(b,),
        in_specs=[
            pl.BlockSpec((None, n_new, 3, N_HEADS, D_HEAD), lambda i: (i, 0, 0, 0, 0)),
            new_spec, new_spec, new_spec,
            pl.BlockSpec((None, n_new, N_HEADS, D_HEAD), lambda i: (i, 0, 0, 0)),
            pl.BlockSpec((None, w0) + tail, lambda i: (i, 0, 0, 0, 0)),
            pl.BlockSpec((None, w1 // d1, d1) + tail, lambda i: (i, 0, 0, 0, 0, 0)),
            pl.BlockSpec((None, w2 // d2, n_new) + tail, lambda i: (i, 0, 0, 0, 0, 0)),
        ],
        out_specs=pl.BlockSpec((None, n_new, N_HEADS, D_HEAD), lambda i: (i, 0, 0, 0)),
        out_shape=jax.ShapeDtypeStruct((b, n_new, N_HEADS, D_HEAD), F32),
        compiler_params=_cparams(("arbitrary",)),
        name="attn_sample",
    )(q, kvs[0], kvs[1], kvs[2], z, caches[0], c1, c2)


def kernel(x_prompt, x_sample, state_delta, state_conv, cache_kv_w128, cache_kv_w512, cache_kv_w2048,
           c_prompt, c_sample, norm_g, ada_w, ada_b, a_w_in, a_conv_w, a_A_log, a_dt_bias,
           a_out_norm_g, a_w_out, b_w_in, b_w_out, final_norm_g):
    bp, seq, _ = x_prompt.shape
    bs, n_new, _ = x_sample.shape
    n_s = bs * n_new

    mod = _ada_mod(jnp.concatenate([c_prompt, c_sample], axis=0), ada_w, ada_b)
    mod_p = [mod[l, :bp].reshape(bp, 1, 3 * D_MODEL) for l in range(2)]
    mod_s = [jnp.repeat(mod[l, bp:], n_new, axis=0).reshape(1, n_s, 3 * D_MODEL) for l in range(2)]

    w_main = a_w_in[0].astype(BF16)
    w_ab = jnp.pad(w_main[:, CONV_DIM + WIDTH:], ((0, 0), (0, LANES - 2 * N_HEADS)))
    w_out_a = a_w_out[0].astype(BF16)
    xs_flat = x_sample.reshape(1, n_s, D_MODEL)

    x1_p, h1_p, delta_p, conv_p = _gdn_layer_prompt(
        x_prompt, mod_p[0], mod_p[1], norm_g[0], norm_g[1], w_main, w_ab, w_out_a,
        a_conv_w[0], a_A_log[0], a_dt_bias[0], a_out_norm_g[0], 1024, 128)

    qkv_s, z_s, ab_s, abt_s = _project(xs_flat, mod_s[0], norm_g[0], w_main, (3, 1), n_s, w_ab)
    abt_s = abt_s.reshape(16, bs, n_new).transpose(1, 0, 2)
    o_s, delta_s, conv_s = _gdn_core_decode(
        qkv_s.reshape(bs, n_new, CONV_DIM), z_s.reshape(bs, n_new, WIDTH),
        ab_s.reshape(bs, n_new, LANES), abt_s, a_conv_w[0], a_A_log[0], a_dt_bias[0],
        a_out_norm_g[0], state_delta[0], state_conv[0], 8)
    x1_s = _out_project(xs_flat, o_s.reshape(1, n_s, WIDTH), mod_s[0], w_out_a, final_norm_g,
                        n_s, False)

    n_g = len(DIL_GROUPS)
    w_b = b_w_in[0].astype(BF16)
    w_out_b = b_w_out[0].astype(BF16)

    ob_p, kv0_p, kv1_p, kv2_p = _attn_layer_prompt(h1_p, w_b)
    y_p = _out_project(x1_p, ob_p, mod_p[1], w_out_b, final_norm_g, 512, True)

    q_s, k_s, v_s, zb_s = _project(x1_s, mod_s[1], norm_g[1], w_b, (n_g, n_g, n_g, 1), n_s)
    k_s = k_s.reshape(bs, n_new, n_g, N_HEADS, D_HEAD)
    v_s = v_s.reshape(bs, n_new, n_g, N_HEADS, D_HEAD)
    kvn = [jnp.stack([k_s[:, :, g], v_s[:, :, g]], axis=2) for g in range(n_g)]
    ob_s = _attn_sample(q_s.reshape(bs, n_new, n_g, N_HEADS, D_HEAD), kvn,
                        zb_s.reshape(bs, n_new, N_HEADS, D_HEAD),
                        (cache_kv_w128[0], cache_kv_w512[0], cache_kv_w2048[0]))
    y_s = _out_project(x1_s, ob_s.reshape(1, n_s, WIDTH), mod_s[1], w_out_b, final_norm_g,
                       n_s, True)

    def kv_prompt(kv):
        return kv.reshape(1, bp, kv.shape[1], 2, N_HEADS, D_HEAD)

    return (y_p, y_s.reshape(bs, n_new, D_MODEL),
            delta_p[None], delta_s[None], conv_p[None], conv_s[None],
            kv_prompt(kv0_p), kvn[0][None], kv_prompt(kv1_p), kvn[1][None],
            kv_prompt(kv2_p), kvn[2][None])
```

```python
import functools

import jax
import jax.numpy as jnp
from jax import lax
from jax.experimental import pallas as pl
from jax.experimental.pallas import tpu as pltpu

F32 = jnp.float32
BF16 = jnp.bfloat16

D_MODEL = 1024
N_HEADS = 8
D_HEAD = 128
WIDTH = N_HEADS * D_HEAD
CONV_W = 4
CONV_DIM = 3 * WIDTH
DIL_GROUPS = ((128, 1), (512, 4), (2048, 16))
KEYS_PER_QUERY = 129
EPS = 1e-6
ATTN_SCALE = D_HEAD ** -0.5
LOG2_E = 1.4426950408889634
NEG_BIG = -1e30

LANES = 128
SUBLANES = 8
VMEM_LIMIT_BYTES = 56 * 1024 * 1024


def _cparams(semantics):
    return pltpu.CompilerParams(dimension_semantics=semantics,
                                vmem_limit_bytes=VMEM_LIMIT_BYTES)


def _silu(x):
    return x * (1.0 / (1.0 + jnp.exp(-x)))


def _softplus(x):
    return jnp.maximum(x, 0.0) + jnp.log1p(jnp.exp(-jnp.abs(x)))


def _dot(a, b):
    return jnp.dot(a.astype(BF16), b.astype(BF16), preferred_element_type=F32)


def _dot_nt(a, b):
    return lax.dot_general(a.astype(BF16), b.astype(BF16), (((1,), (1,)), ((), ())),
                           preferred_element_type=F32)


def _dot_tn(a, b):
    return lax.dot_general(a.astype(BF16), b.astype(BF16), (((0,), (0,)), ((), ())),
                           preferred_element_type=F32)


def _split3(x):
    x1 = x.astype(BF16)
    r = x - x1.astype(F32)
    x2 = r.astype(BF16)
    x3 = (r - x2.astype(F32)).astype(BF16)
    return x1, x2, x3


def _dot_sel(sel, x):
    s = sel.astype(BF16)
    out = None
    for p in _split3(x):
        t = jnp.dot(s, p, preferred_element_type=F32)
        out = t if out is None else out + t
    return out


def _dot_x_sel(x, sel):
    s = sel.astype(BF16)
    out = None
    for p in _split3(x):
        t = jnp.dot(p, s, preferred_element_type=F32)
        out = t if out is None else out + t
    return out


def _mod_kernel(c_ref, w_ref, b_ref, o_ref):
    s = _silu(c_ref[...])
    o_ref[...] = _dot(s, w_ref[...]) + b_ref[...]


def _ada_mod(c_all, ada_w, ada_b):
    n_layers = ada_w.shape[0]
    rows = c_all.shape[0]
    tn = 768
    return pl.pallas_call(
        _mod_kernel,
        grid=(n_layers, 3 * D_MODEL // tn),
        in_specs=[
            pl.BlockSpec((rows, D_MODEL), lambda l, j: (0, 0)),
            pl.BlockSpec((None, D_MODEL, tn), lambda l, j: (l, 0, j)),
            pl.BlockSpec((None, 1, tn), lambda l, j: (l, 0, j)),
        ],
        out_specs=pl.BlockSpec((None, rows, tn), lambda l, j: (l, 0, j)),
        out_shape=jax.ShapeDtypeStruct((n_layers, rows, 3 * D_MODEL), F32),
        compiler_params=_cparams(("arbitrary", "arbitrary")),
        name="ada_mod",
    )(c_all, ada_w, ada_b.reshape(n_layers, 1, 3 * D_MODEL))


def _proj_kernel(seg_bounds, has_ab, *refs):
    x_ref, mod_ref, g_ref, w_ref = refs[:4]
    pos = 4
    if has_ab:
        wab_ref = refs[4]
        pos = 5
    n_seg = len(seg_bounds)
    seg_refs = refs[pos:pos + n_seg]
    pos += n_seg
    if has_ab:
        ab_ref, abt_ref = refs[pos:pos + 2]
        pos += 2
    h_ref = refs[pos]
    j = pl.program_id(2)

    @pl.when(j == 0)
    def _():
        hb = _mod_norm(x_ref[...], g_ref[...], mod_ref[...]).astype(BF16)
        h_ref[...] = hb
        if has_ab:
            ab = jnp.dot(hb, wab_ref[...], preferred_element_type=F32)
            ab_ref[...] = ab
            abt_ref[...] = ab.T[0:16, :]

    res = jnp.dot(h_ref[...], w_ref[...], preferred_element_type=F32)
    for (lo, hi), o_ref in zip(seg_bounds, seg_refs):
        @pl.when((j >= lo) & (j < hi))
        def _(o_ref=o_ref):
            o_ref[...] = res


def _project(x, mod, norm_g, w, seg_slabs, tm, w_ab=None):
    n, t, _ = x.shape
    r = mod.shape[1]
    slab = 1024
    n_slab = w.shape[1] // slab
    assert sum(seg_slabs) == n_slab and t % tm == 0
    has_ab = w_ab is not None
    bounds, lo = [], 0
    for s in seg_slabs:
        bounds.append((lo, lo + s))
        lo += s
    mod_rows = 1 if r == 1 else tm
    mod_map = (lambda b, i, j: (b, 0, 0)) if r == 1 else (lambda b, i, j: (b, i, 0))
    in_specs = [
        pl.BlockSpec((None, tm, D_MODEL), lambda b, i, j: (b, i, 0)),
        pl.BlockSpec((None, mod_rows, 3 * D_MODEL), mod_map),
        pl.BlockSpec((1, D_MODEL), lambda b, i, j: (0, 0)),
        pl.BlockSpec((D_MODEL, slab), lambda b, i, j: (0, j)),
    ]
    args = [x, mod, norm_g.reshape(1, D_MODEL), w]
    if has_ab:
        in_specs.append(pl.BlockSpec((D_MODEL, LANES), lambda b, i, j: (0, 0)))
        args.append(w_ab)
    out_specs, out_shapes = [], []
    for (lo, hi) in bounds:
        def seg_map(b, i, j, lo=lo, hi=hi):
            return (b, i, jnp.clip(j - lo, 0, hi - lo - 1))
        out_specs.append(pl.BlockSpec((None, tm, slab), seg_map))
        out_shapes.append(jax.ShapeDtypeStruct((n, t, slab * (hi - lo)), F32))
    if has_ab:
        out_specs += [pl.BlockSpec((None, tm, LANES), lambda b, i, j: (b, i, 0)),
                      pl.BlockSpec((None, 16, tm), lambda b, i, j: (b, 0, i))]
        out_shapes += [jax.ShapeDtypeStruct((n, t, LANES), F32),
                       jax.ShapeDtypeStruct((n, 16, t), F32)]
    return pl.pallas_call(
        functools.partial(_proj_kernel, tuple(bounds), has_ab),
        grid=(n, t // tm, n_slab),
        in_specs=in_specs,
        out_specs=out_specs,
        out_shape=out_shapes,
        scratch_shapes=[pltpu.VMEM((tm, D_MODEL), BF16)],
        compiler_params=_cparams(("arbitrary", "arbitrary", "arbitrary")),
        name="norm_proj",
    )(*args)


def _mod_norm(x, g, mod):
    ms = jnp.mean(x * x, axis=-1, keepdims=True)
    y = x * lax.rsqrt(ms + EPS) * g
    return y * (1.0 + mod[:, D_MODEL:2 * D_MODEL]) + mod[:, 0:D_MODEL]


def _out_kernel(final_norm, x_ref, o_ref, mod_ref, w_ref, fg_ref, out_ref):
    y = _dot(o_ref[...], w_ref[...])
    gate = mod_ref[:, 2 * D_MODEL:3 * D_MODEL]
    x = x_ref[...] + gate * y
    if final_norm:
        ms = jnp.mean(x * x, axis=-1, keepdims=True)
        x = x * lax.rsqrt(ms + EPS) * fg_ref[...]
    out_ref[...] = x


def _out_project(x, o, mod, w_out, final_g, tm, final_norm):
    n, t, _ = x.shape
    r = mod.shape[1]
    mod_rows = 1 if r == 1 else tm
    mod_map = (lambda b, i: (b, 0, 0)) if r == 1 else (lambda b, i: (b, i, 0))
    row_spec = pl.BlockSpec((None, tm, D_MODEL), lambda b, i: (b, i, 0))
    return pl.pallas_call(
        functools.partial(_out_kernel, final_norm),
        grid=(n, t // tm),
        in_specs=[
            row_spec,
            pl.BlockSpec((None, tm, WIDTH), lambda b, i: (b, i, 0)),
            pl.BlockSpec((None, mod_rows, 3 * D_MODEL), mod_map),
            pl.BlockSpec((WIDTH, D_MODEL), lambda b, i: (0, 0)),
            pl.BlockSpec((1, D_MODEL), lambda b, i: (0, 0)),
        ],
        out_specs=row_spec,
        out_shape=jax.ShapeDtypeStruct((n, t, D_MODEL), F32),
        compiler_params=_cparams(("arbitrary", "arbitrary")),
        name="out_proj",
    )(x, o, mod, w_out, final_g.reshape(1, D_MODEL))


def _unit_lower_inverses(n_mats, c):
    row = lax.broadcasted_iota(jnp.int32, (c, c), 0)
    col = lax.broadcasted_iota(jnp.int32, (c, c), 1)
    base = min(16, c)
    blk_id = lambda v, size: jnp.right_shift(v, size.bit_length() - 1)
    eye = (row == col).astype(F32)
    same = blk_id(row, base) == blk_id(col, base)
    ps = [jnp.where(same, -n, 0.0) for n in n_mats]
    invs = [eye + p for p in ps]
    assert base >= 4
    ps = [_dot(p, p).astype(BF16) for p in ps]
    power = 2
    while 2 * power < base:
        if c % LANES == 0:
            wide = [_dot(p, jnp.concatenate([p, inv.astype(BF16)], axis=1))
                    for p, inv in zip(ps, invs)]
            invs = [inv + w[:, c:] for inv, w in zip(invs, wide)]
            ps = [w[:, 0:c].astype(BF16) for w in wide]
        else:
            invs = [inv + _dot(p, inv) for p, inv in zip(ps, invs)]
            ps = [_dot(p, p).astype(BF16) for p in ps]
        power *= 2
    invs = [inv + _dot(p, inv) for p, inv in zip(ps, invs)]
    size = base
    while size < c:
        same_next = blk_id(row, 2 * size) == blk_id(col, 2 * size)
        off = same_next & jnp.logical_not(same)
        ls = [jnp.where(off, n, 0.0).astype(BF16) for n in n_mats]
        invs_b = [inv.astype(BF16) for inv in invs]
        ts = [_dot(ib, l) for ib, l in zip(invs_b, ls)]
        invs = [inv - _dot(t, ib) for inv, t, ib in zip(invs, ts, invs_b)]
        same = same_next
        size *= 2
    return invs


def _aligned(v, m):
    return v if isinstance(v, int) else pl.multiple_of(v, m)


EXT_SLABS = CONV_DIM // D_HEAD
EXT_LEAD = SUBLANES


def _ext_idx(slab, start, n):
    return (slab // 2, pl.ds(2 * start + slab % 2, n, stride=2), slice(None))


def _ext_shape(rows):
    return (EXT_SLABS // 2, 2 * (rows + EXT_LEAD), D_HEAD)


def _chunk_masks(c):
    row = lax.broadcasted_iota(jnp.int32, (c, c), 0)
    col = lax.broadcasted_iota(jnp.int32, (c, c), 1)
    tril_sel = jnp.where(row >= col, 1.0, 0.0).astype(BF16)
    triu_sel = jnp.where(row <= col, 1.0, 0.0).astype(BF16)
    return row >= col, row > col, tril_sel, triu_sel


def _problems(n_req):
    return [(i, h) for i in range(n_req) for h in range(N_HEADS)]


def _gdn_chunk_prepare(c, r0, masks, requests, cw_ref):
    tril, strict, tril_sel, triu_sel = masks
    pad = EXT_LEAD
    halo = CONV_W - 1
    heads = _problems(len(requests))
    gc_alls, gct_alls, beta_alls = [], [], []
    for _, _, ab_ref, g_ref, gt_ref, _ in requests:
        gc_alls.append(_dot_sel(tril_sel, g_ref[pl.ds(r0, c), :]))
        gct_alls.append(_dot_x_sel(gt_ref[0:N_HEADS, pl.ds(r0, c)], triu_sel))
        beta_alls.append(1.0 / (1.0 + jnp.exp(-ab_ref[pl.ds(r0, c), :])))

    def conv(off, i, h):
        slab = off // D_HEAD + h
        cols = slice(slab * D_HEAD, (slab + 1) * D_HEAD)
        acc = None
        for tap in range(CONV_W):
            rows = _ext_idx(slab, r0 + pad - halo + tap, c)
            term = requests[i][0][rows] * cw_ref[tap:tap + 1, cols]
            acc = term if acc is None else acc + term
        return _silu(acc)

    qs, ks, vs = [], [], []
    for i, h in heads:
        q = conv(0, i, h)
        k = conv(WIDTH, i, h)
        qs.append(q * lax.rsqrt(jnp.sum(q * q, axis=-1, keepdims=True) + EPS)
                  * (D_HEAD ** -0.5))
        ks.append(k * lax.rsqrt(jnp.sum(k * k, axis=-1, keepdims=True) + EPS))
        vs.append(conv(2 * WIDTH, i, h))
    gcs = [jnp.broadcast_to(gc_alls[i][:, h:h + 1], (c, D_HEAD)) for i, h in heads]
    betas = [jnp.broadcast_to(beta_alls[i][:, N_HEADS + h:N_HEADS + h + 1], (c, D_HEAD))
             for i, h in heads]
    g_lasts = [gc[c - 1:c, :] for gc in gcs]
    decays = [jnp.where(tril, jnp.exp2(gc[:, 0:c] - gct_alls[i][h:h + 1, :]), 0.0)
              for gc, (i, h) in zip(gcs, heads)]
    kbs = [k * beta for k, beta in zip(ks, betas)]
    ks_b = [k.astype(BF16) for k in ks]
    kks = [_dot_nt(kb, k) for kb, k in zip(kbs, ks_b)]
    qks = [_dot_nt(q, k) for q, k in zip(qs, ks_b)]
    n_mats = [jnp.where(strict, kk * decay, 0.0) for kk, decay in zip(kks, decays)]
    qks = [(qk * decay).astype(BF16) for qk, decay in zip(qks, decays)]
    invs = _unit_lower_inverses(n_mats, c)
    e_gcs = [jnp.exp2(gc) for gc in gcs]
    rhss = [jnp.concatenate([v * beta, kb * e_gc], axis=1).astype(BF16)
            for v, beta, kb, e_gc in zip(vs, betas, kbs, e_gcs)]
    sols = [_dot(inv, rhs) for inv, rhs in zip(invs, rhss)]
    q_decs = [(q * e_gc).astype(BF16) for q, e_gc in zip(qs, e_gcs)]
    k_decs = [(k * jnp.exp2(g_last - gc)).astype(BF16)
              for k, g_last, gc in zip(ks, g_lasts, gcs)]
    return [(sol[:, 0:D_HEAD], sol[:, D_HEAD:2 * D_HEAD].astype(BF16), qk, q_dec, k_dec,
             jnp.exp2(g_last))
            for sol, qk, q_dec, k_dec, g_last in zip(sols, qks, q_decs, k_decs, g_lasts)]


def _gdn_chunk_apply(c, r0, prepared, requests, og_ref):
    n_req = len(requests)
    heads = _problems(n_req)
    s_olds = [requests[i][5][h] for i, h in heads]
    s_bs = [s.astype(BF16) for s in s_olds]
    on_s = [_dot(jnp.concatenate([k_cum, q_dec], axis=0), s_b)
            for (_, k_cum, _, q_dec, _, _), s_b in zip(prepared, s_bs)]
    us = [w - ks[0:c] for (w, _, _, _, _, _), ks in zip(prepared, on_s)]
    us_b = [u.astype(BF16) for u in us]
    os_ = [ks[c:2 * c] + _dot(qk, u_b)
           for (_, _, qk, _, _, _), ks, u_b in zip(prepared, on_s, us_b)]
    for n, (i, h) in enumerate(heads):
        k_dec, s_decay = prepared[n][4], prepared[n][5]
        requests[i][5][h] = s_olds[n] * s_decay + _dot_tn(k_dec, us_b[n])
    outs = [[] for _ in range(n_req)]
    for n, (i, h) in enumerate(heads):
        o = os_[n]
        o = o * lax.rsqrt(jnp.mean(o * o, axis=-1, keepdims=True) + EPS) * og_ref[...]
        z = requests[i][1][pl.ds(r0, c), h * D_HEAD:(h + 1) * D_HEAD]
        outs[i].append(o * _silu(z))
    return outs


def _decay_logits(ab, alog, dtb):
    return (-LOG2_E * jnp.exp(alog)) * _softplus(ab + dtb)


def _gdn_decode_kernel(c, n_req, qkv_ref, z_ref, ab_ref, abt_ref, cw_ref, alog_ref, dtb_ref,
                       alogc_ref, dtbc_ref, og_ref, s0_ref, conv0_ref,
                       o_ref, s_ref, conv_ref, ext_ref, g_ref, gt_ref):
    halo = CONV_W - 1
    s_ref[...] = s0_ref[...]
    lead_fill = jnp.zeros((EXT_LEAD - halo, D_HEAD), F32)
    for i in range(n_req):
        for slab in range(EXT_SLABS):
            cols = slice(slab * D_HEAD, (slab + 1) * D_HEAD)
            ext_ref[(i,) + _ext_idx(slab, 0, EXT_LEAD)] = jnp.concatenate(
                [lead_fill, conv0_ref[i, :, cols]], axis=0)
            ext_ref[(i,) + _ext_idx(slab, EXT_LEAD, c)] = qkv_ref[i, :, cols]
    g_ref[...] = _decay_logits(ab_ref[...], alog_ref[...], dtb_ref[...])
    gt_ref[...] = _decay_logits(abt_ref[...], alogc_ref[...], dtbc_ref[...])
    requests = [(ext_ref.at[i], z_ref.at[i], ab_ref.at[i], g_ref.at[i], gt_ref.at[i], s_ref.at[i])
                for i in range(n_req)]
    prepared = _gdn_chunk_prepare(c, 0, _chunk_masks(c), requests, cw_ref)
    outs = _gdn_chunk_apply(c, 0, prepared, requests, og_ref)
    for i in range(n_req):
        for h, o in enumerate(outs[i]):
            o_ref[i, :, h * D_HEAD:(h + 1) * D_HEAD] = o
        for slab in range(EXT_SLABS):
            last = ext_ref[(i,) + _ext_idx(slab, c, EXT_LEAD)]
            conv_ref[i, :, slab * D_HEAD:(slab + 1) * D_HEAD] = last[EXT_LEAD - halo:]


def _gdn_core_decode(qkv, z, ab, abt, conv_w, a_log, dt_bias, out_g, s0, conv0, n_req):
    b, c, _ = qkv.shape
    assert b % n_req == 0
    const = lambda shape: pl.BlockSpec(shape, lambda i: (0,) * len(shape))
    per_req = lambda *tail: pl.BlockSpec((n_req,) + tail, lambda i: (i,) + (0,) * len(tail))
    return pl.pallas_call(
        functools.partial(_gdn_decode_kernel, c, n_req),
        grid=(b // n_req,),
        in_specs=[
            per_req(c, CONV_DIM), per_req(c, WIDTH), per_req(c, LANES), per_req(16, c),
            const((CONV_W, CONV_DIM)), const((1, LANES)), const((1, LANES)),
            const((16, 1)), const((16, 1)), const((1, D_HEAD)),
            per_req(N_HEADS, D_HEAD, D_HEAD), per_req(CONV_W - 1, CONV_DIM),
        ],
        out_specs=[per_req(c, WIDTH), per_req(N_HEADS, D_HEAD, D_HEAD),
                   per_req(CONV_W - 1, CONV_DIM)],
        out_shape=[
            jax.ShapeDtypeStruct((b, c, WIDTH), F32),
            jax.ShapeDtypeStruct((b, N_HEADS, D_HEAD, D_HEAD), F32),
            jax.ShapeDtypeStruct((b, CONV_W - 1, CONV_DIM), F32),
        ],
        scratch_shapes=[
            pltpu.VMEM((n_req,) + _ext_shape(c), F32),
            pltpu.VMEM((n_req, c, LANES), F32),
            pltpu.VMEM((n_req, 16, c), F32),
        ],
        compiler_params=_cparams(("arbitrary",)),
        name="gdn_core",
    )(qkv, z, ab, abt, conv_w, *_head_params(a_log, dt_bias), out_g.reshape(1, D_HEAD),
      s0, conv0)


def _head_params(a_log, dt_bias):
    row = lambda v: jnp.pad(v.reshape(1, N_HEADS), ((0, 0), (0, LANES - N_HEADS)))
    col = lambda v: jnp.pad(v.reshape(N_HEADS, 1), ((0, 16 - N_HEADS), (0, 0)))
    return row(a_log), row(dt_bias), col(a_log), col(dt_bias)


def _gdn_layer_kernel(tm, c, x_ref, mod_ref, modn_ref, g_ref, gn_ref, w_ref, wab_ref,
                      wout_ref, cw_ref, alog_ref, dtb_ref, alogc_ref, dtbc_ref, og_ref,
                      x1_ref, h1_ref, s_ref, conv_ref, ext_ref, z_s, ab_s, gl_s, gt_s):
    t = pl.program_id(1)
    pad = EXT_LEAD
    halo = CONV_W - 1
    n_chunks = tm // c
    n_col = 512
    request = (ext_ref, z_s, ab_s, gl_s, gt_s, s_ref)

    @pl.when(t == 0)
    def _():
        s_ref[...] = jnp.zeros_like(s_ref)
        ext_ref[:, 0:2 * pad, :] = jnp.zeros((EXT_SLABS // 2, 2 * pad, D_HEAD), F32)

    def project(r0):
        hb = _mod_norm(x_ref[pl.ds(r0, c), :], g_ref[...], mod_ref[...]).astype(BF16)
        for lo in range(0, CONV_DIM + WIDTH, n_col):
            res = jnp.dot(hb, w_ref[:, lo:lo + n_col], preferred_element_type=F32)
            if lo < CONV_DIM:
                for j in range(n_col // D_HEAD):
                    ext_ref[_ext_idx(lo // D_HEAD + j, pad + r0, c)] = (
                        res[:, j * D_HEAD:(j + 1) * D_HEAD])
            else:
                z_s[pl.ds(r0, c), lo - CONV_DIM:lo - CONV_DIM + n_col] = res
        ab = jnp.dot(hb, wab_ref[...], preferred_element_type=F32)
        ab_s[pl.ds(r0, c), :] = ab
        gl_s[pl.ds(r0, c), :] = _decay_logits(ab, alog_ref[...], dtb_ref[...])
        gt_s[:, pl.ds(r0, c)] = _decay_logits(ab.T[0:16, :], alogc_ref[...], dtbc_ref[...])

    masks = _chunk_masks(c)
    project(0)

    def step(i, carry):
        r0 = pl.multiple_of(i * c, c)
        prepared = _gdn_chunk_prepare(c, r0, masks, [request], cw_ref)
        outs, = _gdn_chunk_apply(c, r0, prepared, [request], og_ref)
        o = jnp.concatenate([o.astype(BF16) for o in outs], axis=1)
        y = jnp.dot(o, wout_ref[...], preferred_element_type=F32)
        x1 = x_ref[pl.ds(r0, c), :] + mod_ref[:, 2 * D_MODEL:3 * D_MODEL] * y
        x1_ref[pl.ds(r0, c), :] = x1
        h1_ref[pl.ds(r0, c), :] = _mod_norm(x1, gn_ref[...], modn_ref[...]).astype(BF16)
        project(pl.multiple_of(jnp.minimum(i + 1, n_chunks - 1) * c, c))
        return carry

    lax.fori_loop(0, n_chunks, step, 0)

    for slab in range(EXT_SLABS):
        last = ext_ref[_ext_idx(slab, tm, pad)]
        ext_ref[_ext_idx(slab, 0, pad)] = last
        conv_ref[:, slab * D_HEAD:(slab + 1) * D_HEAD] = last[pad - halo:]


def _gdn_layer_prompt(x, mod, mod_next, norm_g, norm_g_next, w_main, w_ab, w_out, conv_w,
                      a_log, dt_bias, out_g, tm, c):
    b, t, _ = x.shape
    const = lambda shape: pl.BlockSpec(shape, lambda i, j: (0,) * len(shape))
    resident = lambda shape: pl.BlockSpec(shape, lambda i, j: (0,) * len(shape),
                                          pipeline_mode=pl.Buffered(1))
    mod_spec = pl.BlockSpec((None, 1, 3 * D_MODEL), lambda i, j: (i, 0, 0))
    row_spec = pl.BlockSpec((None, tm, D_MODEL), lambda i, j: (i, j, 0))
    return pl.pallas_call(
        functools.partial(_gdn_layer_kernel, tm, c),
        grid=(b, t // tm),
        in_specs=[
            row_spec, mod_spec, mod_spec, const((1, D_MODEL)), const((1, D_MODEL)),
            resident((D_MODEL, CONV_DIM + WIDTH)), const((D_MODEL, LANES)),
            resident((WIDTH, D_MODEL)), const((CONV_W, CONV_DIM)),
            const((1, LANES)), const((1, LANES)), const((16, 1)), const((16, 1)),
            const((1, D_HEAD)),
        ],
        out_specs=[
            row_spec, row_spec,
            pl.BlockSpec((None, N_HEADS, D_HEAD, D_HEAD), lambda i, j: (i, 0, 0, 0)),
            pl.BlockSpec((None, CONV_W - 1, CONV_DIM), lambda i, j: (i, 0, 0)),
        ],
        out_shape=[
            jax.ShapeDtypeStruct((b, t, D_MODEL), F32),
            jax.ShapeDtypeStruct((b, t, D_MODEL), BF16),
            jax.ShapeDtypeStruct((b, N_HEADS, D_HEAD, D_HEAD), F32),
            jax.ShapeDtypeStruct((b, CONV_W - 1, CONV_DIM), F32),
        ],
        scratch_shapes=[
            pltpu.VMEM(_ext_shape(tm), F32),
            pltpu.VMEM((tm, WIDTH), F32),
            pltpu.VMEM((tm, LANES), F32),
            pltpu.VMEM((tm, LANES), F32),
            pltpu.VMEM((16, tm), F32),
        ],
        compiler_params=_cparams(("arbitrary", "arbitrary")),
        name="gdn_layer",
    )(x, mod, mod_next, norm_g.reshape(1, D_MODEL), norm_g_next.reshape(1, D_MODEL),
      w_main, w_ab, w_out, conv_w, *_head_params(a_log, dt_bias), out_g.reshape(1, D_HEAD))


def _banded_attention(dil, seq, q_ref, k_ref, v_ref, acc_ref, m_ref, l_ref, normalized):
    blk = D_HEAD
    n_blk = seq // dil // blk
    with_prev = n_blk > 1
    per_step = 16 if with_prev else 8
    n_keys = 2 * blk if with_prev else blk
    row = lax.broadcasted_iota(jnp.int32, (blk, n_keys), 0)
    col = lax.broadcasted_iota(jnp.int32, (blk, n_keys), 1)
    is_prev = col < (n_keys - blk)
    slack = jnp.where(is_prev, col - row, row + (n_keys - blk) - col)
    cur_ok = slack >= 0
    ones = jnp.ones((n_keys, blk), BF16)

    def rows(start):
        if dil == 1:
            return pl.ds(start, blk)
        return pl.ds(start, blk, stride=dil)

    def body(i, carry):
        starts, masks, qs, kcs, vcs = [], [], [], [], []
        for u in range(per_step):
            idx = i * per_step + u
            r = idx // n_blk
            n = idx % n_blk
            start = r + dil * blk * n
            starts.append(start)
            qs.append(q_ref[rows(start), :].astype(BF16))
            k_cur = k_ref[rows(start), :].astype(BF16)
            v_cur = v_ref[rows(start), :].astype(BF16)
            if with_prev:
                if u % min(n_blk, per_step) != 0:
                    k_prev, v_prev = k_last, v_last
                else:
                    start_p = r + dil * blk * jnp.maximum(n - 1, 0)
                    k_prev = k_ref[rows(start_p), :].astype(BF16)
                    v_prev = v_ref[rows(start_p), :].astype(BF16)
                k_last, v_last = k_cur, v_cur
                first = jnp.where(n > 0, 0, 2 * blk)
                masks.append(slack - jnp.where(is_prev, first, 0) >= 0)
                k_cur = jnp.concatenate([k_prev, k_cur], axis=0)
                v_cur = jnp.concatenate([v_prev, v_cur], axis=0)
            else:
                masks.append(cur_ok)
            kcs.append(k_cur)
            vcs.append(jnp.concatenate([v_cur, ones], axis=1))
        ss = [jnp.where(mask, _dot_nt(q, kc), NEG_BIG) for q, kc, mask in zip(qs, kcs, masks)]
        ms = [jnp.max(s, axis=-1, keepdims=True) for s in ss]
        ps = [jnp.exp2(s - m).astype(BF16) for s, m in zip(ss, ms)]
        accs = [jnp.dot(p, vc, preferred_element_type=F32) for p, vc in zip(ps, vcs)]
        for start, acc, m in zip(starts, accs, ms):
            if normalized:
                l = acc[:, blk:2 * blk]
                acc_ref[rows(start), :] = acc[:, 0:blk] / l
                m_ref[rows(start), :] = m + jnp.log2(l)
            else:
                acc_ref[rows(start), :] = acc[:, 0:blk]
                l_ref[rows(start), :] = acc[:, blk:2 * blk]
                m_ref[rows(start), :] = jnp.broadcast_to(m, (blk, blk))
        return carry

    lax.fori_loop(0, dil * n_blk // per_step, body, 0)


def _attn_layer_kernel(seq, h_ref, *refs):
    n_g = len(DIL_GROUPS)
    w_refs = refs[:3 * n_g + 1]
    (o_ref, kv0_ref, kv1_ref, kv2_ref, q_s, k_s, v_s, z_s, acc_s, m_s, l_s,
     sems) = refs[3 * n_g + 1:]
    b = pl.program_id(0)
    hp = pl.program_id(1)
    kv_refs = (kv0_ref, kv1_ref, kv2_ref)

    def proj(slab):
        return jnp.dot(h_ref[...], w_refs[slab][...], preferred_element_type=F32)

    def kv_copy(g, i, kv):
        keep = kv_refs[g].shape[1]
        src = (k_s, v_s)[kv].at[g, i, pl.ds(seq - keep, keep), :]
        col = pl.multiple_of(kv * WIDTH + (2 * hp + i) * D_HEAD, D_HEAD)
        dst = kv_refs[g].at[b, :, pl.ds(col, D_HEAD)]
        return pltpu.make_async_copy(src, dst, sems.at[(g * 2 + i) * 2 + kv])

    for g in range(n_g):
        k = proj(n_g + g)
        v = proj(2 * n_g + g)
        for i in range(2):
            k_s[g, i] = k[:, i * D_HEAD:(i + 1) * D_HEAD]
            v_s[g, i] = v[:, i * D_HEAD:(i + 1) * D_HEAD]
        for i in range(2):
            kv_copy(g, i, 0).start()
            kv_copy(g, i, 1).start()
        q = proj(g) * (ATTN_SCALE * LOG2_E)
        for i in range(2):
            q_s[g, i] = q[:, i * D_HEAD:(i + 1) * D_HEAD]

    normalized = [dil % 16 == 0 for _, dil in DIL_GROUPS]
    for i in range(2):
        for g, (_, dil) in enumerate(DIL_GROUPS):
            _banded_attention(dil, seq, q_s.at[g, i], k_s.at[g, i], v_s.at[g, i],
                              acc_s.at[g], m_s.at[g], l_s.at[g], normalized[g])
        if i == 0:
            z = proj(3 * n_g)
            for j in range(2):
                z_s[j] = z[:, j * D_HEAD:(j + 1) * D_HEAD]
        m_all = jnp.maximum(jnp.maximum(m_s[0], m_s[1]), m_s[2])
        num = jnp.zeros((seq, D_HEAD), F32)
        den = jnp.zeros((seq, D_HEAD), F32)
        for g in range(n_g):
            w = jnp.exp2(m_s[g] - m_all)
            num = num + w * acc_s[g]
            den = den + (w if normalized[g] else w * l_s[g])
        o_ref[:, i * D_HEAD:(i + 1) * D_HEAD] = (num / den * _silu(z_s[i])).astype(BF16)

    for g in range(n_g):
        for i in range(2):
            kv_copy(g, i, 0).wait()
            kv_copy(g, i, 1).wait()


def _attn_layer_prompt(h, w):
    b, seq, _ = h.shape
    n_pair = N_HEADS // 2
    n_g = len(DIL_GROUPS)
    pair = 2 * D_HEAD
    keeps = [min(window, seq) for window, _ in DIL_GROUPS]
    slab = lambda n: pltpu.VMEM((n, seq, D_HEAD), F32)
    w_specs = [pl.BlockSpec((D_MODEL, pair), lambda i, p, s=s: (0, s * n_pair + p))
               for s in range(3 * n_g + 1)]
    return pl.pallas_call(
        functools.partial(_attn_layer_kernel, seq),
        grid=(b, n_pair),
        in_specs=[
            pl.BlockSpec((None, seq, D_MODEL), lambda i, p: (i, 0, 0),
                         pipeline_mode=pl.Buffered(1)),
        ] + w_specs,
        out_specs=[pl.BlockSpec((None, seq, 2 * D_HEAD), lambda i, p: (i, 0, p))]
        + [pl.BlockSpec(memory_space=pl.ANY)] * n_g,
        out_shape=[jax.ShapeDtypeStruct((b, seq, WIDTH), BF16)]
        + [jax.ShapeDtypeStruct((b, keep, 2 * WIDTH), F32) for keep in keeps],
        scratch_shapes=[
            pltpu.VMEM((n_g, 2, seq, D_HEAD), F32),
            pltpu.VMEM((n_g, 2, seq, D_HEAD), F32),
            pltpu.VMEM((n_g, 2, seq, D_HEAD), F32),
            slab(2),
            slab(n_g), slab(n_g), slab(n_g),
            pltpu.SemaphoreType.DMA((n_g * 2 * 2,)),
        ],
        compiler_params=_cparams(("arbitrary", "arbitrary")),
        name="attn_layer",
    )(h, *([w] * (3 * n_g + 1)))


def _attn_sample_kernel(n_new, *refs):
    q_ref, kv0_ref, kv1_ref, kv2_ref, z_ref, c0_ref, c1_ref, c2_ref, o_ref = refs
    n_t = KEYS_PER_QUERY
    ones = jnp.ones((D_HEAD, D_HEAD), BF16)
    dil1 = DIL_GROUPS[1][1]

    def key_tiles(g, l, kv):
        if g == 0:
            return jnp.concatenate([c0_ref[l:, kv], kv0_ref[0:l + 1, kv]], axis=0)
        if g == 1:
            a, r = divmod(l, dil1)
            parts = [c1_ref[a:, r, kv]]
            if a:
                parts.append(kv1_ref[r:r + 1, kv])
            parts.append(kv1_ref[l:l + 1, kv])
            return jnp.concatenate(parts, axis=0)
        return jnp.concatenate([c2_ref[:, l, kv], kv2_ref[l:l + 1, kv]], axis=0)

    for l in range(n_new):
        m_g, l_g, acc_g = [], [], []
        for g in range(len(DIL_GROUPS)):
            q = q_ref[l, g] * (ATTN_SCALE * LOG2_E)
            prod = (key_tiles(g, l, 0) * q[None]).reshape(n_t * N_HEADS, D_HEAD)
            s = jnp.dot(prod.astype(BF16), ones, preferred_element_type=F32)
            s = s.reshape(n_t, N_HEADS, D_HEAD)
            m = jnp.max(s, axis=0)
            p = jnp.exp2(s - m[None])
            m_g.append(m)
            l_g.append(jnp.sum(p, axis=0))
            acc_g.append(jnp.sum(p * key_tiles(g, l, 1), axis=0))
        m_all = jnp.maximum(jnp.maximum(m_g[0], m_g[1]), m_g[2])
        w_g = [jnp.exp2(m - m_all) for m in m_g]
        num = w_g[0] * acc_g[0] + w_g[1] * acc_g[1] + w_g[2] * acc_g[2]
        den = w_g[0] * l_g[0] + w_g[1] * l_g[1] + w_g[2] * l_g[2]
        o_ref[l] = num / den * _silu(z_ref[l])


def _attn_sample(q, kvs, z, caches):
    b, n_new = q.shape[:2]
    assert n_new == SUBLANES
    (w0, d0), (w1, d1), (w2, d2) = DIL_GROUPS
    assert caches[0].shape[1] == w0 and caches[1].shape[1] == w1 and caches[2].shape[1] == w2
    assert d0 == 1 and n_new % d1 == 0 and d2 == 2 * n_new
    assert w0 // d0 + 1 == KEYS_PER_QUERY and w1 // d1 + 1 == KEYS_PER_QUERY
    assert w2 // d2 + 1 == KEYS_PER_QUERY
    c1 = caches[1].reshape(b, w1 // d1, d1, 2, N_HEADS, D_HEAD)
    c2 = caches[2].reshape(b, w2 // d2, d2, 2, N_HEADS, D_HEAD)
    tail = (2, N_HEADS, D_HEAD)
    new_spec = pl.BlockSpec((None, n_new) + tail, lambda i: (i, 0, 0, 0, 0))
    return pl.pallas_call(
        functools.partial(_attn_sample_kernel, n_new),
        grid=(b,),
        in_specs=[
            pl.BlockSpec((None, n_new, 3, N_HEADS, D_HEAD), lambda i: (i, 0, 0, 0, 0)),
            new_spec, new_spec, new_spec,
            pl.BlockSpec((None, n_new, N_HEADS, D_HEAD), lambda i: (i, 0, 0, 0)),
            pl.BlockSpec((None, w0) + tail, lambda i: (i, 0, 0, 0, 0)),
            pl.BlockSpec((None, w1 // d1, d1) + tail, lambda i: (i, 0, 0, 0, 0, 0)),
            pl.BlockSpec((None, w2 // d2, n_new) + tail, lambda i: (i, 0, 0, 0, 0, 0)),
        ],
        out_specs=pl.BlockSpec((None, n_new, N_HEADS, D_HEAD), lambda i: (i, 0, 0, 0)),
        out_shape=jax.ShapeDtypeStruct((b, n_new, N_HEADS, D_HEAD), F32),
        compiler_params=_cparams(("arbitrary",)),
        name="attn_sample",
    )(q, kvs[0], kvs[1], kvs[2], z, caches[0], c1, c2)


def kernel(x_prompt, x_sample, state_delta, state_conv, cache_kv_w128, cache_kv_w512, cache_kv_w2048,
           c_prompt, c_sample, norm_g, ada_w, ada_b, a_w_in, a_conv_w, a_A_log, a_dt_bias,
           a_out_norm_g, a_w_out, b_w_in, b_w_out, final_norm_g):
    bp, seq, _ = x_prompt.shape
    bs, n_new, _ = x_sample.shape
    n_s = bs * n_new

    mod = _ada_mod(jnp.concatenate([c_prompt, c_sample], axis=0), ada_w, ada_b)
    mod_p = [mod[l, :bp].reshape(bp, 1, 3 * D_MODEL) for l in range(2)]
    mod_s = [jnp.repeat(mod[l, bp:], n_new, axis=0).reshape(1, n_s, 3 * D_MODEL) for l in range(2)]

    w_main = a_w_in[0].astype(BF16)
    w_ab = jnp.pad(w_main[:, CONV_DIM + WIDTH:], ((0, 0), (0, LANES - 2 * N_HEADS)))
    w_out_a = a_w_out[0].astype(BF16)
    xs_flat = x_sample.reshape(1, n_s, D_MODEL)

    x1_p, h1_p, delta_p, conv_p = _gdn_layer_prompt(
        x_prompt, mod_p[0], mod_p[1], norm_g[0], norm_g[1], w_main, w_ab, w_out_a,
        a_conv_w[0], a_A_log[0], a_dt_bias[0], a_out_norm_g[0], 1024, 128)

    qkv_s, z_s, ab_s, abt_s = _project(xs_flat, mod_s[0], norm_g[0], w_main, (3, 1), n_s, w_ab)
    abt_s = abt_s.reshape(16, bs, n_new).transpose(1, 0, 2)
    o_s, delta_s, conv_s = _gdn_core_decode(
        qkv_s.reshape(bs, n_new, CONV_DIM), z_s.reshape(bs, n_new, WIDTH),
        ab_s.reshape(bs, n_new, LANES), abt_s, a_conv_w[0], a_A_log[0], a_dt_bias[0],
        a_out_norm_g[0], state_delta[0], state_conv[0], 8)
    x1_s = _out_project(xs_flat, o_s.reshape(1, n_s, WIDTH), mod_s[0], w_out_a, final_norm_g,
                        n_s, False)

    n_g = len(DIL_GROUPS)
    w_b = b_w_in[0].astype(BF16)
    w_out_b = b_w_out[0].astype(BF16)

    ob_p, kv0_p, kv1_p, kv2_p = _attn_layer_prompt(h1_p, w_b)
    y_p = _out_project(x1_p, ob_p, mod_p[1], w_out_b, final_norm_g, 512, True)

    q_s, k_s, v_s, zb_s = _project(x1_s, mod_s[1], norm_g[1], w_b, (n_g, n_g, n_g, 1), n_s)
    k_s = k_s.reshape(bs, n_new, n_g, N_HEADS, D_HEAD)
    v_s = v_s.reshape(bs, n_new, n_g, N_HEADS, D_HEAD)
    kvn = [jnp.stack([k_s[:, :, g], v_s[:, :, g]], axis=2) for g in range(n_g)]
    ob_s = _attn_sample(q_s.reshape(bs, n_new, n_g, N_HEADS, D_HEAD), kvn,
                        zb_s.reshape(bs, n_new, N_HEADS, D_HEAD),
                        (cache_kv_w128[0], cache_kv_w512[0], cache_kv_w2048[0]))
    y_s = _out_project(x1_s, ob_s.reshape(1, n_s, WIDTH), mod_s[1], w_out_b, final_norm_g,
                       n_s, True)

    def kv_prompt(kv):
        return kv.reshape(1, bp, kv.shape[1], 2, N_HEADS, D_HEAD)

    return (y_p, y_s.reshape(bs, n_new, D_MODEL),
            delta_p[None], delta_s[None], conv_p[None], conv_s[None],
            kv_prompt(kv0_p), kvn[0][None], kv_prompt(kv1_p), kvn[1][None],
            kv_prompt(kv2_p), kvn[2][None])
```

```python
import functools

import jax
import jax.numpy as jnp
from jax import lax
from jax.experimental import pallas as pl
from jax.experimental.pallas import tpu as pltpu

F32 = jnp.float32
BF16 = jnp.bfloat16

D_MODEL = 1024
N_HEADS = 8
D_HEAD = 128
WIDTH = N_HEADS * D_HEAD
CONV_W = 4
CONV_DIM = 3 * WIDTH
DIL_GROUPS = ((128, 1), (512, 4), (2048, 16))
KEYS_PER_QUERY = 129
EPS = 1e-6
ATTN_SCALE = D_HEAD ** -0.5
LOG2_E = 1.4426950408889634
NEG_BIG = -1e30

LANES = 128
SUBLANES = 8
VMEM_LIMIT_BYTES = 56 * 1024 * 1024


def _cparams(semantics):
    return pltpu.CompilerParams(dimension_semantics=semantics,
                                vmem_limit_bytes=VMEM_LIMIT_BYTES)


def _silu(x):
    return x * (1.0 / (1.0 + jnp.exp(-x)))


def _softplus(x):
    return jnp.maximum(x, 0.0) + jnp.log1p(jnp.exp(-jnp.abs(x)))


def _dot(a, b):
    return jnp.dot(a.astype(BF16), b.astype(BF16), preferred_element_type=F32)


def _dot_nt(a, b):
    return lax.dot_general(a.astype(BF16), b.astype(BF16), (((1,), (1,)), ((), ())),
                           preferred_element_type=F32)


def _dot_tn(a, b):
    return lax.dot_general(a.astype(BF16), b.astype(BF16), (((0,), (0,)), ((), ())),
                           preferred_element_type=F32)


def _split3(x):
    x1 = x.astype(BF16)
    r = x - x1.astype(F32)
    x2 = r.astype(BF16)
    x3 = (r - x2.astype(F32)).astype(BF16)
    return x1, x2, x3


def _dot_sel(sel, x):
    s = sel.astype(BF16)
    out = None
    for p in _split3(x):
        t = jnp.dot(s, p, preferred_element_type=F32)
        out = t if out is None else out + t
    return out


def _dot_x_sel(x, sel):
    s = sel.astype(BF16)
    out = None
    for p in _split3(x):
        t = jnp.dot(p, s, preferred_element_type=F32)
        out = t if out is None else out + t
    return out


def _mod_kernel(c_ref, w_ref, b_ref, o_ref):
    s = _silu(c_ref[...])
    o_ref[...] = _dot(s, w_ref[...]) + b_ref[...]


def _ada_mod(c_all, ada_w, ada_b):
    n_layers = ada_w.shape[0]
    rows = c_all.shape[0]
    tn = 768
    return pl.pallas_call(
        _mod_kernel,
        grid=(n_layers, 3 * D_MODEL // tn),
        in_specs=[
            pl.BlockSpec((rows, D_MODEL), lambda l, j: (0, 0)),
            pl.BlockSpec((None, D_MODEL, tn), lambda l, j: (l, 0, j)),
            pl.BlockSpec((None, 1, tn), lambda l, j: (l, 0, j)),
        ],
        out_specs=pl.BlockSpec((None, rows, tn), lambda l, j: (l, 0, j)),
        out_shape=jax.ShapeDtypeStruct((n_layers, rows, 3 * D_MODEL), F32),
        compiler_params=_cparams(("arbitrary", "arbitrary")),
        name="ada_mod",
    )(c_all, ada_w, ada_b.reshape(n_layers, 1, 3 * D_MODEL))


def _proj_kernel(seg_bounds, has_ab, *refs):
    x_ref, mod_ref, g_ref, w_ref = refs[:4]
    pos = 4
    if has_ab:
        wab_ref = refs[4]
        pos = 5
    n_seg = len(seg_bounds)
    seg_refs = refs[pos:pos + n_seg]
    pos += n_seg
    if has_ab:
        ab_ref, abt_ref = refs[pos:pos + 2]
        pos += 2
    h_ref = refs[pos]
    j = pl.program_id(2)

    @pl.when(j == 0)
    def _():
        hb = _mod_norm(x_ref[...], g_ref[...], mod_ref[...]).astype(BF16)
        h_ref[...] = hb
        if has_ab:
            ab = jnp.dot(hb, wab_ref[...], preferred_element_type=F32)
            ab_ref[...] = ab
            abt_ref[...] = ab.T[0:16, :]

    res = jnp.dot(h_ref[...], w_ref[...], preferred_element_type=F32)
    for (lo, hi), o_ref in zip(seg_bounds, seg_refs):
        @pl.when((j >= lo) & (j < hi))
        def _(o_ref=o_ref):
            o_ref[...] = res


def _project(x, mod, norm_g, w, seg_slabs, tm, w_ab=None):
    n, t, _ = x.shape
    r = mod.shape[1]
    slab = 1024
    n_slab = w.shape[1] // slab
    assert sum(seg_slabs) == n_slab and t % tm == 0
    has_ab = w_ab is not None
    bounds, lo = [], 0
    for s in seg_slabs:
        bounds.append((lo, lo + s))
        lo += s
    mod_rows = 1 if r == 1 else tm
    mod_map = (lambda b, i, j: (b, 0, 0)) if r == 1 else (lambda b, i, j: (b, i, 0))
    in_specs = [
        pl.BlockSpec((None, tm, D_MODEL), lambda b, i, j: (b, i, 0)),
        pl.BlockSpec((None, mod_rows, 3 * D_MODEL), mod_map),
        pl.BlockSpec((1, D_MODEL), lambda b, i, j: (0, 0)),
        pl.BlockSpec((D_MODEL, slab), lambda b, i, j: (0, j)),
    ]
    args = [x, mod, norm_g.reshape(1, D_MODEL), w]
    if has_ab:
        in_specs.append(pl.BlockSpec((D_MODEL, LANES), lambda b, i, j: (0, 0)))
        args.append(w_ab)
    out_specs, out_shapes = [], []
    for (lo, hi) in bounds:
        def seg_map(b, i, j, lo=lo, hi=hi):
            return (b, i, jnp.clip(j - lo, 0, hi - lo - 1))
        out_specs.append(pl.BlockSpec((None, tm, slab), seg_map))
        out_shapes.append(jax.ShapeDtypeStruct((n, t, slab * (hi - lo)), F32))
    if has_ab:
        out_specs += [pl.BlockSpec((None, tm, LANES), lambda b, i, j: (b, i, 0)),
                      pl.BlockSpec((None, 16, tm), lambda b, i, j: (b, 0, i))]
        out_shapes += [jax.ShapeDtypeStruct((n, t, LANES), F32),
                       jax.ShapeDtypeStruct((n, 16, t), F32)]
    return pl.pallas_call(
        functools.partial(_proj_kernel, tuple(bounds), has_ab),
        grid=(n, t // tm, n_slab),
        in_specs=in_specs,
        out_specs=out_specs,
        out_shape=out_shapes,
        scratch_shapes=[pltpu.VMEM((tm, D_MODEL), BF16)],
        compiler_params=_cparams(("arbitrary", "arbitrary", "arbitrary")),
        name="norm_proj",
    )(*args)


def _mod_norm(x, g, mod):
    ms = jnp.mean(x * x, axis=-1, keepdims=True)
    y = x * lax.rsqrt(ms + EPS) * g
    return y * (1.0 + mod[:, D_MODEL:2 * D_MODEL]) + mod[:, 0:D_MODEL]


def _out_kernel(final_norm, x_ref, o_ref, mod_ref, w_ref, fg_ref, out_ref):
    y = _dot(o_ref[...], w_ref[...])
    gate = mod_ref[:, 2 * D_MODEL:3 * D_MODEL]
    x = x_ref[...] + gate * y
    if final_norm:
        ms = jnp.mean(x * x, axis=-1, keepdims=True)
        x = x * lax.rsqrt(ms + EPS) * fg_ref[...]
    out_ref[...] = x


def _out_project(x, o, mod, w_out, final_g, tm, final_norm):
    n, t, _ = x.shape
    r = mod.shape[1]
    mod_rows = 1 if r == 1 else tm
    mod_map = (lambda b, i: (b, 0, 0)) if r == 1 else (lambda b, i: (b, i, 0))
    row_spec = pl.BlockSpec((None, tm, D_MODEL), lambda b, i: (b, i, 0))
    return pl.pallas_call(
        functools.partial(_out_kernel, final_norm),
        grid=(n, t // tm),
        in_specs=[
            row_spec,
            pl.BlockSpec((None, tm, WIDTH), lambda b, i: (b, i, 0)),
            pl.BlockSpec((None, mod_rows, 3 * D_MODEL), mod_map),
            pl.BlockSpec((WIDTH, D_MODEL), lambda b, i: (0, 0)),
            pl.BlockSpec((1, D_MODEL), lambda b, i: (0, 0)),
        ],
        out_specs=row_spec,
        out_shape=jax.ShapeDtypeStruct((n, t, D_MODEL), F32),
        compiler_params=_cparams(("arbitrary", "arbitrary")),
        name="out_proj",
    )(x, o, mod, w_out, final_g.reshape(1, D_MODEL))


def _unit_lower_inverses(n_mats, c):
    row = lax.broadcasted_iota(jnp.int32, (c, c), 0)
    col = lax.broadcasted_iota(jnp.int32, (c, c), 1)
    base = min(16, c)
    blk_id = lambda v, size: jnp.right_shift(v, size.bit_length() - 1)
    eye = (row == col).astype(F32)
    same = blk_id(row, base) == blk_id(col, base)
    ps = [jnp.where(same, -n, 0.0) for n in n_mats]
    invs = [eye + p for p in ps]
    assert base >= 4
    ps = [_dot(p, p).astype(BF16) for p in ps]
    power = 2
    while 2 * power < base:
        if c % LANES == 0:
            wide = [_dot(p, jnp.concatenate([p, inv.astype(BF16)], axis=1))
                    for p, inv in zip(ps, invs)]
            invs = [inv + w[:, c:] for inv, w in zip(invs, wide)]
            ps = [w[:, 0:c].astype(BF16) for w in wide]
        else:
            invs = [inv + _dot(p, inv) for p, inv in zip(ps, invs)]
            ps = [_dot(p, p).astype(BF16) for p in ps]
        power *= 2
    invs = [inv + _dot(p, inv) for p, inv in zip(ps, invs)]
    size = base
    while size < c:
        same_next = blk_id(row, 2 * size) == blk_id(col, 2 * size)
        off = same_next & jnp.logical_not(same)
        ls = [jnp.where(off, n, 0.0).astype(BF16) for n in n_mats]
        invs_b = [inv.astype(BF16) for inv in invs]
        ts = [_dot(ib, l) for ib, l in zip(invs_b, ls)]
        invs = [inv - _dot(t, ib) for inv, t, ib in zip(invs, ts, invs_b)]
        same = same_next
        size *= 2
    return invs


def _aligned(v, m):
    return v if isinstance(v, int) else pl.multiple_of(v, m)


EXT_SLABS = CONV_DIM // D_HEAD
EXT_LEAD = SUBLANES


def _ext_idx(slab, start, n):
    return (slab // 2, pl.ds(2 * start + slab % 2, n, stride=2), slice(None))


def _ext_shape(rows):
    return (EXT_SLABS // 2, 2 * (rows + EXT_LEAD), D_HEAD)


def _chunk_masks(c):
    row = lax.broadcasted_iota(jnp.int32, (c, c), 0)
    col = lax.broadcasted_iota(jnp.int32, (c, c), 1)
    tril_sel = jnp.where(row >= col, 1.0, 0.0).astype(BF16)
    triu_sel = jnp.where(row <= col, 1.0, 0.0).astype(BF16)
    return row >= col, row > col, tril_sel, triu_sel


def _problems(n_req):
    return [(i, h) for i in range(n_req) for h in range(N_HEADS)]


def _gdn_chunk_prepare(c, r0, masks, requests, cw_ref):
    tril, strict, tril_sel, triu_sel = masks
    pad = EXT_LEAD
    halo = CONV_W - 1
    heads = _problems(len(requests))
    gc_alls, gct_alls, beta_alls = [], [], []
    for _, _, ab_ref, g_ref, gt_ref, _ in requests:
        gc_alls.append(_dot_sel(tril_sel, g_ref[pl.ds(r0, c), :]))
        gct_alls.append(_dot_x_sel(gt_ref[0:N_HEADS, pl.ds(r0, c)], triu_sel))
        beta_alls.append(1.0 / (1.0 + jnp.exp(-ab_ref[pl.ds(r0, c), :])))

    def conv(off, i, h):
        slab = off // D_HEAD + h
        cols = slice(slab * D_HEAD, (slab + 1) * D_HEAD)
        acc = None
        for tap in range(CONV_W):
            rows = _ext_idx(slab, r0 + pad - halo + tap, c)
            term = requests[i][0][rows] * cw_ref[tap:tap + 1, cols]
            acc = term if acc is None else acc + term
        return _silu(acc)

    qs, ks, vs = [], [], []
    for i, h in heads:
        q = conv(0, i, h)
        k = conv(WIDTH, i, h)
        qs.append(q * lax.rsqrt(jnp.sum(q * q, axis=-1, keepdims=True) + EPS)
                  * (D_HEAD ** -0.5))
        ks.append(k * lax.rsqrt(jnp.sum(k * k, axis=-1, keepdims=True) + EPS))
        vs.append(conv(2 * WIDTH, i, h))
    gcs = [jnp.broadcast_to(gc_alls[i][:, h:h + 1], (c, D_HEAD)) for i, h in heads]
    betas = [jnp.broadcast_to(beta_alls[i][:, N_HEADS + h:N_HEADS + h + 1], (c, D_HEAD))
             for i, h in heads]
    g_lasts = [gc[c - 1:c, :] for gc in gcs]
    decays = [jnp.where(tril, jnp.exp2(gc[:, 0:c] - gct_alls[i][h:h + 1, :]), 0.0)
              for gc, (i, h) in zip(gcs, heads)]
    kbs = [k * beta for k, beta in zip(ks, betas)]
    ks_b = [k.astype(BF16) for k in ks]
    kks = [_dot_nt(kb, k) for kb, k in zip(kbs, ks_b)]
    qks = [_dot_nt(q, k) for q, k in zip(qs, ks_b)]
    n_mats = [jnp.where(strict, kk * decay, 0.0) for kk, decay in zip(kks, decays)]
    qks = [(qk * decay).astype(BF16) for qk, decay in zip(qks, decays)]
    invs = _unit_lower_inverses(n_mats, c)
    e_gcs = [jnp.exp2(gc) for gc in gcs]
    rhss = [jnp.concatenate([v * beta, kb * e_gc], axis=1).astype(BF16)
            for v, beta, kb, e_gc in zip(vs, betas, kbs, e_gcs)]
    sols = [_dot(inv, rhs) for inv, rhs in zip(invs, rhss)]
    q_decs = [(q * e_gc).astype(BF16) for q, e_gc in zip(qs, e_gcs)]
    k_decs = [(k * jnp.exp2(g_last - gc)).astype(BF16)
              for k, g_last, gc in zip(ks, g_lasts, gcs)]
    return [(sol[:, 0:D_HEAD], sol[:, D_HEAD:2 * D_HEAD].astype(BF16), qk, q_dec, k_dec,
             jnp.exp2(g_last))
            for sol, qk, q_dec, k_dec, g_last in zip(sols, qks, q_decs, k_decs, g_lasts)]


def _gdn_chunk_apply(c, r0, prepared, requests, og_ref):
    n_req = len(requests)
    heads = _problems(n_req)
    s_olds = [requests[i][5][h] for i, h in heads]
    s_bs = [s.astype(BF16) for s in s_olds]
    on_s = [_dot(jnp.concatenate([k_cum, q_dec], axis=0), s_b)
            for (_, k_cum, _, q_dec, _, _), s_b in zip(prepared, s_bs)]
    us = [w - ks[0:c] for (w, _, _, _, _, _), ks in zip(prepared, on_s)]
    us_b = [u.astype(BF16) for u in us]
    os_ = [ks[c:2 * c] + _dot(qk, u_b)
           for (_, _, qk, _, _, _), ks, u_b in zip(prepared, on_s, us_b)]
    for n, (i, h) in enumerate(heads):
        k_dec, s_decay = prepared[n][4], prepared[n][5]
        requests[i][5][h] = s_olds[n] * s_decay + _dot_tn(k_dec, us_b[n])
    outs = [[] for _ in range(n_req)]
    for n, (i, h) in enumerate(heads):
        o = os_[n]
        o = o * lax.rsqrt(jnp.mean(o * o, axis=-1, keepdims=True) + EPS) * og_ref[...]
        z = requests[i][1][pl.ds(r0, c), h * D_HEAD:(h + 1) * D_HEAD]
        outs[i].append(o * _silu(z))
    return outs


def _decay_logits(ab, alog, dtb):
    return (-LOG2_E * jnp.exp(alog)) * _softplus(ab + dtb)


def _gdn_decode_kernel(c, n_req, qkv_ref, z_ref, ab_ref, abt_ref, cw_ref, alog_ref, dtb_ref,
                       alogc_ref, dtbc_ref, og_ref, s0_ref, conv0_ref,
                       o_ref, s_ref, conv_ref, ext_ref, g_ref, gt_ref):
    halo = CONV_W - 1
    s_ref[...] = s0_ref[...]
    lead_fill = jnp.zeros((EXT_LEAD - halo, D_HEAD), F32)
    for i in range(n_req):
        for slab in range(EXT_SLABS):
            cols = slice(slab * D_HEAD, (slab + 1) * D_HEAD)
            ext_ref[(i,) + _ext_idx(slab, 0, EXT_LEAD)] = jnp.concatenate(
                [lead_fill, conv0_ref[i, :, cols]], axis=0)
            ext_ref[(i,) + _ext_idx(slab, EXT_LEAD, c)] = qkv_ref[i, :, cols]
    g_ref[...] = _decay_logits(ab_ref[...], alog_ref[...], dtb_ref[...])
    gt_ref[...] = _decay_logits(abt_ref[...], alogc_ref[...], dtbc_ref[...])
    requests = [(ext_ref.at[i], z_ref.at[i], ab_ref.at[i], g_ref.at[i], gt_ref.at[i], s_ref.at[i])
                for i in range(n_req)]
    prepared = _gdn_chunk_prepare(c, 0, _chunk_masks(c), requests, cw_ref)
    outs = _gdn_chunk_apply(c, 0, prepared, requests, og_ref)
    for i in range(n_req):
        for h, o in enumerate(outs[i]):
            o_ref[i, :, h * D_HEAD:(h + 1) * D_HEAD] = o
        for slab in range(EXT_SLABS):
            last = ext_ref[(i,) + _ext_idx(slab, c, EXT_LEAD)]
            conv_ref[i, :, slab * D_HEAD:(slab + 1) * D_HEAD] = last[EXT_LEAD - halo:]


def _gdn_core_decode(qkv, z, ab, abt, conv_w, a_log, dt_bias, out_g, s0, conv0, n_req):
    b, c, _ = qkv.shape
    assert b % n_req == 0
    const = lambda shape: pl.BlockSpec(shape, lambda i: (0,) * len(shape))
    per_req = lambda *tail: pl.BlockSpec((n_req,) + tail, lambda i: (i,) + (0,) * len(tail))
    return pl.pallas_call(
        functools.partial(_gdn_decode_kernel, c, n_req),
        grid=(b // n_req,),
        in_specs=[
            per_req(c, CONV_DIM), per_req(c, WIDTH), per_req(c, LANES), per_req(16, c),
            const((CONV_W, CONV_DIM)), const((1, LANES)), const((1, LANES)),
            const((16, 1)), const((16, 1)), const((1, D_HEAD)),
            per_req(N_HEADS, D_HEAD, D_HEAD), per_req(CONV_W - 1, CONV_DIM),
        ],
        out_specs=[per_req(c, WIDTH), per_req(N_HEADS, D_HEAD, D_HEAD),
                   per_req(CONV_W - 1, CONV_DIM)],
        out_shape=[
            jax.ShapeDtypeStruct((b, c, WIDTH), F32),
            jax.ShapeDtypeStruct((b, N_HEADS, D_HEAD, D_HEAD), F32),
            jax.ShapeDtypeStruct((b, CONV_W - 1, CONV_DIM), F32),
        ],
        scratch_shapes=[
            pltpu.VMEM((n_req,) + _ext_shape(c), F32),
            pltpu.VMEM((n_req, c, LANES), F32),
            pltpu.VMEM((n_req, 16, c), F32),
        ],
        compiler_params=_cparams(("arbitrary",)),
        name="gdn_core",
    )(qkv, z, ab, abt, conv_w, *_head_params(a_log, dt_bias), out_g.reshape(1, D_HEAD),
      s0, conv0)


def _head_params(a_log, dt_bias):
    row = lambda v: jnp.pad(v.reshape(1, N_HEADS), ((0, 0), (0, LANES - N_HEADS)))
    col = lambda v: jnp.pad(v.reshape(N_HEADS, 1), ((0, 16 - N_HEADS), (0, 0)))
    return row(a_log), row(dt_bias), col(a_log), col(dt_bias)


def _gdn_layer_kernel(tm, c, x_ref, mod_ref, modn_ref, g_ref, gn_ref, w_ref, wab_ref,
                      wout_ref, cw_ref, alog_ref, dtb_ref, alogc_ref, dtbc_ref, og_ref,
                      x1_ref, h1_ref, s_ref, conv_ref, ext_ref, z_s, ab_s, gl_s, gt_s):
    t = pl.program_id(1)
    pad = EXT_LEAD
    halo = CONV_W - 1
    n_chunks = tm // c
    n_col = 512
    request = (ext_ref, z_s, ab_s, gl_s, gt_s, s_ref)

    @pl.when(t == 0)
    def _():
        s_ref[...] = jnp.zeros_like(s_ref)
        ext_ref[:, 0:2 * pad, :] = jnp.zeros((EXT_SLABS // 2, 2 * pad, D_HEAD), F32)

    def project(r0):
        hb = _mod_norm(x_ref[pl.ds(r0, c), :], g_ref[...], mod_ref[...]).astype(BF16)
        for lo in range(0, CONV_DIM + WIDTH, n_col):
            res = jnp.dot(hb, w_ref[:, lo:lo + n_col], preferred_element_type=F32)
            if lo < CONV_DIM:
                for j in range(n_col // D_HEAD):
                    ext_ref[_ext_idx(lo // D_HEAD + j, pad + r0, c)] = (
                        res[:, j * D_HEAD:(j + 1) * D_HEAD])
            else:
                z_s[pl.ds(r0, c), lo - CONV_DIM:lo - CONV_DIM + n_col] = res
        ab = jnp.dot(hb, wab_ref[...], preferred_element_type=F32)
        ab_s[pl.ds(r0, c), :] = ab
        gl_s[pl.ds(r0, c), :] = _decay_logits(ab, alog_ref[...], dtb_ref[...])
        gt_s[:, pl.ds(r0, c)] = _decay_logits(ab.T[0:16, :], alogc_ref[...], dtbc_ref[...])

    masks = _chunk_masks(c)
    project(0)

    def step(i, carry):
        r0 = pl.multiple_of(i * c, c)
        prepared = _gdn_chunk_prepare(c, r0, masks, [request], cw_ref)
        outs, = _gdn_chunk_apply(c, r0, prepared, [request], og_ref)
        o = jnp.concatenate([o.astype(BF16) for o in outs], axis=1)
        y = jnp.dot(o, wout_ref[...], preferred_element_type=F32)
        x1 = x_ref[pl.ds(r0, c), :] + mod_ref[:, 2 * D_MODEL:3 * D_MODEL] * y
        x1_ref[pl.ds(r0, c), :] = x1
        h1_ref[pl.ds(r0, c), :] = _mod_norm(x1, gn_ref[...], modn_ref[...]).astype(BF16)
        project(pl.multiple_of(jnp.minimum(i + 1, n_chunks - 1) * c, c))
        return carry

    lax.fori_loop(0, n_chunks, step, 0)

    for slab in range(EXT_SLABS):
        last = ext_ref[_ext_idx(slab, tm, pad)]
        ext_ref[_ext_idx(slab, 0, pad)] = last
        conv_ref[:, slab * D_HEAD:(slab + 1) * D_HEAD] = last[pad - halo:]


def _gdn_layer_prompt(x, mod, mod_next, norm_g, norm_g_next, w_main, w_ab, w_out, conv_w,
                      a_log, dt_bias, out_g, tm, c):
    b, t, _ = x.shape
    const = lambda shape: pl.BlockSpec(shape, lambda i, j: (0,) * len(shape))
    resident = lambda shape: pl.BlockSpec(shape, lambda i, j: (0,) * len(shape),
                                          pipeline_mode=pl.Buffered(1))
    mod_spec = pl.BlockSpec((None, 1, 3 * D_MODEL), lambda i, j: (i, 0, 0))
    row_spec = pl.BlockSpec((None, tm, D_MODEL), lambda i, j: (i, j, 0))
    return pl.pallas_call(
        functools.partial(_gdn_layer_kernel, tm, c),
        grid=(b, t // tm),
        in_specs=[
            row_spec, mod_spec, mod_spec, const((1, D_MODEL)), const((1, D_MODEL)),
            resident((D_MODEL, CONV_DIM + WIDTH)), const((D_MODEL, LANES)),
            resident((WIDTH, D_MODEL)), const((CONV_W, CONV_DIM)),
            const((1, LANES)), const((1, LANES)), const((16, 1)), const((16, 1)),
            const((1, D_HEAD)),
        ],
        out_specs=[
            row_spec, row_spec,
            pl.BlockSpec((None, N_HEADS, D_HEAD, D_HEAD), lambda i, j: (i, 0, 0, 0)),
            pl.BlockSpec((None, CONV_W - 1, CONV_DIM), lambda i, j: (i, 0, 0)),
        ],
        out_shape=[
            jax.ShapeDtypeStruct((b, t, D_MODEL), F32),
            jax.ShapeDtypeStruct((b, t, D_MODEL), BF16),
            jax.ShapeDtypeStruct((b, N_HEADS, D_HEAD, D_HEAD), F32),
            jax.ShapeDtypeStruct((b, CONV_W - 1, CONV_DIM), F32),
        ],
        scratch_shapes=[
            pltpu.VMEM(_ext_shape(tm), F32),
            pltpu.VMEM((tm, WIDTH), F32),
            pltpu.VMEM((tm, LANES), F32),
            pltpu.VMEM((tm, LANES), F32),
            pltpu.VMEM((16, tm), F32),
        ],
        compiler_params=_cparams(("arbitrary", "arbitrary")),
        name="gdn_layer",
    )(x, mod, mod_next, norm_g.reshape(1, D_MODEL), norm_g_next.reshape(1, D_MODEL),
      w_main, w_ab, w_out, conv_w, *_head_params(a_log, dt_bias), out_g.reshape(1, D_HEAD))


def _banded_attention(dil, seq, heads, normalized):
    blk = D_HEAD
    n_blk = seq // dil // blk
    with_prev = n_blk > 1
    per_step = 16 if with_prev else 8
    n_keys = 2 * blk if with_prev else blk
    row = lax.broadcasted_iota(jnp.int32, (blk, n_keys), 0)
    col = lax.broadcasted_iota(jnp.int32, (blk, n_keys), 1)
    is_prev = col < (n_keys - blk)
    slack = jnp.where(is_prev, col - row, row + (n_keys - blk) - col)
    cur_ok = slack >= 0
    ones = jnp.ones((n_keys, blk), BF16)

    def rows(start):
        if dil == 1:
            return pl.ds(start, blk)
        return pl.ds(start, blk, stride=dil)

    def body(i, carry):
        dests, masks, qs, kcs, vcs = [], [], [], [], []
        for q_ref, k_ref, v_ref, acc_ref, m_ref, l_ref in heads:
            for u in range(per_step):
                idx = i * per_step + u
                r = idx // n_blk
                n = idx % n_blk
                start = r + dil * blk * n
                dests.append((start, acc_ref, m_ref, l_ref))
                qs.append(q_ref[rows(start), :].astype(BF16))
                k_cur = k_ref[rows(start), :].astype(BF16)
                v_cur = v_ref[rows(start), :].astype(BF16)
                if with_prev:
                    if u % min(n_blk, per_step) != 0:
                        k_prev, v_prev = k_last, v_last
                    else:
                        start_p = r + dil * blk * jnp.maximum(n - 1, 0)
                        k_prev = k_ref[rows(start_p), :].astype(BF16)
                        v_prev = v_ref[rows(start_p), :].astype(BF16)
                    k_last, v_last = k_cur, v_cur
                    first = jnp.where(n > 0, 0, 2 * blk)
                    masks.append(slack - jnp.where(is_prev, first, 0) >= 0)
                    k_cur = jnp.concatenate([k_prev, k_cur], axis=0)
                    v_cur = jnp.concatenate([v_prev, v_cur], axis=0)
                else:
                    masks.append(cur_ok)
                kcs.append(k_cur)
                vcs.append(jnp.concatenate([v_cur, ones], axis=1))
        ss = [jnp.where(mask, _dot_nt(q, kc), NEG_BIG) for q, kc, mask in zip(qs, kcs, masks)]
        ms = [jnp.max(s, axis=-1, keepdims=True) for s in ss]
        ps = [jnp.exp2(s - m).astype(BF16) for s, m in zip(ss, ms)]
        accs = [jnp.dot(p, vc, preferred_element_type=F32) for p, vc in zip(ps, vcs)]
        for (start, acc_ref, m_ref, l_ref), acc, m in zip(dests, accs, ms):
            if normalized:
                l = acc[:, blk:2 * blk]
                acc_ref[rows(start), :] = acc[:, 0:blk] / l
                m_ref[rows(start), :] = m + jnp.log2(l)
            else:
                acc_ref[rows(start), :] = acc[:, 0:blk]
                l_ref[rows(start), :] = acc[:, blk:2 * blk]
                m_ref[rows(start), :] = jnp.broadcast_to(m, (blk, blk))
        return carry

    lax.fori_loop(0, dil * n_blk // per_step, body, 0)


def _attn_layer_kernel(seq, h_ref, *refs):
    n_g = len(DIL_GROUPS)
    w_refs = refs[:3 * n_g + 1]
    (o_ref, kv0_ref, kv1_ref, kv2_ref, q_s, k_s, v_s, z_s, acc_s, m_s, l_s, acc2_s,
     sems) = refs[3 * n_g + 1:]
    b = pl.program_id(0)
    hp = pl.program_id(1)
    kv_refs = (kv0_ref, kv1_ref, kv2_ref)

    def proj(slab):
        return jnp.dot(h_ref[...], w_refs[slab][...], preferred_element_type=F32)

    def kv_copy(g, i, kv):
        keep = kv_refs[g].shape[1]
        src = (k_s, v_s)[kv].at[g, i, pl.ds(seq - keep, keep), :]
        col = pl.multiple_of(kv * WIDTH + (2 * hp + i) * D_HEAD, D_HEAD)
        dst = kv_refs[g].at[b, :, pl.ds(col, D_HEAD)]
        return pltpu.make_async_copy(src, dst, sems.at[(g * 2 + i) * 2 + kv])

    for g in range(n_g):
        k = proj(n_g + g)
        v = proj(2 * n_g + g)
        for i in range(2):
            k_s[g, i] = k[:, i * D_HEAD:(i + 1) * D_HEAD]
            v_s[g, i] = v[:, i * D_HEAD:(i + 1) * D_HEAD]
        for i in range(2):
            kv_copy(g, i, 0).start()
            kv_copy(g, i, 1).start()
        q = proj(g) * (ATTN_SCALE * LOG2_E)
        for i in range(2):
            q_s[g, i] = q[:, i * D_HEAD:(i + 1) * D_HEAD]

    normalized = [dil % 16 == 0 for _, dil in DIL_GROUPS]
    shared = n_g - 1
    assert normalized[shared]
    results = [[(acc_s.at[g], m_s.at[g], l_s.at[g]) for g in range(n_g)] for _ in range(2)]
    results[1][shared] = (acc2_s.at[0], l_s.at[shared], l_s.at[shared])

    def head(i, g):
        return (q_s.at[g, i], k_s.at[g, i], v_s.at[g, i]) + results[i][g]

    def combine(i):
        accs = [results[i][g][0][...] for g in range(n_g)]
        ms = [results[i][g][1][...] for g in range(n_g)]
        m_all = jnp.maximum(jnp.maximum(ms[0], ms[1]), ms[2])
        num = jnp.zeros((seq, D_HEAD), F32)
        den = jnp.zeros((seq, D_HEAD), F32)
        for g in range(n_g):
            w = jnp.exp2(ms[g] - m_all)
            num = num + w * accs[g]
            den = den + (w if normalized[g] else w * results[i][g][2][...])
        o_ref[:, i * D_HEAD:(i + 1) * D_HEAD] = (num / den * _silu(z_s[i])).astype(BF16)

    for g in range(shared):
        _banded_attention(DIL_GROUPS[g][1], seq, [head(0, g)], normalized[g])
    _banded_attention(DIL_GROUPS[shared][1], seq, [head(0, shared), head(1, shared)], True)
    z = proj(3 * n_g)
    for j in range(2):
        z_s[j] = z[:, j * D_HEAD:(j + 1) * D_HEAD]
    combine(0)
    for g in range(shared):
        _banded_attention(DIL_GROUPS[g][1], seq, [head(1, g)], normalized[g])
    combine(1)

    for g in range(n_g):
        for i in range(2):
            kv_copy(g, i, 0).wait()
            kv_copy(g, i, 1).wait()


def _attn_layer_prompt(h, w):
    b, seq, _ = h.shape
    n_pair = N_HEADS // 2
    n_g = len(DIL_GROUPS)
    pair = 2 * D_HEAD
    keeps = [min(window, seq) for window, _ in DIL_GROUPS]
    slab = lambda n: pltpu.VMEM((n, seq, D_HEAD), F32)
    w_specs = [pl.BlockSpec((D_MODEL, pair), lambda i, p, s=s: (0, s * n_pair + p))
               for s in range(3 * n_g + 1)]
    return pl.pallas_call(
        functools.partial(_attn_layer_kernel, seq),
        grid=(b, n_pair),
        in_specs=[
            pl.BlockSpec((None, seq, D_MODEL), lambda i, p: (i, 0, 0),
                         pipeline_mode=pl.Buffered(1)),
        ] + w_specs,
        out_specs=[pl.BlockSpec((None, seq, 2 * D_HEAD), lambda i, p: (i, 0, p))]
        + [pl.BlockSpec(memory_space=pl.ANY)] * n_g,
        out_shape=[jax.ShapeDtypeStruct((b, seq, WIDTH), BF16)]
        + [jax.ShapeDtypeStruct((b, keep, 2 * WIDTH), F32) for keep in keeps],
        scratch_shapes=[
            pltpu.VMEM((n_g, 2, seq, D_HEAD), F32),
            pltpu.VMEM((n_g, 2, seq, D_HEAD), F32),
            pltpu.VMEM((n_g, 2, seq, D_HEAD), F32),
            slab(2),
            slab(n_g), slab(n_g), slab(n_g),
            slab(1),
            pltpu.SemaphoreType.DMA((n_g * 2 * 2,)),
        ],
        compiler_params=_cparams(("arbitrary", "arbitrary")),
        name="attn_layer",
    )(h, *([w] * (3 * n_g + 1)))


def _attn_sample_kernel(n_new, *refs):
    q_ref, kv0_ref, kv1_ref, kv2_ref, z_ref, c0_ref, c1_ref, c2_ref, o_ref = refs
    n_t = KEYS_PER_QUERY
    ones = jnp.ones((D_HEAD, D_HEAD), BF16)
    dil1 = DIL_GROUPS[1][1]

    def key_tiles(g, l, kv):
        if g == 0:
            return jnp.concatenate([c0_ref[l:, kv], kv0_ref[0:l + 1, kv]], axis=0)
        if g == 1:
            a, r = divmod(l, dil1)
            parts = [c1_ref[a:, r, kv]]
            if a:
                parts.append(kv1_ref[r:r + 1, kv])
            parts.append(kv1_ref[l:l + 1, kv])
            return jnp.concatenate(parts, axis=0)
        return jnp.concatenate([c2_ref[:, l, kv], kv2_ref[l:l + 1, kv]], axis=0)

    for l in range(n_new):
        m_g, l_g, acc_g = [], [], []
        for g in range(len(DIL_GROUPS)):
            q = q_ref[l, g] * (ATTN_SCALE * LOG2_E)
            prod = (key_tiles(g, l, 0) * q[None]).reshape(n_t * N_HEADS, D_HEAD)
            s = jnp.dot(prod.astype(BF16), ones, preferred_element_type=F32)
            s = s.reshape(n_t, N_HEADS, D_HEAD)
            m = jnp.max(s, axis=0)
            p = jnp.exp2(s - m[None])
            m_g.append(m)
            l_g.append(jnp.sum(p, axis=0))
            acc_g.append(jnp.sum(p * key_tiles(g, l, 1), axis=0))
        m_all = jnp.maximum(jnp.maximum(m_g[0], m_g[1]), m_g[2])
        w_g = [jnp.exp2(m - m_all) for m in m_g]
        num = w_g[0] * acc_g[0] + w_g[1] * acc_g[1] + w_g[2] * acc_g[2]
        den = w_g[0] * l_g[0] + w_g[1] * l_g[1] + w_g[2] * l_g[2]
        o_ref[l] = num / den * _silu(z_ref[l])


def _attn_sample(q, kvs, z, caches):
    b, n_new = q.shape[:2]
    assert n_new == SUBLANES
    (w0, d0), (w1, d1), (w2, d2) = DIL_GROUPS
    assert caches[0].shape[1] == w0 and caches[1].shape[1] == w1 and caches[2].shape[1] == w2
    assert d0 == 1 and n_new % d1 == 0 and d2 == 2 * n_new
    assert w0 // d0 + 1 == KEYS_PER_QUERY and w1 // d1 + 1 == KEYS_PER_QUERY
    assert w2 // d2 + 1 == KEYS_PER_QUERY
    c1 = caches[1].reshape(b, w1 // d1, d1, 2, N_HEADS, D_HEAD)
    c2 = caches[2].reshape(b, w2 // d2, d2, 2, N_HEADS, D_HEAD)
    tail = (2, N_HEADS, D_HEAD)
    new_spec = pl.BlockSpec((None, n_new) + tail, lambda i: (i, 0, 0, 0, 0))
    return pl.pallas_call(
        functools.partial(_attn_sample_kernel, n_new),
        grid=(b,),
        in_specs=[
            pl.BlockSpec((None, n_new, 3, N_HEADS, D_HEAD), lambda i: (i, 0, 0, 0, 0)),
            new_spec, new_spec, new_spec,
            pl.BlockSpec((None, n_new, N_HEADS, D_HEAD), lambda i: (i, 0, 0, 0)),
            pl.BlockSpec((None, w0) + tail, lambda i: (i, 0, 0, 0, 0)),
            pl.BlockSpec((None, w1 // d1, d1) + tail, lambda i: (i, 0, 0, 0, 0, 0)),
            pl.BlockSpec((None, w2 // d2, n_new) + tail, lambda i: (i, 0, 0, 0, 0, 0)),
        ],
        out_specs=pl.BlockSpec((None, n_new, N_HEADS, D_HEAD), lambda i: (i, 0, 0, 0)),
        out_shape=jax.ShapeDtypeStruct((b, n_new, N_HEADS, D_HEAD), F32),
        compiler_params=_cparams(("arbitrary",)),
        name="attn_sample",
    )(q, kvs[0], kvs[1], kvs[2], z, caches[0], c1, c2)


def kernel(x_prompt, x_sample, state_delta, state_conv, cache_kv_w128, cache_kv_w512, cache_kv_w2048,
           c_prompt, c_sample, norm_g, ada_w, ada_b, a_w_in, a_conv_w, a_A_log, a_dt_bias,
           a_out_norm_g, a_w_out, b_w_in, b_w_out, final_norm_g):
    bp, seq, _ = x_prompt.shape
    bs, n_new, _ = x_sample.shape
    n_s = bs * n_new

    mod = _ada_mod(jnp.concatenate([c_prompt, c_sample], axis=0), ada_w, ada_b)
    mod_p = [mod[l, :bp].reshape(bp, 1, 3 * D_MODEL) for l in range(2)]
    mod_s = [jnp.repeat(mod[l, bp:], n_new, axis=0).reshape(1, n_s, 3 * D_MODEL) for l in range(2)]

    w_main = a_w_in[0].astype(BF16)
    w_ab = jnp.pad(w_main[:, CONV_DIM + WIDTH:], ((0, 0), (0, LANES - 2 * N_HEADS)))
    w_out_a = a_w_out[0].astype(BF16)
    xs_flat = x_sample.reshape(1, n_s, D_MODEL)

    x1_p, h1_p, delta_p, conv_p = _gdn_layer_prompt(
        x_prompt, mod_p[0], mod_p[1], norm_g[0], norm_g[1], w_main, w_ab, w_out_a,
        a_conv_w[0], a_A_log[0], a_dt_bias[0], a_out_norm_g[0], 1024, 128)

    qkv_s, z_s, ab_s, abt_s = _project(xs_flat, mod_s[0], norm_g[0], w_main, (3, 1), n_s, w_ab)
    abt_s = abt_s.reshape(16, bs, n_new).transpose(1, 0, 2)
    o_s, delta_s, conv_s = _gdn_core_decode(
        qkv_s.reshape(bs, n_new, CONV_DIM), z_s.reshape(bs, n_new, WIDTH),
        ab_s.reshape(bs, n_new, LANES), abt_s, a_conv_w[0], a_A_log[0], a_dt_bias[0],
        a_out_norm_g[0], state_delta[0], state_conv[0], 8)
    x1_s = _out_project(xs_flat, o_s.reshape(1, n_s, WIDTH), mod_s[0], w_out_a, final_norm_g,
                        n_s, False)

    n_g = len(DIL_GROUPS)
    w_b = b_w_in[0].astype(BF16)
    w_out_b = b_w_out[0].astype(BF16)

    ob_p, kv0_p, kv1_p, kv2_p = _attn_layer_prompt(h1_p, w_b)
    y_p = _out_project(x1_p, ob_p, mod_p[1], w_out_b, final_norm_g, 512, True)

    q_s, k_s, v_s, zb_s = _project(x1_s, mod_s[1], norm_g[1], w_b, (n_g, n_g, n_g, 1), n_s)
    k_s = k_s.reshape(bs, n_new, n_g, N_HEADS, D_HEAD)
    v_s = v_s.reshape(bs, n_new, n_g, N_HEADS, D_HEAD)
    kvn = [jnp.stack([k_s[:, :, g], v_s[:, :, g]], axis=2) for g in range(n_g)]
    ob_s = _attn_sample(q_s.reshape(bs, n_new, n_g, N_HEADS, D_HEAD), kvn,
                        zb_s.reshape(bs, n_new, N_HEADS, D_HEAD),
                        (cache_kv_w128[0], cache_kv_w512[0], cache_kv_w2048[0]))
    y_s = _out_project(x1_s, ob_s.reshape(1, n_s, WIDTH), mod_s[1], w_out_b, final_norm_g,
                       n_s, True)

    def kv_prompt(kv):
        return kv.reshape(1, bp, kv.shape[1], 2, N_HEADS, D_HEAD)

    return (y_p, y_s.reshape(bs, n_new, D_MODEL),
            delta_p[None], delta_s[None], conv_p[None], conv_s[None],
            kv_prompt(kv0_p), kvn[0][None], kv_prompt(kv1_p), kvn[1][None],
            kv_prompt(kv2_p), kvn[2][None])
```
